```python
import jax, jax.numpy as jnp
from jax import lax
import numpy as np

D_MODEL = 1024
BATCH = 2
SEQ = 8192
DEPTH = 4
DEC_BATCH = 32
DEC_SEQ = 4
PAST_LEN = 8192
PAGE_SIZE = 128

D_PLE = 256
N_A_LAYERS = DEPTH // 2
N_B_LAYERS = DEPTH - N_A_LAYERS
EPS = 1e-6

SSM_EXPAND = 2
D_INNER = SSM_EXPAND * D_MODEL
SSM_HEADDIM = 64
SSM_HEADS = D_INNER // SSM_HEADDIM
SSM_GROUPS = 4
SSM_HPG = SSM_HEADS // SSM_GROUPS
D_STATE = 128
D_CONV = 4
CONV_DIM = D_INNER + 2 * SSM_GROUPS * D_STATE
IN_PROJ_DIM = D_INNER + CONV_DIM + SSM_HEADS
SSD_CHUNK = 128

HEAD_DIM = 128
N_KV_HEADS = D_MODEL // HEAD_DIM
DILATED_PATTERNS = ((128, 1), (512, 4), (2048, 16))
N_PATTERNS = len(DILATED_PATTERNS)
Q_DIM = N_PATTERNS * N_KV_HEADS * HEAD_DIM
ATTN_DIM = N_KV_HEADS * HEAD_DIM
WINDOW_MAX = 2048
ROPE_THETA = 10000.0

N_EXPERT_GROUPS = 4
EXPERTS_PER_GROUP = 8
N_EXPERTS = N_EXPERT_GROUPS * EXPERTS_PER_GROUP
TOP_K_IN_GROUP = 2
D_EXPERT = 512
MOE_BLOCK = 1024

kernel_name = 'yoco_mamba2_dilated_hmoe_step'


def rms_norm(x, gain):
    xf = x.astype(jnp.float32)
    y = xf * lax.rsqrt(jnp.mean(xf * xf, axis=-1, keepdims=True) + EPS)
    return (y * gain.astype(jnp.float32)).astype(x.dtype)


def rope(x, pos):
    dh = x.shape[-1]
    inv_freq = ROPE_THETA ** (-jnp.arange(0, dh, 2, dtype=jnp.float32) / dh)
    ang = pos.astype(jnp.float32)[:, None] * inv_freq[None, :]
    cos = jnp.cos(ang)[None, :, None, :]
    sin = jnp.sin(ang)[None, :, None, :]
    xf = x.astype(jnp.float32)
    x1, x2 = xf[..., : dh // 2], xf[..., dh // 2:]
    return jnp.concatenate([x1 * cos - x2 * sin, x2 * cos + x1 * sin], axis=-1).astype(x.dtype)


def causal_depthwise_conv(u, buf, w, bias):
    l = u.shape[1]
    up = jnp.concatenate([buf.astype(u.dtype), u], axis=1)
    out = bias + sum(w[k] * up[:, k:k + l] for k in range(D_CONV))
    return out, up[:, l:]


def ssd_scan(x, dt, a, bm, cm, h0, chunk):
    b, l = x.shape[:2]
    c = l // chunk
    xf = x.astype(jnp.float32).reshape(b, c, chunk, SSM_GROUPS, SSM_HPG, SSM_HEADDIM)
    dtc = dt.reshape(b, c, chunk, SSM_GROUPS, SSM_HPG)
    a_cs = jnp.cumsum(dtc * a.reshape(SSM_GROUPS, SSM_HPG), axis=2)
    xdt = xf * dtc[..., None]
    bc = bm.astype(jnp.float32).reshape(b, c, chunk, SSM_GROUPS, D_STATE)
    cc = cm.astype(jnp.float32).reshape(b, c, chunk, SSM_GROUPS, D_STATE)
    seg = a_cs[:, :, :, None] - a_cs[:, :, None, :]
    causal = jnp.tril(jnp.ones((chunk, chunk), dtype=bool))[:, :, None, None]
    decay = jnp.exp(jnp.where(causal, seg, -jnp.inf))
    cb = jnp.einsum('bclgn,bcsgn->bclsg', cc, bc)
    y_diag = jnp.einsum('bclsg,bclsgj,bcsgjp->bclgjp', cb, decay, xdt)
    to_end = jnp.exp(a_cs[:, :, -1:] - a_cs)
    states = jnp.einsum('bclgn,bclgj,bclgjp->bcgjpn', bc, to_end, xdt)
    chunk_decay = jnp.exp(a_cs[:, :, -1])

    def step(h, inp):
        st, dec = inp
        return h * dec[..., None, None] + st, h

    h_init = h0.astype(jnp.float32).reshape(b, SSM_GROUPS, SSM_HPG, SSM_HEADDIM, D_STATE)
    h_fin, h_prev = lax.scan(step, h_init, (jnp.moveaxis(states, 1, 0), jnp.moveaxis(chunk_decay, 1, 0)))
    h_prev = jnp.moveaxis(h_prev, 0, 1)
    y_off = jnp.einsum('bclgn,bcgjpn,bclgj->bclgjp', cc, h_prev, jnp.exp(a_cs))
    y = (y_diag + y_off).reshape(b, l, SSM_HEADS, SSM_HEADDIM)
    return y, h_fin.reshape(b, SSM_HEADS, SSM_HEADDIM, D_STATE)


def mamba2_mixer(h, conv_buf, h0, in_proj, conv_w, conv_b, dt_bias, a_log, d_skip, norm_y, out_proj):
    b, l, _ = h.shape
    zxbcdt = h @ in_proj
    z = zxbcdt[..., :D_INNER]
    xbc = zxbcdt[..., D_INNER:D_INNER + CONV_DIM]
    dt = zxbcdt[..., D_INNER + CONV_DIM:]
    xbc, new_buf = causal_depthwise_conv(xbc, conv_buf, conv_w, conv_b)
    xbc = jax.nn.silu(xbc)
    xs = xbc[..., :D_INNER].reshape(b, l, SSM_HEADS, SSM_HEADDIM)
    bm = xbc[..., D_INNER:D_INNER + SSM_GROUPS * D_STATE].reshape(b, l, SSM_GROUPS, D_STATE)
    cm = xbc[..., D_INNER + SSM_GROUPS * D_STATE:].reshape(b, l, SSM_GROUPS, D_STATE)
    dt = jax.nn.softplus(dt.astype(jnp.float32) + dt_bias.astype(jnp.float32))
    a = -jnp.exp(a_log.astype(jnp.float32))
    y, h_fin = ssd_scan(xs, dt, a, bm, cm, h0, min(SSD_CHUNK, l))
    y = y + d_skip.astype(jnp.float32)[:, None] * xs.astype(jnp.float32)
    y = y.reshape(b, l, D_INNER).astype(h.dtype)
    y = rms_norm(y * jax.nn.silu(z), norm_y)
    return y @ out_proj, new_buf, h_fin.astype(h0.dtype)


def dilated_band_prompt(q, k, v, window, dilation):
    b, s, h, dh = q.shape
    reach = window // dilation
    n = s // dilation
    nb = -(-n // reach)
    n_pad = nb * reach
    bd = b * dilation

    def lanes(a):
        a = a.reshape(b, n, dilation, h, dh).transpose(0, 2, 1, 3, 4).reshape(bd, n, h, dh)
        return jnp.pad(a, ((0, 0), (0, n_pad - n), (0, 0), (0, 0))).reshape(bd, nb, reach, h, dh)

    def with_prev(a):
        prev = jnp.pad(a[:, :-1], ((0, 0), (1, 0), (0, 0), (0, 0), (0, 0)))
        return jnp.concatenate([prev, a], axis=2)

    qb = lanes(q)
    kc = with_prev(lanes(k))
    vc = with_prev(lanes(v))
    scores = jnp.einsum('bcqhd,bckhd->bchqk', qb, kc, preferred_element_type=jnp.float32) * (HEAD_DIM ** -0.5)
    qi = jnp.arange(reach)[:, None]
    ki = jnp.arange(2 * reach)[None, :]
    dist = qi + reach - ki
    band = (dist >= 0) & (dist <= reach)
    has_prev = (jnp.arange(nb)[:, None, None] > 0) | (ki >= reach)[None]
    mask = band[None] & has_prev
    scores = jnp.where(mask[None, :, None], scores, -jnp.inf)
    lse = jax.nn.logsumexp(scores, axis=-1)
    p = jnp.exp(scores - lse[..., None])
    o = jnp.einsum('bchqk,bckhd->bcqhd', p.astype(v.dtype), vc)
    o = o.reshape(bd, n_pad, h, dh)[:, :n].reshape(b, dilation, n, h, dh).transpose(0, 2, 1, 3, 4).reshape(b, s, h, dh)
    lse = lse.transpose(0, 1, 3, 2).reshape(bd, n_pad, h)[:, :n].reshape(b, dilation, n, h).transpose(0, 2, 1, 3).reshape(b, s, h)
    return o, lse


def dilated_gather_sample(q, kbuf, vbuf, window, dilation):
    b, t, h, dh = q.shape
    wb = kbuf.shape[1] - t
    offs = jnp.arange(window // dilation + 1) * dilation
    idx = wb + jnp.arange(t)[:, None] - offs[None, :]
    valid = idx >= 0
    idx = jnp.maximum(idx, 0)
    kg = kbuf[:, idx]
    vg = vbuf[:, idx]
    scores = jnp.einsum('bthd,btjhd->bthj', q, kg, preferred_element_type=jnp.float32) * (HEAD_DIM ** -0.5)
    scores = jnp.where(valid[None, :, None, :], scores, -jnp.inf)
    lse = jax.nn.logsumexp(scores, axis=-1)
    p = jnp.exp(scores - lse[..., None])
    o = jnp.einsum('bthj,btjhd->bthd', p.astype(vbuf.dtype), vg)
    return o, lse


def dilated_mixer(h, w_q, w_o, k_all, v_all, pos, from_cache):
    b, l, _ = h.shape
    q = rope((h @ w_q).reshape(b, l, N_PATTERNS * N_KV_HEADS, HEAD_DIM), pos)
    q = q.reshape(b, l, N_PATTERNS, N_KV_HEADS, HEAD_DIM)
    outs, lses = [], []
    for g, (window, dilation) in enumerate(DILATED_PATTERNS):
        if from_cache:
            o, lse = dilated_gather_sample(q[:, :, g], k_all, v_all, window, dilation)
        else:
            o, lse = dilated_band_prompt(q[:, :, g], k_all, v_all, window, dilation)
        outs.append(o)
        lses.append(lse)
    wts = jax.nn.softmax(jnp.stack(lses, axis=0), axis=0)
    o = jnp.einsum('gblh,gblhd->blhd', wts, jnp.stack(outs, axis=0).astype(jnp.float32)).astype(h.dtype)
    return o.reshape(b, l, ATTN_DIM) @ w_o


def shared_kv(x, norm_kv, w_kv, pos):
    b, l, _ = x.shape
    kv = rms_norm(x, norm_kv) @ w_kv
    k = rope(kv[..., :ATTN_DIM].reshape(b, l, N_KV_HEADS, HEAD_DIM), pos)
    v = kv[..., ATTN_DIM:].reshape(b, l, N_KV_HEADS, HEAD_DIM)
    return k, v


def hier_moe(h, w_rg, b_rg, w_re, b_re, w_gate, w_up, w_down):
    b, l, d = h.shape
    n = b * l
    xt = h.reshape(n, d)
    rows = jnp.arange(n)
    g_logits = (xt @ w_rg).astype(jnp.float32) + b_rg.astype(jnp.float32)
    g_prob = jax.nn.softmax(g_logits, axis=-1)
    _, g_idx = lax.top_k(g_logits, 1)
    g_sel = g_idx[:, 0]
    g_w = g_prob[rows, g_sel]
    e_logits = ((xt @ w_re).astype(jnp.float32) + b_re.astype(jnp.float32)).reshape(n, N_EXPERT_GROUPS, EXPERTS_PER_GROUP)
    e_top, e_idx = lax.top_k(e_logits[rows, g_sel], TOP_K_IN_GROUP)
    e_w = jax.nn.softmax(e_top, axis=-1) * g_w[:, None]
    expert_id = g_sel[:, None] * EXPERTS_PER_GROUP + e_idx
    gates = jnp.einsum('nke,nk->ne', jax.nn.one_hot(expert_id, N_EXPERTS, dtype=jnp.float32), e_w)
    blk = min(MOE_BLOCK, n)
    nblk = -(-n // blk)
    pad = nblk * blk - n
    xp = jnp.pad(xt, ((0, pad), (0, 0))).reshape(nblk, blk, d)
    gp = jnp.pad(gates, ((0, pad), (0, 0))).reshape(nblk, blk, N_EXPERTS)

    def block(args):
        xb, gb = args
        hid = jax.nn.silu(jnp.einsum('nd,edf->nef', xb, w_gate)) * jnp.einsum('nd,edf->nef', xb, w_up)
        return jnp.einsum('nef,efd->nd', hid * gb.astype(hid.dtype)[..., None], w_down)

    y = lax.map(block, (xp, gp))
    return y.reshape(nblk * blk, d)[:n].reshape(b, l, d)


def per_layer_embedding(x, p_i, w_proj, w_gate, gain):
    gate = jax.nn.sigmoid(rms_norm(x, gain) @ w_gate)
    return (p_i @ w_proj) * gate


def trunk(x, p, pos0, conv_state, ssm_state, cache_k, cache_v, prm):
    b, l, _ = x.shape
    pos = pos0 + jnp.arange(l, dtype=jnp.int32)
    from_cache = cache_k is not None
    new_conv, new_ssm = [], []
    k_all = v_all = k_new = v_new = None
    for i in range(DEPTH):
        h = rms_norm(x, prm['norm_mix'][i])
        if i < N_A_LAYERS:
            y, cbuf, hs = mamba2_mixer(h, conv_state[i], ssm_state[i], prm['ssm_in_proj'][i], prm['ssm_conv_w'][i],
                                       prm['ssm_conv_b'][i], prm['ssm_dt_bias'][i], prm['ssm_a_log'][i],
                                       prm['ssm_d'][i], prm['ssm_norm_y'][i], prm['ssm_out_proj'][i])
            new_conv.append(cbuf)
            new_ssm.append(hs)
        else:
            j = i - N_A_LAYERS
            y = dilated_mixer(h, prm['attn_w_q'][j], prm['attn_w_o'][j], k_all, v_all, pos, from_cache)
        x = x + y
        x = x + hier_moe(rms_norm(x, prm['norm_ffn'][i]), prm['moe_w_rg'][i], prm['moe_b_rg'][i], prm['moe_w_re'][i],
                         prm['moe_b_re'][i], prm['moe_w_gate'][i], prm['moe_w_up'][i], prm['moe_w_down'][i])
        x = x + per_layer_embedding(x, p[i], prm['ple_w_proj'][i], prm['ple_w_gate'][i], prm['ple_norm'][i])
        if i == N_A_LAYERS - 1:
            k_new, v_new = shared_kv(x, prm['norm_kv'], prm['w_kv'], pos)
            if from_cache:
                k_all = jnp.concatenate([cache_k.astype(k_new.dtype), k_new], axis=1)
                v_all = jnp.concatenate([cache_v.astype(v_new.dtype), v_new], axis=1)
            else:
                k_all, v_all = k_new, v_new
    return rms_norm(x, prm['norm_final']), jnp.stack(new_conv), jnp.stack(new_ssm), k_new, v_new


def setup_inputs(seed: int = 0) -> dict:
    key = jax.random.key(seed)
    ks = iter(jax.random.split(key, 48))

    def nrm(shape, scale):
        return jax.random.normal(next(ks), shape, jnp.float32) * scale

    def gain(shape):
        return 1.0 + nrm(shape, 0.02)

    win_buf = min(WINDOW_MAX, PAST_LEN)
    dt0 = jnp.exp(jax.random.uniform(next(ks), (N_A_LAYERS, SSM_HEADS), jnp.float32,
                                     float(np.log(1e-3)), float(np.log(1e-1))))
    dt_bias = dt0 + jnp.log(-jnp.expm1(-dt0))
    a_log = jnp.log(jax.random.uniform(next(ks), (N_A_LAYERS, SSM_HEADS), jnp.float32, 1.0, 16.0))
    return {
        'x_prompt': nrm((BATCH, SEQ, D_MODEL), 1.0),
        'x_sample': nrm((DEC_BATCH, DEC_SEQ, D_MODEL), 1.0),
        'state_ssm': nrm((N_A_LAYERS, DEC_BATCH, SSM_HEADS, SSM_HEADDIM, D_STATE), 0.1),
        'state_conv': nrm((N_A_LAYERS, DEC_BATCH, D_CONV - 1, CONV_DIM), 1.0),
        'cache_k': nrm((DEC_BATCH, win_buf, N_KV_HEADS, HEAD_DIM), 1.0),
        'cache_v': nrm((DEC_BATCH, win_buf, N_KV_HEADS, HEAD_DIM), 1.0),
        'p_prompt': nrm((DEPTH, BATCH, SEQ, D_PLE), 1.0),
        'p_sample': nrm((DEPTH, DEC_BATCH, DEC_SEQ, D_PLE), 1.0),
        'norm_mix': gain((DEPTH, D_MODEL)),
        'ssm_in_proj': nrm((N_A_LAYERS, D_MODEL, IN_PROJ_DIM), D_MODEL ** -0.5),
        'ssm_conv_w': nrm((N_A_LAYERS, D_CONV, CONV_DIM), D_CONV ** -0.5),
        'ssm_conv_b': nrm((N_A_LAYERS, CONV_DIM), 0.01),
        'ssm_dt_bias': dt_bias,
        'ssm_a_log': a_log,
        'ssm_d': gain((N_A_LAYERS, SSM_HEADS)),
        'ssm_norm_y': gain((N_A_LAYERS, D_INNER)),
        'ssm_out_proj': nrm((N_A_LAYERS, D_INNER, D_MODEL), D_INNER ** -0.5),
        'norm_kv': gain((D_MODEL,)),
        'w_kv': nrm((D_MODEL, 2 * ATTN_DIM), D_MODEL ** -0.5),
        'attn_w_q': nrm((N_B_LAYERS, D_MODEL, Q_DIM), D_MODEL ** -0.5),
        'attn_w_o': nrm((N_B_LAYERS, ATTN_DIM, D_MODEL), ATTN_DIM ** -0.5),
        'norm_ffn': gain((DEPTH, D_MODEL)),
        'moe_w_rg': nrm((DEPTH, D_MODEL, N_EXPERT_GROUPS), D_MODEL ** -0.5),
        'moe_b_rg': nrm((DEPTH, N_EXPERT_GROUPS), 0.01),
        'moe_w_re': nrm((DEPTH, D_MODEL, N_EXPERTS), D_MODEL ** -0.5),
        'moe_b_re': nrm((DEPTH, N_EXPERTS), 0.01),
        'moe_w_gate': nrm((DEPTH, N_EXPERTS, D_MODEL, D_EXPERT), D_MODEL ** -0.5),
        'moe_w_up': nrm((DEPTH, N_EXPERTS, D_MODEL, D_EXPERT), D_MODEL ** -0.5),
        'moe_w_down': nrm((DEPTH, N_EXPERTS, D_EXPERT, D_MODEL), D_EXPERT ** -0.5),
        'ple_w_proj': nrm((DEPTH, D_PLE, D_MODEL), D_PLE ** -0.5),
        'ple_w_gate': nrm((DEPTH, D_MODEL, D_MODEL), D_MODEL ** -0.5),
        'ple_norm': gain((DEPTH, D_MODEL)),
        'norm_final': gain((D_MODEL,)),
    }


def reference(x_prompt, x_sample, state_ssm, state_conv, cache_k, cache_v, p_prompt, p_sample,
              norm_mix, ssm_in_proj, ssm_conv_w, ssm_conv_b, ssm_dt_bias, ssm_a_log, ssm_d, ssm_norm_y,
              ssm_out_proj, norm_kv, w_kv, attn_w_q, attn_w_o, norm_ffn, moe_w_rg, moe_b_rg, moe_w_re, moe_b_re,
              moe_w_gate, moe_w_up, moe_w_down, ple_w_proj, ple_w_gate, ple_norm, norm_final):
    prm = dict(norm_mix=norm_mix, ssm_in_proj=ssm_in_proj, ssm_conv_w=ssm_conv_w, ssm_conv_b=ssm_conv_b,
               ssm_dt_bias=ssm_dt_bias, ssm_a_log=ssm_a_log, ssm_d=ssm_d, ssm_norm_y=ssm_norm_y,
               ssm_out_proj=ssm_out_proj, norm_kv=norm_kv, w_kv=w_kv, attn_w_q=attn_w_q, attn_w_o=attn_w_o,
               norm_ffn=norm_ffn, moe_w_rg=moe_w_rg, moe_b_rg=moe_b_rg, moe_w_re=moe_w_re, moe_b_re=moe_b_re,
               moe_w_gate=moe_w_gate, moe_w_up=moe_w_up, moe_w_down=moe_w_down, ple_w_proj=ple_w_proj,
               ple_w_gate=ple_w_gate, ple_norm=ple_norm, norm_final=norm_final)
    b, s, _ = x_prompt.shape
    conv0 = jnp.zeros((N_A_LAYERS, b, D_CONV - 1, CONV_DIM), x_prompt.dtype)
    ssm0 = jnp.zeros((N_A_LAYERS, b, SSM_HEADS, SSM_HEADDIM, D_STATE), state_ssm.dtype)
    y_prompt, conv_p, ssm_p, k_p, v_p = trunk(x_prompt, p_prompt, 0, conv0, ssm0, None, None, prm)
    keep = min(WINDOW_MAX, s)
    k_p = k_p[:, s - keep:]
    v_p = v_p[:, s - keep:]
    y_sample, conv_s, ssm_s, k_s, v_s = trunk(x_sample, p_sample, PAST_LEN, state_conv, state_ssm, cache_k, cache_v, prm)
    return (y_prompt, y_sample, ssm_p, conv_p, k_p, v_p, ssm_s, conv_s, k_s, v_s)
```

```python
import functools

import jax
import jax.numpy as jnp
from jax import lax
from jax.experimental import pallas as pl
from jax.experimental.pallas import tpu as pltpu

F32 = jnp.float32
BF16 = jnp.bfloat16
I32 = jnp.int32

EPS = 1e-6
D_MODEL = 1024
D_INNER = 2048
SSM_HEADS = 32
SSM_HEADDIM = 64
SSM_GROUPS = 4
D_STATE = 128
CONV_DIM = 3072
ZXBC_DIM = D_INNER + CONV_DIM
CHUNK = 128
HEAD_DIM = 128
N_KV_HEADS = 8
ATTN_DIM = 1024
PATTERNS = ((128, 1), (512, 4), (2048, 16))
ROPE_THETA = 10000.0
N_EXPERTS = 32
N_EXPERT_GROUPS = 4
EXPERTS_PER_GROUP = 8
D_EXPERT = 512
N_A_LAYERS = 2
DEPTH = 4
ROUTER_ROWS = 8 + N_EXPERTS
LANES = 128
VMEM_LIMIT = 56 * 1024 * 1024


def _params(sem):
    return pltpu.CompilerParams(dimension_semantics=sem, vmem_limit_bytes=VMEM_LIMIT)


def _iota(shape, dim):
    return lax.broadcasted_iota(I32, shape, dim)


def _rms(x, gain):
    ms = jnp.mean(x * x, axis=-1, keepdims=True)
    return x * lax.rsqrt(ms + EPS) * gain


def _silu(x):
    return x * jax.nn.sigmoid(x)


def _dot(a, b):
    return jnp.dot(a, b, preferred_element_type=F32)


def _dot_nt(a, b):
    return lax.dot_general(a, b, (((1,), (1,)), ((), ())), preferred_element_type=F32)


def _split3(x):
    hi = x.astype(BF16)
    r = x - hi.astype(F32)
    mid = r.astype(BF16)
    lo = (r - mid.astype(F32)).astype(BF16)
    return hi, mid, lo


def _proj_body(*refs, has_norm, has_rope, has_res, n_out):
    it = iter(refs)
    x_ref = next(it)
    w_ref = next(it)
    g_ref = next(it) if has_norm else None
    cos_ref = next(it) if has_rope else None
    sin_ref = next(it) if has_rope else None
    r_ref = next(it) if has_res else None
    o_refs = [next(it) for _ in range(n_out)]
    xb_ref = next(it)

    @pl.when(pl.program_id(1) == 0)
    def _():
        x = x_ref[...].astype(F32)
        if has_norm:
            x = _rms(x, g_ref[...])
        xb_ref[...] = x.astype(BF16)

    y = _dot(xb_ref[...], w_ref[...])
    if has_rope:
        cos = cos_ref[...]
        sin = sin_ref[...]
        parts = []
        for h in range(y.shape[1] // HEAD_DIM):
            t = y[:, h * HEAD_DIM:(h + 1) * HEAD_DIM]
            parts.append(t * cos + pltpu.roll(t, HEAD_DIM // 2, 1) * sin)
        y = parts[0] if len(parts) == 1 else jnp.concatenate(parts, axis=1)
    if has_res:
        y = y + r_ref[...]
    for o in o_refs:
        o[...] = y.astype(o.dtype)


def _proj(x, w, *, gain=None, rope=None, residual=None, out_dtypes=(F32,), tm, tn):
    n, k = x.shape
    nout = w.shape[1]
    tn = min(tn, nout)
    grid = (n // tm, nout // tn)
    in_specs = [pl.BlockSpec((tm, k), lambda i, j: (i, 0)),
                pl.BlockSpec((k, tn), lambda i, j: (0, j))]
    args = [x, w]
    if gain is not None:
        in_specs.append(pl.BlockSpec((1, k), lambda i, j: (0, 0)))
        args.append(gain.reshape(1, k))
    if rope is not None:
        cos, sin = rope
        nblk = cos.shape[0] // tm
        for t in (cos, sin):
            in_specs.append(pl.BlockSpec((tm, HEAD_DIM), lambda i, j: (i % nblk, 0)))
            args.append(t)
    if residual is not None:
        in_specs.append(pl.BlockSpec((tm, tn), lambda i, j: (i, j)))
        args.append(residual)
    out_shape = [jax.ShapeDtypeStruct((n, nout), dt) for dt in out_dtypes]
    out_specs = [pl.BlockSpec((tm, tn), lambda i, j: (i, j)) for _ in out_dtypes]
    body = functools.partial(_proj_body, has_norm=gain is not None, has_rope=rope is not None,
                             has_res=residual is not None, n_out=len(out_dtypes))
    outs = pl.pallas_call(
        body, grid=grid, in_specs=in_specs, out_specs=out_specs, out_shape=out_shape,
        scratch_shapes=[pltpu.VMEM((tm, k), BF16)],
        compiler_params=_params(("parallel", "arbitrary")), name="proj")(*args)
    return outs[0] if len(outs) == 1 else outs


def _ssd_body(zx_ref, dt_ref, cw_ref, cb_ref, dtb_ref, alog_ref, d_ref, ny_ref, conv0_ref, ssm0_ref,
              y_ref, convo_ref, ssmo_ref, st_ref, tail_ref, xbc_ref, yacc_ref, *pad_refs, valid, nchunks):
    L = CHUNK
    c = pl.program_id(1)
    npair = SSM_HEADS // 2

    @pl.when(c == 0)
    def _():
        for j in range(npair):
            st_ref[:, j * LANES:(j + 1) * LANES] = ssm0_ref[0, j * LANES:(j + 1) * LANES, :].T
        tail_ref[...] = conv0_ref[0]

    if valid < L:
        pzx_ref, pdt_ref = pad_refs
        pzx_ref[...] = jnp.zeros_like(pzx_ref)
        pdt_ref[...] = jnp.zeros_like(pdt_ref)
        pzx_ref[0:valid, :] = zx_ref[0]
        pdt_ref[0:valid, :] = dt_ref[0]
        zsrc, dsrc = pzx_ref, pdt_ref
        z = zsrc[:, :D_INNER]
        u = zsrc[:, D_INNER:]
        dtr = dsrc[...]
    else:
        z = zx_ref[0, :, :D_INNER]
        u = zx_ref[0, :, D_INNER:]
        dtr = dt_ref[0]

    tail8 = tail_ref[...]
    row8 = _iota((8, 1), 0)
    acc = cb_ref[...] + cw_ref[3:4, :] * u
    for k in range(1, 4):
        rk = pltpu.roll(u, k, 0)
        tk = pltpu.roll(tail8, k, 0)
        first = jnp.where(row8 < k, tk, rk[:8])
        sh = jnp.concatenate([first, rk[8:]], axis=0)
        acc = acc + cw_ref[3 - k:4 - k, :] * sh
    if valid >= 8:
        tail_ref[...] = u[valid - 8:valid]
    else:
        s = 8 - valid
        tail_ref[...] = jnp.where(row8 < s, pltpu.roll(tail8, s, 0), pltpu.roll(u[:8], s, 0))
    xbc_ref[...] = _silu(acc)

    lane = _iota((1, LANES), 1)
    rowl = _iota((L, 1), 0)
    xdt = dtr + dtb_ref[...]
    dt = jnp.maximum(xdt, 0.0) + jnp.log1p(jnp.exp(-jnp.abs(xdt)))
    dt = jnp.where((lane < SSM_HEADS) & (rowl < valid), dt, 0.0)
    a_neg = -jnp.exp(alog_ref[...])
    da = dt * a_neg
    rr = _iota((L, L), 0)
    cc = _iota((L, L), 1)
    causal = rr >= cc
    tri = jnp.where(causal, 1.0, 0.0).astype(BF16)
    hi, mid, lo = _split3(da)
    a_cs = _dot(tri, hi) + _dot(tri, mid) + _dot(tri, lo)
    a_cs_t = a_cs.T
    dt_t = dt.T
    tdt_t = dt_t * jnp.exp(a_cs_t[:, L - 1:L] - a_cs_t)
    half = _iota((1, LANES), 1) < SSM_HEADDIM

    for g in range(SSM_GROUPS):
        b_g = xbc_ref[:, D_INNER + g * D_STATE:D_INNER + (g + 1) * D_STATE]
        c_g = xbc_ref[:, D_INNER + SSM_GROUPS * D_STATE + g * D_STATE:
                      D_INNER + SSM_GROUPS * D_STATE + (g + 1) * D_STATE]
        cb = _dot_nt(c_g.astype(BF16), b_g.astype(BF16))
        b_t = b_g.T
        for jj in range(npair // SSM_GROUPS):
            j = g * (npair // SSM_GROUPS) + jj
            cols = slice(j * LANES, (j + 1) * LANES)
            xs_b = xbc_ref[:, cols].astype(BF16)
            st_p = st_ref[:, cols]
            rhs = jnp.concatenate([xs_b, st_p.astype(BF16)], axis=0)
            ys, ss, cds = [], [], []
            for h in (2 * j, 2 * j + 1):
                colb = jnp.broadcast_to(a_cs[:, h:h + 1], (L, LANES))
                seg = colb - a_cs_t[h:h + 1, :]
                dec = jnp.exp(jnp.where(causal, seg, -jnp.inf))
                m = cb * dec * dt_t[h:h + 1, :]
                ce = c_g * jnp.exp(colb)
                lhs = jnp.concatenate([m.astype(BF16), ce.astype(BF16)], axis=1)
                ys.append(_dot(lhs, rhs))
                ss.append(_dot((b_t * tdt_t[h:h + 1, :]).astype(BF16), xs_b))
                cds.append(jnp.exp(colb[L - 1:L, :]))
            yacc_ref[:, cols] = jnp.where(half, ys[0], ys[1])
            st_ref[:, cols] = jnp.where(half, st_p * cds[0] + ss[0], st_p * cds[1] + ss[1])

    y = yacc_ref[...] + d_ref[...] * xbc_ref[:, :D_INNER]
    yn = _rms(y * _silu(z), ny_ref[...])
    if valid < L:
        y_ref[0] = yn[0:valid].astype(y_ref.dtype)
    else:
        y_ref[0] = yn.astype(y_ref.dtype)

    @pl.when(c == nchunks - 1)
    def _():
        convo_ref[0] = tail_ref[...]
        for j in range(npair):
            ssmo_ref[0, j * LANES:(j + 1) * LANES, :] = st_ref[:, j * LANES:(j + 1) * LANES].T


def _ssd(zx, dtr, conv_w, conv_b, dt_bias, a_log, d_skip, norm_y, conv0, ssm0):
    b, l, _ = zx.shape
    if l >= CHUNK:
        lb, valid, nchunks = CHUNK, CHUNK, l // CHUNK
    else:
        lb, valid, nchunks = l, l, 1
    pad_lanes = LANES - SSM_HEADS
    dtb = jnp.pad(dt_bias.astype(F32), (0, pad_lanes)).reshape(1, LANES)
    alog = jnp.pad(a_log.astype(F32), (0, pad_lanes)).reshape(1, LANES)
    d_exp = jnp.repeat(d_skip.astype(F32), SSM_HEADDIM).reshape(1, D_INNER)
    conv0p = jnp.pad(conv0, ((0, 0), (5, 0), (0, 0)))
    ssm0f = ssm0.reshape(b, SSM_HEADS * SSM_HEADDIM, D_STATE)
    const = lambda bi, c: (0, 0)
    in_specs = [
        pl.BlockSpec((1, lb, ZXBC_DIM), lambda bi, c: (bi, c, 0)),
        pl.BlockSpec((1, lb, LANES), lambda bi, c: (bi, c, 0)),
        pl.BlockSpec((4, CONV_DIM), const),
        pl.BlockSpec((1, CONV_DIM), const),
        pl.BlockSpec((1, LANES), const),
        pl.BlockSpec((1, LANES), const),
        pl.BlockSpec((1, D_INNER), const),
        pl.BlockSpec((1, D_INNER), const),
        pl.BlockSpec((1, 8, CONV_DIM), lambda bi, c: (bi, 0, 0)),
        pl.BlockSpec((1, D_INNER, D_STATE), lambda bi, c: (bi, 0, 0)),
    ]
    out_shape = [jax.ShapeDtypeStruct((b, l, D_INNER), BF16),
                 jax.ShapeDtypeStruct((b, 8, CONV_DIM), F32),
                 jax.ShapeDtypeStruct((b, D_INNER, D_STATE), F32)]
    out_specs = [pl.BlockSpec((1, lb, D_INNER), lambda bi, c: (bi, c, 0)),
                 pl.BlockSpec((1, 8, CONV_DIM), lambda bi, c: (bi, 0, 0)),
                 pl.BlockSpec((1, D_INNER, D_STATE), lambda bi, c: (bi, 0, 0))]
    scratch = [pltpu.VMEM((D_STATE, D_INNER), F32), pltpu.VMEM((8, CONV_DIM), F32),
               pltpu.VMEM((CHUNK, CONV_DIM), F32), pltpu.VMEM((CHUNK, D_INNER), F32)]
    if valid < CHUNK:
        scratch += [pltpu.VMEM((CHUNK, ZXBC_DIM), F32), pltpu.VMEM((CHUNK, LANES), F32)]
    y, convo, ssmo = pl.pallas_call(
        functools.partial(_ssd_body, valid=valid, nchunks=nchunks),
        grid=(b, nchunks), in_specs=in_specs, out_specs=out_specs, out_shape=out_shape,
        scratch_shapes=scratch, compiler_params=_params(("parallel", "arbitrary")), name="ssd")(
            zx, dtr, conv_w, conv_b.reshape(1, CONV_DIM), dtb, alog, d_exp, norm_y.reshape(1, D_INNER),
            conv0p, ssm0f)
    return y, convo[:, 5:], ssmo.reshape(b, SSM_HEADS, SSM_HEADDIM, D_STATE)


def _router_body(x_ref, g_ref, w_ref, b_ref, xn_ref, slab_ref):
    xn = _rms(x_ref[...], g_ref[...])
    xn_ref[...] = xn.astype(BF16)
    xh, xm, xl = _split3(xn)
    wh = w_ref[0]
    wm = w_ref[1]
    wl = w_ref[2]
    logits = (_dot_nt(wh, xh) + _dot_nt(wh, xm) + _dot_nt(wm, xh)
              + _dot_nt(wh, xl) + _dot_nt(wm, xm) + _dot_nt(wl, xh)) + b_ref[...]
    tm = logits.shape[1]
    gl = [logits[i:i + 1, :] for i in range(N_EXPERT_GROUPS)]
    best = gl[0]
    sel = jnp.zeros((1, tm), I32)
    for i in range(1, N_EXPERT_GROUPS):
        upd = gl[i] > best
        sel = jnp.where(upd, i, sel)
        best = jnp.where(upd, gl[i], best)
    den = jnp.exp(gl[0] - best)
    for i in range(1, N_EXPERT_GROUPS):
        den = den + jnp.exp(gl[i] - best)
    g_w = 1.0 / den
    chosen = logits[8:8 + EXPERTS_PER_GROUP, :]
    for i in range(1, N_EXPERT_GROUPS):
        chosen = jnp.where(sel == i, logits[8 + i * EXPERTS_PER_GROUP:8 + (i + 1) * EXPERTS_PER_GROUP, :], chosen)
    idx = _iota((EXPERTS_PER_GROUP, tm), 0)
    m1 = jnp.max(chosen, axis=0, keepdims=True)
    i1 = jnp.min(jnp.where(chosen == m1, idx, EXPERTS_PER_GROUP), axis=0, keepdims=True)
    rest = jnp.where(idx == i1, -jnp.inf, chosen)
    m2 = jnp.max(rest, axis=0, keepdims=True)
    i2 = jnp.min(jnp.where(rest == m2, idx, EXPERTS_PER_GROUP), axis=0, keepdims=True)
    t = jnp.exp(m2 - m1)
    inv = 1.0 / (1.0 + t)
    w0 = inv * g_w
    w1 = (t * inv) * g_w
    e0 = (sel * EXPERTS_PER_GROUP + i1).astype(F32)
    e1 = (sel * EXPERTS_PER_GROUP + i2).astype(F32)
    r = _iota((8, tm), 0)
    slab = jnp.where(r == 0, e0, jnp.where(r == 1, e1, jnp.where(r == 2, w0, jnp.where(r == 3, w1, 0.0))))
    slab_ref[...] = slab


def _router(x, gain, w_split, bias, tm):
    n = x.shape[0]
    xn, slab = pl.pallas_call(
        _router_body, grid=(n // tm,),
        in_specs=[pl.BlockSpec((tm, D_MODEL), lambda i: (i, 0)),
                  pl.BlockSpec((1, D_MODEL), lambda i: (0, 0)),
                  pl.BlockSpec((3, ROUTER_ROWS, D_MODEL), lambda i: (0, 0, 0)),
                  pl.BlockSpec((ROUTER_ROWS, 1), lambda i: (0, 0))],
        out_specs=[pl.BlockSpec((tm, D_MODEL), lambda i: (i, 0)),
                   pl.BlockSpec((8, tm), lambda i: (0, i))],
        out_shape=[jax.ShapeDtypeStruct((n, D_MODEL), BF16), jax.ShapeDtypeStruct((8, n), F32)],
        compiler_params=_params(("parallel",)), name="router")(x, gain.reshape(1, D_MODEL), w_split, bias)
    e_ids = slab[0:2].T.astype(I32)
    e_w = slab[2:4].T
    return xn, e_ids, e_w


def _moe_plan(e_ids, e_w, tm):
    n = e_ids.shape[0]
    a = 2 * n
    flat_e = e_ids.reshape(a)
    flat_w = e_w.reshape(a)
    order = jnp.argsort(flat_e, stable=True).astype(I32)
    sorted_e = flat_e[order]
    counts = jnp.bincount(flat_e, length=N_EXPERTS).astype(I32)
    padded = ((counts + tm - 1) // tm) * tm
    pend = jnp.cumsum(padded)
    start = jnp.cumsum(counts) - counts
    pstart = pend - padded
    dest = pstart[sorted_e] + (jnp.arange(a, dtype=I32) - start[sorted_e])
    slots = a + N_EXPERTS * tm
    slot_token = jnp.zeros((slots,), I32).at[dest].set(order // 2)
    slot_w = jnp.zeros((slots,), F32).at[dest].set(flat_w[order])
    pos = jnp.zeros((a,), I32).at[order].set(dest).reshape(n, 2)
    n_tiles = slots // tm
    n_used = (pend[-1] // tm).astype(I32)
    tile_e = jnp.searchsorted(pend, jnp.arange(n_tiles, dtype=I32) * tm, side='right').astype(I32)
    tile_e = jnp.minimum(tile_e, N_EXPERTS - 1)
    last_e = tile_e[jnp.maximum(n_used - 1, 0)]
    tile_e = jnp.where(jnp.arange(n_tiles) < n_used, tile_e, last_e)
    return slot_token, slot_w.reshape(slots, 1), pos, tile_e, n_used.reshape(1)


def _moe_body(te_ref, nu_ref, xs_ref, sw_ref, wg_ref, wu_ref, wd_ref, o_ref, wgb_ref, wub_ref, wdb_ref):
    t = pl.program_id(0)
    e = te_ref[t]
    prev = te_ref[jnp.maximum(t - 1, 0)]

    @pl.when((t == 0) | (e != prev))
    def _():
        wgb_ref[...] = wg_ref[0].astype(BF16)
        wub_ref[...] = wu_ref[0].astype(BF16)
        wdb_ref[...] = wd_ref[0].astype(BF16)

    @pl.when(t < nu_ref[0])
    def _():
        x = xs_ref[...]
        g = _dot(x, wgb_ref[...])
        u = _dot(x, wub_ref[...])
        hid = _silu(g) * u * sw_ref[...]
        o_ref[...] = _dot(hid.astype(BF16), wdb_ref[...])


def _moe_experts(xs, slot_w, tile_e, n_used, w_gate, w_up, w_down, tm):
    slots = xs.shape[0]
    n_tiles = slots // tm
    row = lambda t, te, nu: (jnp.minimum(t, nu[0] - 1), 0)
    wsel = lambda t, te, nu: (te[t], 0, 0)
    grid_spec = pltpu.PrefetchScalarGridSpec(
        num_scalar_prefetch=2, grid=(n_tiles,),
        in_specs=[pl.BlockSpec((tm, D_MODEL), row),
                  pl.BlockSpec((tm, 1), row),
                  pl.BlockSpec((1, D_MODEL, D_EXPERT), wsel),
                  pl.BlockSpec((1, D_MODEL, D_EXPERT), wsel),
                  pl.BlockSpec((1, D_EXPERT, D_MODEL), wsel)],
        out_specs=pl.BlockSpec((tm, D_MODEL), row),
        scratch_shapes=[pltpu.VMEM((D_MODEL, D_EXPERT), BF16), pltpu.VMEM((D_MODEL, D_EXPERT), BF16),
                        pltpu.VMEM((D_EXPERT, D_MODEL), BF16)])
    return pl.pallas_call(
        _moe_body, grid_spec=grid_spec, out_shape=jax.ShapeDtypeStruct((slots, D_MODEL), F32),
        compiler_params=_params(("arbitrary",)), name="moe_experts")(
            tile_e, n_used, xs, slot_w, w_gate, w_up, w_down)


def _ple_body(*refs, final):
    x_ref, my_ref, p_ref, g_ref, wg_ref, wp_ref = refs[:6]
    if final:
        fg_ref, o_ref, f_ref = refs[6:]
    else:
        (o_ref,) = refs[6:]
    x2 = x_ref[...] + my_ref[...]
    xn = _rms(x2, g_ref[...])
    gate = jax.nn.sigmoid(_dot(xn.astype(BF16), wg_ref[...]))
    pe = _dot(p_ref[...].astype(BF16), wp_ref[...])
    out = x2 + pe * gate
    o_ref[...] = out
    if final:
        f_ref[...] = _rms(out, fg_ref[...])


def _ple(x, moe_y, p, gain, w_gate, w_proj, final_gain, tm):
    n = x.shape[0]
    dp = p.shape[1]
    final = final_gain is not None
    rowspec = pl.BlockSpec((tm, D_MODEL), lambda i: (i, 0))
    vec = pl.BlockSpec((1, D_MODEL), lambda i: (0, 0))
    in_specs = [rowspec, rowspec, pl.BlockSpec((tm, dp), lambda i: (i, 0)), vec,
                pl.BlockSpec((D_MODEL, D_MODEL), lambda i: (0, 0)),
                pl.BlockSpec((dp, D_MODEL), lambda i: (0, 0))]
    args = [x, moe_y, p, gain.reshape(1, D_MODEL), w_gate, w_proj]
    out_shape = [jax.ShapeDtypeStruct((n, D_MODEL), F32)]
    out_specs = [rowspec]
    if final:
        in_specs.append(vec)
        args.append(final_gain.reshape(1, D_MODEL))
        out_shape.append(jax.ShapeDtypeStruct((n, D_MODEL), F32))
        out_specs.append(rowspec)
    outs = pl.pallas_call(
        functools.partial(_ple_body, final=final), grid=(n // tm,), in_specs=in_specs,
        out_specs=out_specs, out_shape=out_shape, compiler_params=_params(("parallel",)), name="ple")(*args)
    return (outs[0], outs[1]) if final else (outs[0], None)


def _attn_body(q_ref, kc_ref, kp_ref, vc_ref, vp_ref, o_ref, lse_ref):
    has_prev = pl.program_id(2) > 0
    R = CHUNK
    row = _iota((R, R), 0)
    col = _iota((R, R), 1)
    m_cur = col <= row
    m_prev = (col >= row) & has_prev
    lane = _iota((R, LANES), 1)
    scale = HEAD_DIM ** -0.5
    lse_tile = jnp.zeros((R, LANES), F32)
    for h in range(N_KV_HEADS):
        sl = slice(h * HEAD_DIM, (h + 1) * HEAD_DIM)
        q = q_ref[0, :, sl]
        sc = jnp.where(m_cur, _dot_nt(q, kc_ref[0, :, sl]) * scale, -jnp.inf)
        sp = jnp.where(m_prev, _dot_nt(q, kp_ref[0, :, sl]) * scale, -jnp.inf)
        m = jnp.maximum(jnp.max(sc, axis=-1, keepdims=True), jnp.max(sp, axis=-1, keepdims=True))
        pc = jnp.exp(sc - m)
        pp = jnp.exp(sp - m)
        l = jnp.sum(pc, axis=-1, keepdims=True) + jnp.sum(pp, axis=-1, keepdims=True)
        o = _dot(pc.astype(BF16), vc_ref[0, :, sl]) + _dot(pp.astype(BF16), vp_ref[0, :, sl])
        o_ref[0, :, sl] = o * (1.0 / l)
        lse_tile = jnp.where(lane == h, m + jnp.log(l), lse_tile)
    lse_ref[0] = lse_tile


def _attn_prompt(q, k, v, b, s, g, dil):
    n_lane = s // dil
    nb = n_lane // CHUNK
    qv = q.reshape(b, n_lane, dil * 3 * ATTN_DIM)
    kv = k.reshape(b, n_lane, dil * ATTN_DIM)
    vv = v.reshape(b, n_lane, dil * ATTN_DIM)
    blk = (1, CHUNK, ATTN_DIM)
    cur = lambda bi, r, c: (bi, c, r)
    prev = lambda bi, r, c: (bi, jnp.maximum(c - 1, 0), r)
    o, lse = pl.pallas_call(
        _attn_body, grid=(b, dil, nb),
        in_specs=[pl.BlockSpec(blk, lambda bi, r, c: (bi, c, r * 3 + g)),
                  pl.BlockSpec(blk, cur), pl.BlockSpec(blk, prev),
                  pl.BlockSpec(blk, cur), pl.BlockSpec(blk, prev)],
        out_specs=[pl.BlockSpec(blk, cur), pl.BlockSpec((1, CHUNK, LANES), cur)],
        out_shape=[jax.ShapeDtypeStruct((b, n_lane, dil * ATTN_DIM), F32),
                   jax.ShapeDtypeStruct((b, n_lane, dil * LANES), F32)],
        compiler_params=_params(("parallel", "parallel", "arbitrary")), name="attn_prompt")(qv, kv, kv, vv, vv)
    return o.reshape(b * s, ATTN_DIM), lse.reshape(b * s, LANES)


def _combine_body(o0_ref, o1_ref, o2_ref, l0_ref, l1_ref, l2_ref, w_ref, r_ref, out_ref):
    l0 = l0_ref[...]
    l1 = l1_ref[...]
    l2 = l2_ref[...]
    m = jnp.maximum(jnp.maximum(l0, l1), l2)
    e0 = jnp.exp(l0 - m)
    e1 = jnp.exp(l1 - m)
    e2 = jnp.exp(l2 - m)
    inv = 1.0 / (e0 + e1 + e2)
    w0 = e0 * inv
    w1 = e1 * inv
    w2 = e2 * inv
    parts = []
    for h in range(N_KV_HEADS):
        sl = slice(h * HEAD_DIM, (h + 1) * HEAD_DIM)
        parts.append(w0[:, h:h + 1] * o0_ref[:, sl] + w1[:, h:h + 1] * o1_ref[:, sl]
                     + w2[:, h:h + 1] * o2_ref[:, sl])
    att = jnp.concatenate(parts, axis=1)
    out_ref[...] = r_ref[...] + _dot(att.astype(BF16), w_ref[...])


def _combine_wo(os, lses, w_o, residual, tm):
    n = residual.shape[0]
    rowspec = pl.BlockSpec((tm, ATTN_DIM), lambda i: (i, 0))
    lspec = pl.BlockSpec((tm, LANES), lambda i: (i, 0))
    return pl.pallas_call(
        _combine_body, grid=(n // tm,),
        in_specs=[rowspec, rowspec, rowspec, lspec, lspec, lspec,
                  pl.BlockSpec((ATTN_DIM, D_MODEL), lambda i: (0, 0)), rowspec],
        out_specs=rowspec, out_shape=jax.ShapeDtypeStruct((n, D_MODEL), F32),
        compiler_params=_params(("parallel",)), name="combine_wo")(*os, *lses, w_o, residual)


def _attn_sample_body(q_ref, kc_ref, vc_ref, kn_ref, vn_ref, o_ref, *, wb, t_new):
    rows = q_ref.shape[2]
    q = q_ref[0, 0]
    scale = HEAD_DIM ** -0.5
    rho = _iota((rows, 1), 0)
    t = rho & (t_new - 1)
    g = rho >> (t_new.bit_length() - 1)
    dil = jnp.where(g == 1, PATTERNS[1][1], jnp.where(g == 2, PATTERNS[2][1], PATTERNS[0][1]))
    win = jnp.where(g == 1, PATTERNS[1][0], jnp.where(g == 2, PATTERNS[2][0], PATTERNS[0][0]))
    key = _iota((rows, wb), 1)
    dist = wb + t - key
    valid = (dist <= win) & ((dist & (dil - 1)) == 0)
    sc = _dot_nt(q.astype(BF16), kc_ref[0].astype(BF16)) * scale
    sc = jnp.where(valid, sc, -jnp.inf)
    m = jnp.max(sc, axis=-1, keepdims=True)
    kn = kn_ref[0]
    vn = vn_ref[0]
    qf = q.astype(BF16).astype(F32)
    s_new = []
    for u in range(t_new):
        su = jnp.sum(qf * kn[u:u + 1, :].astype(BF16).astype(F32), axis=-1, keepdims=True) * scale
        du = t - u
        su = jnp.where((du >= 0) & ((du & (dil - 1)) == 0), su, -jnp.inf)
        s_new.append(su)
        m = jnp.maximum(m, su)
    p = jnp.exp(sc - m)
    l = jnp.sum(p, axis=-1, keepdims=True)
    o = _dot(p.astype(BF16), vc_ref[0].astype(BF16))
    for u in range(t_new):
        pu = jnp.exp(s_new[u] - m)
        l = l + pu
        o = o + pu.astype(BF16).astype(F32) * vn[u:u + 1, :].astype(BF16).astype(F32)
    o = o * (1.0 / l)
    lse = jnp.broadcast_to(m + jnp.log(l), (rows, LANES))
    lse1 = pltpu.roll(lse, rows - t_new, 0)
    lse2 = pltpu.roll(lse, rows - 2 * t_new, 0)
    o1 = pltpu.roll(o, rows - t_new, 0)
    o2 = pltpu.roll(o, rows - 2 * t_new, 0)
    mm = jnp.maximum(jnp.maximum(lse, lse1), lse2)
    e0 = jnp.exp(lse - mm)
    e1 = jnp.exp(lse1 - mm)
    e2 = jnp.exp(lse2 - mm)
    comb = (e0 * o + e1 * o1 + e2 * o2) * (1.0 / (e0 + e1 + e2))
    o_ref[0] = comb[0:t_new]


def _attn_sample(q, cache_k, cache_v, k_new, v_new, b, t_new):
    wb = cache_k.shape[1]
    rows = 16
    qh = q.reshape(b, t_new, 3, N_KV_HEADS, HEAD_DIM).transpose(0, 3, 2, 1, 4).reshape(b, N_KV_HEADS, 3 * t_new, HEAD_DIM)
    qh = jnp.pad(qh, ((0, 0), (0, 0), (0, rows - 3 * t_new), (0, 0)))
    kn = jnp.pad(k_new.reshape(b, t_new, ATTN_DIM), ((0, 0), (0, 8 - t_new), (0, 0)))
    vn = jnp.pad(v_new.reshape(b, t_new, ATTN_DIM), ((0, 0), (0, 8 - t_new), (0, 0)))
    head = lambda bi, h: (bi, 0, h)
    o = pl.pallas_call(
        functools.partial(_attn_sample_body, wb=wb, t_new=t_new), grid=(b, N_KV_HEADS),
        in_specs=[pl.BlockSpec((1, 1, rows, HEAD_DIM), lambda bi, h: (bi, h, 0, 0)),
                  pl.BlockSpec((1, wb, HEAD_DIM), head), pl.BlockSpec((1, wb, HEAD_DIM), head),
                  pl.BlockSpec((1, 8, HEAD_DIM), head), pl.BlockSpec((1, 8, HEAD_DIM), head)],
        out_specs=pl.BlockSpec((1, t_new, HEAD_DIM), head),
        out_shape=jax.ShapeDtypeStruct((b, t_new, ATTN_DIM), F32),
        compiler_params=_params(("parallel", "arbitrary")), name="attn_sample")(qh, cache_k, cache_v, kn, vn)
    return o.reshape(b * t_new, ATTN_DIM)


def _rope_tables(pos0, l, reps):
    inv_freq = ROPE_THETA ** (-jnp.arange(0, HEAD_DIM, 2, dtype=F32) / HEAD_DIM)
    pos = pos0 + jnp.arange(l, dtype=jnp.int32)
    ang = pos.astype(F32)[:, None] * inv_freq[None, :]
    cos = jnp.cos(ang)
    sin = jnp.sin(ang)
    cosf = jnp.concatenate([cos, cos], axis=-1)
    sinf = jnp.concatenate([-sin, sin], axis=-1)
    if reps > 1:
        cosf = jnp.tile(cosf, (reps, 1))
        sinf = jnp.tile(sinf, (reps, 1))
    return cosf, sinf


def _prep_weights(prm):
    w = {}
    w['in_zxbc'] = prm['ssm_in_proj'][:, :, :ZXBC_DIM].astype(BF16)
    w['in_dt'] = jnp.pad(prm['ssm_in_proj'][:, :, ZXBC_DIM:], ((0, 0), (0, 0), (0, LANES - SSM_HEADS))).astype(BF16)
    w['out_proj'] = prm['ssm_out_proj'].astype(BF16)
    w['w_k'] = prm['w_kv'][:, :ATTN_DIM].astype(BF16)
    w['w_v'] = prm['w_kv'][:, ATTN_DIM:].astype(BF16)
    w['w_q'] = prm['attn_w_q'].astype(BF16)
    w['w_o'] = prm['attn_w_o'].astype(BF16)
    w['ple_gate'] = prm['ple_w_gate'].astype(BF16)
    w['ple_proj'] = prm['ple_w_proj'].astype(BF16)
    wr = jnp.concatenate([jnp.swapaxes(prm['moe_w_rg'], 1, 2),
                          jnp.zeros((DEPTH, 8 - N_EXPERT_GROUPS, D_MODEL), F32),
                          jnp.swapaxes(prm['moe_w_re'], 1, 2)], axis=1)
    w['router'] = jnp.stack(_split3(wr), axis=1)
    w['router_b'] = jnp.concatenate([prm['moe_b_rg'], jnp.zeros((DEPTH, 8 - N_EXPERT_GROUPS), F32),
                                     prm['moe_b_re']], axis=1).reshape(DEPTH, ROUTER_ROWS, 1)
    return w


def _trunk(x, p, pos0, conv_state, ssm_state, cache_k, cache_v, prm, w):
    b, l, _ = x.shape
    n = b * l
    sample = cache_k is not None
    tm = min(512, n)
    moe_tm = 256 if n >= 4096 else 32
    xf = x.reshape(n, D_MODEL)
    rope = _rope_tables(pos0, l, 1 if l >= tm else tm // l)
    new_conv, new_ssm = [], []
    k32 = v32 = kb = vb = None
    y_final = None
    for i in range(DEPTH):
        gain = prm['norm_mix'][i]
        if i < N_A_LAYERS:
            zx = _proj(xf, w['in_zxbc'][i], gain=gain, tm=tm, tn=1024)
            dtr = _proj(xf, w['in_dt'][i], gain=gain, tm=tm, tn=LANES)
            y, cbuf, hs = _ssd(zx.reshape(b, l, ZXBC_DIM), dtr.reshape(b, l, LANES), prm['ssm_conv_w'][i],
                               prm['ssm_conv_b'][i], prm['ssm_dt_bias'][i], prm['ssm_a_log'][i], prm['ssm_d'][i],
                               prm['ssm_norm_y'][i], conv_state[i], ssm_state[i])
            new_conv.append(cbuf)
            new_ssm.append(hs)
            xf = _proj(y.reshape(n, D_INNER), w['out_proj'][i], residual=xf, tm=tm, tn=1024)
        else:
            j = i - N_A_LAYERS
            if sample:
                q = _proj(xf, w['w_q'][j], gain=gain, rope=rope, tm=tm, tn=1024)
                att = _attn_sample(q, cache_k, cache_v, k32, v32, b, l)
                xf = _proj(att, w['w_o'][j], residual=xf, tm=tm, tn=1024)
            else:
                q = _proj(xf, w['w_q'][j], gain=gain, rope=rope, out_dtypes=(BF16,), tm=tm, tn=1024)
                os, lses = [], []
                for g, (_, dil) in enumerate(PATTERNS):
                    o, lse = _attn_prompt(q, kb, vb, b, l, g, dil)
                    os.append(o)
                    lses.append(lse)
                xf = _combine_wo(os, lses, w['w_o'][j], xf, tm)
        xn, e_ids, e_w = _router(xf, prm['norm_ffn'][i], w['router'][i], w['router_b'][i], tm)
        slot_token, slot_w, pos, tile_e, n_used = _moe_plan(e_ids, e_w, moe_tm)
        xs = xn[slot_token]
        ys = _moe_experts(xs, slot_w, tile_e, n_used, prm['moe_w_gate'][i], prm['moe_w_up'][i],
                          prm['moe_w_down'][i], moe_tm)
        moe_y = ys[pos[:, 0]] + ys[pos[:, 1]]
        xf, y_final = _ple(xf, moe_y, p[i].reshape(n, -1), prm['ple_norm'][i], w['ple_gate'][i], w['ple_proj'][i],
                           prm['norm_final'] if i == DEPTH - 1 else None, tm)
        if i == N_A_LAYERS - 1:
            k32, kb = _proj(xf, w['w_k'], gain=prm['norm_kv'], rope=rope, out_dtypes=(F32, BF16), tm=tm, tn=1024)
            v32, vb = _proj(xf, w['w_v'], gain=prm['norm_kv'], out_dtypes=(F32, BF16), tm=tm, tn=1024)
    return (y_final.reshape(b, l, D_MODEL), jnp.stack(new_conv), jnp.stack(new_ssm),
            k32.reshape(b, l, N_KV_HEADS, HEAD_DIM), v32.reshape(b, l, N_KV_HEADS, HEAD_DIM))


def kernel(x_prompt, x_sample, state_ssm, state_conv, cache_k, cache_v, p_prompt, p_sample, norm_mix, ssm_in_proj, ssm_conv_w, ssm_conv_b, ssm_dt_bias, ssm_a_log, ssm_d, ssm_norm_y, ssm_out_proj, norm_kv, w_kv, attn_w_q, attn_w_o, norm_ffn, moe_w_rg, moe_b_rg, moe_w_re, moe_b_re, moe_w_gate, moe_w_up, moe_w_down, ple_w_proj, ple_w_gate, ple_norm, norm_final):
    prm = dict(norm_mix=norm_mix, ssm_in_proj=ssm_in_proj, ssm_conv_w=ssm_conv_w, ssm_conv_b=ssm_conv_b,
               ssm_dt_bias=ssm_dt_bias, ssm_a_log=ssm_a_log, ssm_d=ssm_d, ssm_norm_y=ssm_norm_y,
               ssm_out_proj=ssm_out_proj, norm_kv=norm_kv, w_kv=w_kv, attn_w_q=attn_w_q, attn_w_o=attn_w_o,
               norm_ffn=norm_ffn, moe_w_rg=moe_w_rg, moe_b_rg=moe_b_rg, moe_w_re=moe_w_re, moe_b_re=moe_b_re,
               moe_w_gate=moe_w_gate, moe_w_up=moe_w_up, moe_w_down=moe_w_down, ple_w_proj=ple_w_proj,
               ple_w_gate=ple_w_gate, ple_norm=ple_norm, norm_final=norm_final)
    w = _prep_weights(prm)
    b, s, _ = x_prompt.shape
    db, dl, _ = x_sample.shape
    past_len = 8192
    conv0 = jnp.zeros((N_A_LAYERS, b, 3, CONV_DIM), x_prompt.dtype)
    ssm0 = jnp.zeros((N_A_LAYERS, b, SSM_HEADS, SSM_HEADDIM, D_STATE), state_ssm.dtype)
    y_p, conv_p, ssm_p, k_p, v_p = _trunk(x_prompt, p_prompt, 0, conv0, ssm0, None, None, prm, w)
    keep = min(2048, s)
    k_p = k_p[:, s - keep:]
    v_p = v_p[:, s - keep:]
    wb = cache_k.shape[1]
    y_s, conv_s, ssm_s, k_s, v_s = _trunk(x_sample, p_sample, past_len, state_conv, state_ssm,
                                          cache_k.reshape(db, wb, ATTN_DIM), cache_v.reshape(db, wb, ATTN_DIM), prm, w)
    return (y_p, y_s, ssm_p, conv_p, k_p, v_p, ssm_s, conv_s, k_s, v_s)
```

```python
import functools

import jax
import jax.numpy as jnp
from jax import lax
from jax.experimental import pallas as pl
from jax.experimental.pallas import tpu as pltpu

F32 = jnp.float32
BF16 = jnp.bfloat16
I32 = jnp.int32

EPS = 1e-6
D_MODEL = 1024
D_INNER = 2048
SSM_HEADS = 32
SSM_HEADDIM = 64
SSM_GROUPS = 4
D_STATE = 128
CONV_DIM = 3072
ZXBC_DIM = D_INNER + CONV_DIM
CHUNK = 128
HEAD_DIM = 128
N_KV_HEADS = 8
ATTN_DIM = 1024
PATTERNS = ((128, 1), (512, 4), (2048, 16))
ATTN_SPAN = 2048
ROPE_THETA = 10000.0
N_EXPERTS = 32
N_EXPERT_GROUPS = 4
EXPERTS_PER_GROUP = 8
D_EXPERT = 512
N_A_LAYERS = 2
DEPTH = 4
ROUTER_ROWS = 8 + N_EXPERTS
LANES = 128
SUBLANES = 8
ROW_TILES = D_MODEL // LANES
VMEM_LIMIT = 56 * 1024 * 1024


def _params(sem):
    return pltpu.CompilerParams(dimension_semantics=sem, vmem_limit_bytes=VMEM_LIMIT)


def _iota(shape, dim):
    return lax.broadcasted_iota(I32, shape, dim)


def _rms(x, gain):
    ms = jnp.mean(x * x, axis=-1, keepdims=True)
    return x * lax.rsqrt(ms + EPS) * gain


def _silu(x):
    return x * jax.nn.sigmoid(x)


def _dot(a, b):
    return jnp.dot(a, b, preferred_element_type=F32)


def _dot_nt(a, b):
    return lax.dot_general(a, b, (((1,), (1,)), ((), ())), preferred_element_type=F32)


def _split3(x):
    hi = x.astype(BF16)
    r = x - hi.astype(F32)
    mid = r.astype(BF16)
    lo = (r - mid.astype(F32)).astype(BF16)
    return hi, mid, lo


def _rows_to_tiles(ref, x):
    m = x.shape[0]
    for s in range(ROW_TILES):
        ref[pl.ds(s, m, stride=ROW_TILES), :] = x[:, s * LANES:(s + 1) * LANES]


def _tiles_to_rows(ref, m):
    return jnp.concatenate([ref[pl.ds(s, m, stride=ROW_TILES), :] for s in range(ROW_TILES)], axis=1)


def _proj_body(*refs, has_norm, has_rope, has_res, n_out):
    it = iter(refs)
    x_ref = next(it)
    w_ref = next(it)
    g_ref = next(it) if has_norm else None
    cos_ref = next(it) if has_rope else None
    sin_ref = next(it) if has_rope else None
    r_ref = next(it) if has_res else None
    o_refs = [next(it) for _ in range(n_out)]
    xb_ref = next(it)

    @pl.when(pl.program_id(1) == 0)
    def _():
        x = x_ref[...].astype(F32)
        if has_norm:
            x = _rms(x, g_ref[...])
        xb_ref[...] = x.astype(BF16)

    y = _dot(xb_ref[...], w_ref[...])
    if has_rope:
        cos = cos_ref[...]
        sin = sin_ref[...]
        parts = []
        for h in range(y.shape[1] // HEAD_DIM):
            t = y[:, h * HEAD_DIM:(h + 1) * HEAD_DIM]
            parts.append(t * cos + pltpu.roll(t, HEAD_DIM // 2, 1) * sin)
        y = parts[0] if len(parts) == 1 else jnp.concatenate(parts, axis=1)
    if has_res:
        y = y + r_ref[...]
    for o in o_refs:
        o[...] = y.astype(o.dtype)


def _proj(x, w, *, gain=None, rope=None, residual=None, out_dtypes=(F32,), nout=None, tm, tn):
    n, k = x.shape
    nout = w.shape[1] if nout is None else nout
    tn = min(tn, nout)
    grid = (n // tm, nout // tn)
    in_specs = [pl.BlockSpec((tm, k), lambda i, j: (i, 0)),
                pl.BlockSpec((k, tn), lambda i, j: (0, j))]
    args = [x, w]
    if gain is not None:
        in_specs.append(pl.BlockSpec((1, k), lambda i, j: (0, 0)))
        args.append(gain.reshape(1, k))
    if rope is not None:
        cos, sin = rope
        nblk = cos.shape[0] // tm
        for t in (cos, sin):
            in_specs.append(pl.BlockSpec((tm, HEAD_DIM), lambda i, j: (i % nblk, 0)))
            args.append(t)
    if residual is not None:
        in_specs.append(pl.BlockSpec((tm, tn), lambda i, j: (i, j)))
        args.append(residual)
    out_shape = [jax.ShapeDtypeStruct((n, nout), dt) for dt in out_dtypes]
    out_specs = [pl.BlockSpec((tm, tn), lambda i, j: (i, j)) for _ in out_dtypes]
    body = functools.partial(_proj_body, has_norm=gain is not None, has_rope=rope is not None,
                             has_res=residual is not None, n_out=len(out_dtypes))
    outs = pl.pallas_call(
        body, grid=grid, in_specs=in_specs, out_specs=out_specs, out_shape=out_shape,
        scratch_shapes=[pltpu.VMEM((tm, k), BF16)],
        compiler_params=_params(("parallel", "arbitrary")), name="proj")(*args)
    return outs[0] if len(outs) == 1 else outs


def _ssd_body(zx_ref, dt_ref, cw_ref, cb_ref, dtb_ref, alog_ref, d_ref, ny_ref, conv0_ref, ssm0_ref,
              y_ref, convo_ref, ssmo_ref, st_ref, tail_ref, xbc_ref, yacc_ref, *pad_refs, valid, nchunks):
    L = CHUNK
    c = pl.program_id(1)
    npair = SSM_HEADS // 2

    @pl.when(c == 0)
    def _():
        for j in range(npair):
            st_ref[:, j * LANES:(j + 1) * LANES] = ssm0_ref[0, j * LANES:(j + 1) * LANES, :].T
        tail_ref[...] = conv0_ref[0]

    if valid < L:
        pzx_ref, pdt_ref = pad_refs
        pzx_ref[...] = jnp.zeros_like(pzx_ref)
        pdt_ref[...] = jnp.zeros_like(pdt_ref)
        pzx_ref[0:valid, :] = zx_ref[0]
        pdt_ref[0:valid, :] = dt_ref[0]
        z = pzx_ref[:, :D_INNER]
        u = pzx_ref[:, D_INNER:]
        dtr = pdt_ref[...]
    else:
        z = zx_ref[0, :, :D_INNER]
        u = zx_ref[0, :, D_INNER:]
        dtr = dt_ref[0]

    tail8 = tail_ref[...]
    row8 = _iota((8, 1), 0)
    acc = cb_ref[...] + cw_ref[3:4, :] * u
    for k in range(1, 4):
        rk = pltpu.roll(u, k, 0)
        tk = pltpu.roll(tail8, k, 0)
        first = jnp.where(row8 < k, tk, rk[:8])
        sh = jnp.concatenate([first, rk[8:]], axis=0)
        acc = acc + cw_ref[3 - k:4 - k, :] * sh
    if valid >= 8:
        tail_ref[...] = u[valid - 8:valid]
    else:
        s = 8 - valid
        tail_ref[...] = jnp.where(row8 < s, pltpu.roll(tail8, s, 0), pltpu.roll(u[:8], s, 0))
    xbc_ref[...] = _silu(acc)

    lane = _iota((1, LANES), 1)
    rowl = _iota((L, 1), 0)
    xdt = dtr + dtb_ref[...]
    dt = jnp.maximum(xdt, 0.0) + jnp.log1p(jnp.exp(-jnp.abs(xdt)))
    dt = jnp.where((lane < SSM_HEADS) & (rowl < valid), dt, 0.0)
    a_neg = -jnp.exp(alog_ref[...])
    da = dt * a_neg
    rr = _iota((L, L), 0)
    cc = _iota((L, L), 1)
    causal = rr >= cc
    tri = jnp.where(causal, 1.0, 0.0).astype(BF16)
    hi, mid, lo = _split3(da)
    a_cs = _dot(tri, hi) + _dot(tri, mid) + _dot(tri, lo)
    a_cs_t = a_cs.T
    dt_t = dt.T
    tdt_t = dt_t * jnp.exp(a_cs_t[:, L - 1:L] - a_cs_t)
    half = _iota((1, LANES), 1) < SSM_HEADDIM

    for g in range(SSM_GROUPS):
        b_g = xbc_ref[:, D_INNER + g * D_STATE:D_INNER + (g + 1) * D_STATE]
        c_g = xbc_ref[:, D_INNER + SSM_GROUPS * D_STATE + g * D_STATE:
                      D_INNER + SSM_GROUPS * D_STATE + (g + 1) * D_STATE]
        cb = _dot_nt(c_g.astype(BF16), b_g.astype(BF16))
        b_t = b_g.T
        for jj in range(npair // SSM_GROUPS):
            j = g * (npair // SSM_GROUPS) + jj
            cols = slice(j * LANES, (j + 1) * LANES)
            xs_b = xbc_ref[:, cols].astype(BF16)
            st_p = st_ref[:, cols]
            rhs = jnp.concatenate([xs_b, st_p.astype(BF16)], axis=0)
            ys, ss, cds = [], [], []
            for h in (2 * j, 2 * j + 1):
                colb = jnp.broadcast_to(a_cs[:, h:h + 1], (L, LANES))
                seg = colb - a_cs_t[h:h + 1, :]
                dec = jnp.exp(jnp.where(causal, seg, -jnp.inf))
                m = cb * dec * dt_t[h:h + 1, :]
                ce = c_g * jnp.exp(colb)
                lhs = jnp.concatenate([m.astype(BF16), ce.astype(BF16)], axis=1)
                ys.append(_dot(lhs, rhs))
                ss.append(_dot((b_t * tdt_t[h:h + 1, :]).astype(BF16), xs_b))
                cds.append(jnp.exp(colb[L - 1:L, :]))
            yacc_ref[:, cols] = jnp.where(half, ys[0], ys[1])
            st_ref[:, cols] = jnp.where(half, st_p * cds[0] + ss[0], st_p * cds[1] + ss[1])

    y = yacc_ref[...] + d_ref[...] * xbc_ref[:, :D_INNER]
    yn = _rms(y * _silu(z), ny_ref[...])
    if valid < L:
        y_ref[0] = yn[0:valid].astype(y_ref.dtype)
    else:
        y_ref[0] = yn.astype(y_ref.dtype)

    @pl.when(c == nchunks - 1)
    def _():
        convo_ref[0] = tail_ref[...]
        for j in range(npair):
            ssmo_ref[0, j * LANES:(j + 1) * LANES, :] = st_ref[:, j * LANES:(j + 1) * LANES].T


def _ssd(zx, dtr, conv_w, conv_b, dt_bias, a_log, d_skip, norm_y, conv0, ssm0):
    b, l, _ = zx.shape
    if l >= CHUNK:
        lb, valid, nchunks = CHUNK, CHUNK, l // CHUNK
    else:
        lb, valid, nchunks = l, l, 1
    pad_lanes = LANES - SSM_HEADS
    dtb = jnp.pad(dt_bias.astype(F32), (0, pad_lanes)).reshape(1, LANES)
    alog = jnp.pad(a_log.astype(F32), (0, pad_lanes)).reshape(1, LANES)
    d_exp = jnp.repeat(d_skip.astype(F32), SSM_HEADDIM).reshape(1, D_INNER)
    conv0p = jnp.pad(conv0, ((0, 0), (5, 0), (0, 0)))
    ssm0f = ssm0.reshape(b, SSM_HEADS * SSM_HEADDIM, D_STATE)
    const = lambda bi, c: (0, 0)
    in_specs = [
        pl.BlockSpec((1, lb, ZXBC_DIM), lambda bi, c: (bi, c, 0)),
        pl.BlockSpec((1, lb, LANES), lambda bi, c: (bi, c, 0)),
        pl.BlockSpec((4, CONV_DIM), const),
        pl.BlockSpec((1, CONV_DIM), const),
        pl.BlockSpec((1, LANES), const),
        pl.BlockSpec((1, LANES), const),
        pl.BlockSpec((1, D_INNER), const),
        pl.BlockSpec((1, D_INNER), const),
        pl.BlockSpec((1, 8, CONV_DIM), lambda bi, c: (bi, 0, 0)),
        pl.BlockSpec((1, D_INNER, D_STATE), lambda bi, c: (bi, 0, 0)),
    ]
    out_shape = [jax.ShapeDtypeStruct((b, l, D_INNER), BF16),
                 jax.ShapeDtypeStruct((b, 8, CONV_DIM), F32),
                 jax.ShapeDtypeStruct((b, D_INNER, D_STATE), F32)]
    out_specs = [pl.BlockSpec((1, lb, D_INNER), lambda bi, c: (bi, c, 0)),
                 pl.BlockSpec((1, 8, CONV_DIM), lambda bi, c: (bi, 0, 0)),
                 pl.BlockSpec((1, D_INNER, D_STATE), lambda bi, c: (bi, 0, 0))]
    scratch = [pltpu.VMEM((D_STATE, D_INNER), F32), pltpu.VMEM((8, CONV_DIM), F32),
               pltpu.VMEM((CHUNK, CONV_DIM), F32), pltpu.VMEM((CHUNK, D_INNER), F32)]
    if valid < CHUNK:
        scratch += [pltpu.VMEM((CHUNK, ZXBC_DIM), F32), pltpu.VMEM((CHUNK, LANES), F32)]
    y, convo, ssmo = pl.pallas_call(
        functools.partial(_ssd_body, valid=valid, nchunks=nchunks),
        grid=(b, nchunks), in_specs=in_specs, out_specs=out_specs, out_shape=out_shape,
        scratch_shapes=scratch, compiler_params=_params(("parallel", "arbitrary")), name="ssd")(
            zx, dtr, conv_w, conv_b.reshape(1, CONV_DIM), dtb, alog, d_exp, norm_y.reshape(1, D_INNER),
            conv0p, ssm0f)
    return y, convo[:, 5:], ssmo.reshape(b, SSM_HEADS, SSM_HEADDIM, D_STATE)


def _router_body(x_ref, g_ref, w_ref, b_ref, xn_ref, slab_ref, cnt_ref, base_ref, su_ref):
    tm = x_ref.shape[0]

    @pl.when(pl.program_id(0) == 0)
    def _():
        base_ref[...] = jnp.zeros_like(base_ref)
        su_ref[...] = jnp.where(_iota((tm, tm), 0) < _iota((tm, tm), 1), 1.0, 0.0).astype(BF16)

    xn = _rms(x_ref[...], g_ref[...])
    _rows_to_tiles(xn_ref, xn)
    xh, xm, xl = _split3(xn)
    wh = w_ref[0]
    wm = w_ref[1]
    wl = w_ref[2]
    logits = (_dot_nt(wh, xh) + _dot_nt(wh, xm) + _dot_nt(wm, xh)
              + _dot_nt(wh, xl) + _dot_nt(wm, xm) + _dot_nt(wl, xh)) + b_ref[...]
    gl = [logits[i:i + 1, :] for i in range(N_EXPERT_GROUPS)]
    best = gl[0]
    sel = jnp.zeros((1, tm), I32)
    for i in range(1, N_EXPERT_GROUPS):
        upd = gl[i] > best
        sel = jnp.where(upd, i, sel)
        best = jnp.where(upd, gl[i], best)
    den = jnp.exp(gl[0] - best)
    for i in range(1, N_EXPERT_GROUPS):
        den = den + jnp.exp(gl[i] - best)
    g_w = 1.0 / den
    chosen = logits[8:8 + EXPERTS_PER_GROUP, :]
    for i in range(1, N_EXPERT_GROUPS):
        chosen = jnp.where(sel == i, logits[8 + i * EXPERTS_PER_GROUP:8 + (i + 1) * EXPERTS_PER_GROUP, :], chosen)
    idx = _iota((EXPERTS_PER_GROUP, tm), 0)
    m1 = jnp.max(chosen, axis=0, keepdims=True)
    i1 = jnp.min(jnp.where(chosen == m1, idx, EXPERTS_PER_GROUP), axis=0, keepdims=True)
    rest = jnp.where(idx == i1, -jnp.inf, chosen)
    m2 = jnp.max(rest, axis=0, keepdims=True)
    i2 = jnp.min(jnp.where(rest == m2, idx, EXPERTS_PER_GROUP), axis=0, keepdims=True)
    t = jnp.exp(m2 - m1)
    inv = 1.0 / (1.0 + t)
    w0 = inv * g_w
    w1 = (t * inv) * g_w
    e0 = sel * EXPERTS_PER_GROUP + i1
    e1 = sel * EXPERTS_PER_GROUP + i2

    eidx = _iota((N_EXPERTS, tm), 0)
    oh0 = eidx == e0
    oh1 = eidx == e1
    oh = jnp.where(oh0, 1.0, jnp.where(oh1, 1.0, 0.0))
    base = base_ref[...]
    before = _dot(oh.astype(BF16), su_ref[...]) + jnp.concatenate([base] * (tm // LANES), axis=1)
    rank0 = jnp.sum(jnp.where(oh0, before, 0.0), axis=0, keepdims=True)
    rank1 = jnp.sum(jnp.where(oh1, before, 0.0), axis=0, keepdims=True)
    new_base = base + jnp.sum(oh, axis=1, keepdims=True)
    base_ref[...] = new_base
    cnt_ref[...] = new_base

    r = _iota((8, tm), 0)
    rows = (e0.astype(F32), e1.astype(F32), w0, w1, rank0, rank1)
    slab = jnp.zeros((8, tm), F32)
    for i, v in enumerate(rows):
        slab = jnp.where(r == i, v, slab)
    slab_ref[...] = slab


def _router(x, gain, w_split, bias, tm):
    n = x.shape[0]
    xn, slab, cnt = pl.pallas_call(
        _router_body, grid=(n // tm,),
        in_specs=[pl.BlockSpec((tm, D_MODEL), lambda i: (i, 0)),
                  pl.BlockSpec((1, D_MODEL), lambda i: (0, 0)),
                  pl.BlockSpec((3, ROUTER_ROWS, D_MODEL), lambda i: (0, 0, 0)),
                  pl.BlockSpec((ROUTER_ROWS, 1), lambda i: (0, 0))],
        out_specs=[pl.BlockSpec((tm * ROW_TILES, LANES), lambda i: (i, 0)),
                   pl.BlockSpec((8, tm), lambda i: (0, i)),
                   pl.BlockSpec((N_EXPERTS, LANES), lambda i: (0, 0))],
        out_shape=[jax.ShapeDtypeStruct((n * ROW_TILES, LANES), F32), jax.ShapeDtypeStruct((8, n), F32),
                   jax.ShapeDtypeStruct((N_EXPERTS, LANES), F32)],
        scratch_shapes=[pltpu.VMEM((N_EXPERTS, LANES), F32), pltpu.VMEM((tm, tm), BF16)],
        compiler_params=_params(("arbitrary",)), name="router")(x, gain.reshape(1, D_MODEL), w_split, bias)
    return xn, slab, cnt


def _moe_plan(slab, cnt, tm, blk):
    n = slab.shape[1]
    e_ids = slab[0:2].astype(I32)
    ranks = slab[4:6].astype(I32)
    counts = cnt[:, 0].astype(I32)
    padded = ((counts + tm - 1) // tm) * tm
    pend = jnp.cumsum(padded)
    pstart = pend - padded
    onehot = e_ids[..., None] == jnp.arange(N_EXPERTS, dtype=I32)
    dest = jnp.sum(jnp.where(onehot, pstart, 0), axis=-1) + ranks
    dest = dest.T.reshape(n // blk, 1, 2 * blk)
    slots = 2 * n + N_EXPERTS * tm
    n_tiles = slots // tm
    n_used = (pend[-1] // tm).astype(I32)
    tile_start = jnp.arange(n_tiles, dtype=I32) * tm
    tile_e = jnp.sum((pend[None, :] <= tile_start[:, None]).astype(I32), axis=1)
    tile_e = jnp.minimum(tile_e, N_EXPERTS - 1)
    last_e = jnp.sum(jnp.where(jnp.arange(n_tiles) == n_used - 1, tile_e, 0))
    tile_e = jnp.where(jnp.arange(n_tiles) < n_used, tile_e, last_e).astype(I32)
    return dest, tile_e, n_used.reshape(1), slots


def _dispatch_body(dest_ref, xn_hbm, xs_in_hbm, xs_hbm, sem, *, blk):
    del xs_in_hbm
    base = pl.program_id(0) * blk

    def issue(i, carry):
        src = xn_hbm.at[base + i]
        pltpu.make_async_copy(src, xs_hbm.at[dest_ref[0, 0, 2 * i]], sem).start()
        pltpu.make_async_copy(src, xs_hbm.at[dest_ref[0, 0, 2 * i + 1]], sem).start()
        return carry

    lax.fori_loop(0, blk, issue, 0, unroll=8)

    def drain(i, carry):
        pltpu.make_async_copy(xn_hbm.at[0], xs_hbm.at[0], sem).wait()
        return carry

    lax.fori_loop(0, 2 * blk, drain, 0, unroll=8)


def _dispatch(xn, dest, slots, blk):
    n = xn.shape[0] // ROW_TILES
    xn3 = xn.reshape(n, ROW_TILES, LANES)
    xs0 = jnp.zeros((slots, ROW_TILES, LANES), F32)
    xs = pl.pallas_call(
        functools.partial(_dispatch_body, blk=blk), grid=(n // blk,),
        in_specs=[pl.BlockSpec((1, 1, 2 * blk), lambda i: (i, 0, 0), memory_space=pltpu.SMEM),
                  pl.BlockSpec(memory_space=pl.ANY), pl.BlockSpec(memory_space=pl.ANY)],
        out_specs=pl.BlockSpec(memory_space=pl.ANY),
        out_shape=jax.ShapeDtypeStruct((slots, ROW_TILES, LANES), F32),
        scratch_shapes=[pltpu.SemaphoreType.DMA(())],
        input_output_aliases={2: 0},
        compiler_params=_params(("arbitrary",)), name="dispatch")(dest, xn3, xs0)
    return xs.reshape(slots * ROW_TILES, LANES)


def _moe_body(te_ref, nu_ref, xs_ref, wg_ref, wu_ref, wd_ref, o_ref, wgb_ref, wub_ref, wdb_ref, *, tm):
    t = pl.program_id(0)
    e = te_ref[t]
    prev = te_ref[jnp.maximum(t - 1, 0)]

    @pl.when((t == 0) | (e != prev))
    def _():
        wgb_ref[...] = wg_ref[0].astype(BF16)
        wub_ref[...] = wu_ref[0].astype(BF16)
        wdb_ref[...] = wd_ref[0].astype(BF16)

    @pl.when(t < nu_ref[0])
    def _():
        x = _tiles_to_rows(xs_ref, tm).astype(BF16)
        g = _dot(x, wgb_ref[...])
        u = _dot(x, wub_ref[...])
        hid = _silu(g) * u
        _rows_to_tiles(o_ref, _dot(hid.astype(BF16), wdb_ref[...]))


def _moe_experts(xs, tile_e, n_used, w_gate, w_up, w_down, tm):
    slots = xs.shape[0] // ROW_TILES
    n_tiles = slots // tm
    row = lambda t, te, nu: (jnp.minimum(t, nu[0] - 1), 0)
    wsel = lambda t, te, nu: (te[t], 0, 0)
    grid_spec = pltpu.PrefetchScalarGridSpec(
        num_scalar_prefetch=2, grid=(n_tiles,),
        in_specs=[pl.BlockSpec((tm * ROW_TILES, LANES), row),
                  pl.BlockSpec((1, D_MODEL, D_EXPERT), wsel),
                  pl.BlockSpec((1, D_MODEL, D_EXPERT), wsel),
                  pl.BlockSpec((1, D_EXPERT, D_MODEL), wsel)],
        out_specs=pl.BlockSpec((tm * ROW_TILES, LANES), row),
        scratch_shapes=[pltpu.VMEM((D_MODEL, D_EXPERT), BF16), pltpu.VMEM((D_MODEL, D_EXPERT), BF16),
                        pltpu.VMEM((D_EXPERT, D_MODEL), BF16)])
    return pl.pallas_call(
        functools.partial(_moe_body, tm=tm), grid_spec=grid_spec,
        out_shape=jax.ShapeDtypeStruct((slots * ROW_TILES, LANES), F32),
        compiler_params=_params(("arbitrary",)), name="moe_experts")(tile_e, n_used, xs, w_gate, w_up, w_down)


def _ple_body(*refs, final, tm):
    dest_ref, x_ref, gw_ref, p_ref, g_ref, wg_ref, wp_ref = refs[:7]
    if final:
        fg_ref, ys_hbm, o_ref, f_ref, y0_ref, y1_ref, sem = refs[7:]
    else:
        ys_hbm, o_ref, y0_ref, y1_ref, sem = refs[7:]

    def issue(i, carry):
        row = pl.multiple_of(i * ROW_TILES, ROW_TILES)
        pltpu.make_async_copy(ys_hbm.at[dest_ref[0, 0, 2 * i]], y0_ref.at[pl.ds(row, ROW_TILES)], sem).start()
        pltpu.make_async_copy(ys_hbm.at[dest_ref[0, 0, 2 * i + 1]], y1_ref.at[pl.ds(row, ROW_TILES)], sem).start()
        return carry

    lax.fori_loop(0, tm, issue, 0, unroll=8)
    pe = _dot(p_ref[...].astype(BF16), wp_ref[...])

    def drain(i, carry):
        pltpu.make_async_copy(ys_hbm.at[0], y0_ref.at[pl.ds(0, ROW_TILES)], sem).wait()
        return carry

    lax.fori_loop(0, 2 * tm, drain, 0, unroll=8)
    gw = gw_ref[...]
    x2 = x_ref[...] + gw[:, 2:3] * _tiles_to_rows(y0_ref, tm) + gw[:, 3:4] * _tiles_to_rows(y1_ref, tm)
    xn = _rms(x2, g_ref[...])
    gate = jax.nn.sigmoid(_dot(xn.astype(BF16), wg_ref[...]))
    out = x2 + pe * gate
    o_ref[...] = out
    if final:
        f_ref[...] = _rms(out, fg_ref[...])


def _ple(x, ys, dest, gw, p, gain, w_gate, w_proj, final_gain, tm):
    n = x.shape[0]
    dp = p.shape[1]
    final = final_gain is not None
    rowspec = pl.BlockSpec((tm, D_MODEL), lambda i: (i, 0))
    vec = pl.BlockSpec((1, D_MODEL), lambda i: (0, 0))
    in_specs = [pl.BlockSpec((1, 1, 2 * tm), lambda i: (i, 0, 0), memory_space=pltpu.SMEM),
                rowspec, pl.BlockSpec((tm, 8), lambda i: (i, 0)), pl.BlockSpec((tm, dp), lambda i: (i, 0)), vec,
                pl.BlockSpec((D_MODEL, D_MODEL), lambda i: (0, 0)),
                pl.BlockSpec((dp, D_MODEL), lambda i: (0, 0))]
    args = [dest, x, gw, p, gain.reshape(1, D_MODEL), w_gate, w_proj]
    out_shape = [jax.ShapeDtypeStruct((n, D_MODEL), F32)]
    out_specs = [rowspec]
    if final:
        in_specs.append(vec)
        args.append(final_gain.reshape(1, D_MODEL))
        out_shape.append(jax.ShapeDtypeStruct((n, D_MODEL), F32))
        out_specs.append(rowspec)
    in_specs.append(pl.BlockSpec(memory_space=pl.ANY))
    args.append(ys.reshape(-1, ROW_TILES, LANES))
    outs = pl.pallas_call(
        functools.partial(_ple_body, final=final, tm=tm), grid=(n // tm,), in_specs=in_specs,
        out_specs=out_specs, out_shape=out_shape,
        scratch_shapes=[pltpu.VMEM((tm * ROW_TILES, LANES), F32), pltpu.VMEM((tm * ROW_TILES, LANES), F32),
                        pltpu.SemaphoreType.DMA(())],
        compiler_params=_params(("arbitrary",)), name="ple")(*args)
    return (outs[0], outs[1]) if final else (outs[0], None)


def _attn_block(q, kcat, vcat, first_key):
    R = CHUNK
    row = _iota((R, 2 * R), 0)
    col = _iota((R, 2 * R), 1)
    band = (col >= jnp.maximum(row, first_key)) & (col <= row + R)
    s = jnp.where(band, _dot_nt(q, kcat) * (HEAD_DIM ** -0.5), -jnp.inf)
    m = jnp.max(s, axis=-1, keepdims=True)
    p = jnp.exp(s - m)
    l = jnp.sum(p, axis=-1, keepdims=True)
    o = _dot(p.astype(BF16), vcat) * (1.0 / l)
    return o, jnp.broadcast_to(m + jnp.log(l), (R, LANES))


def _attn_body(q0_ref, q1_ref, q2_ref, k_ref, v_ref, o_ref, kp_ref, vp_ref, og_ref, lg_ref):
    R = CHUNK
    first_key = jnp.where(pl.program_id(2) > 0, 0, R)

    @pl.when(pl.program_id(2) == 0)
    def _():
        kp_ref[...] = jnp.zeros_like(kp_ref)
        vp_ref[...] = jnp.zeros_like(vp_ref)

    def rows(ref, start, dil):
        idx = pl.ds(start, R) if dil == 1 else pl.ds(start, R, stride=dil)
        return ref[idx, :]

    def put(ref, start, dil, val):
        idx = pl.ds(start, R) if dil == 1 else pl.ds(start, R, stride=dil)
        ref[idx, :] = val

    for g, (q_ref, (_, dil)) in enumerate(zip((q0_ref, q1_ref, q2_ref), PATTERNS)):
        nblk = ATTN_SPAN // (R * dil)

        def lane(r, carry, q_ref=q_ref, dil=dil, nblk=nblk, g=g):
            last = r + dil * R * (nblk - 1)
            kp = rows(kp_ref, last, dil).astype(BF16)
            vp = rows(vp_ref, last, dil).astype(BF16)
            for cb in range(nblk):
                start = r + dil * R * cb
                q = rows(q_ref, start, dil).astype(BF16)
                kc = rows(k_ref, start, dil).astype(BF16)
                vc = rows(v_ref, start, dil).astype(BF16)
                o, lse = _attn_block(q, jnp.concatenate([kp, kc], axis=0), jnp.concatenate([vp, vc], axis=0),
                                     first_key if cb == 0 else 0)
                put(og_ref.at[g], start, dil, o)
                put(lg_ref.at[g], start, dil, lse)
                kp, vp = kc, vc
            return carry

        if dil == 1:
            lane(0, 0)
        else:
            lax.fori_loop(0, dil, lane, 0)

    step = 256
    for i in range(ATTN_SPAN // step):
        sl = pl.ds(i * step, step)
        l0 = lg_ref[0, sl, :]
        l1 = lg_ref[1, sl, :]
        l2 = lg_ref[2, sl, :]
        m = jnp.maximum(jnp.maximum(l0, l1), l2)
        e0 = jnp.exp(l0 - m)
        e1 = jnp.exp(l1 - m)
        e2 = jnp.exp(l2 - m)
        num = e0 * og_ref[0, sl, :] + e1 * og_ref[1, sl, :] + e2 * og_ref[2, sl, :]
        o_ref[sl, :] = num * (1.0 / (e0 + e1 + e2))
    kp_ref[...] = k_ref[...]
    vp_ref[...] = v_ref[...]


def _attn_prompt(q, k, v, b, s):
    nspan = s // ATTN_SPAN
    blk = (ATTN_SPAN, HEAD_DIM)
    qspec = lambda g: pl.BlockSpec(blk, lambda bi, h, c: (bi * nspan + c, g * N_KV_HEADS + h))
    kvspec = pl.BlockSpec(blk, lambda bi, h, c: (bi * nspan + c, h))
    return pl.pallas_call(
        _attn_body, grid=(b, N_KV_HEADS, nspan),
        in_specs=[qspec(0), qspec(1), qspec(2), kvspec, kvspec],
        out_specs=kvspec, out_shape=jax.ShapeDtypeStruct((b * s, ATTN_DIM), F32),
        scratch_shapes=[pltpu.VMEM(blk, F32), pltpu.VMEM(blk, F32),
                        pltpu.VMEM((3,) + blk, F32), pltpu.VMEM((3,) + blk, F32)],
        compiler_params=_params(("parallel", "parallel", "arbitrary")), name="attn_prompt")(q, q, q, k, v)


def _attn_sample_body(q_ref, kc_ref, vc_ref, kn_ref, vn_ref, o_ref, *, wb, t_new):
    rows = q_ref.shape[2]
    q = q_ref[0, 0]
    scale = HEAD_DIM ** -0.5
    rho = _iota((rows, 1), 0)
    t = rho & (t_new - 1)
    g = rho >> (t_new.bit_length() - 1)
    dil = jnp.where(g == 1, PATTERNS[1][1], jnp.where(g == 2, PATTERNS[2][1], PATTERNS[0][1]))
    win = jnp.where(g == 1, PATTERNS[1][0], jnp.where(g == 2, PATTERNS[2][0], PATTERNS[0][0]))
    key = _iota((rows, wb), 1)
    dist = wb + t - key
    valid = (dist <= win) & ((dist & (dil - 1)) == 0)
    sc = _dot_nt(q.astype(BF16), kc_ref[0].astype(BF16)) * scale
    sc = jnp.where(valid, sc, -jnp.inf)
    m = jnp.max(sc, axis=-1, keepdims=True)
    kn = kn_ref[0]
    vn = vn_ref[0]
    qf = q.astype(BF16).astype(F32)
    s_new = []
    for u in range(t_new):
        su = jnp.sum(qf * kn[u:u + 1, :].astype(BF16).astype(F32), axis=-1, keepdims=True) * scale
        du = t - u
        su = jnp.where((du >= 0) & ((du & (dil - 1)) == 0), su, -jnp.inf)
        s_new.append(su)
        m = jnp.maximum(m, su)
    p = jnp.exp(sc - m)
    l = jnp.sum(p, axis=-1, keepdims=True)
    o = _dot(p.astype(BF16), vc_ref[0].astype(BF16))
    for u in range(t_new):
        pu = jnp.exp(s_new[u] - m)
        l = l + pu
        o = o + pu.astype(BF16).astype(F32) * vn[u:u + 1, :].astype(BF16).astype(F32)
    o = o * (1.0 / l)
    lse = jnp.broadcast_to(m + jnp.log(l), (rows, LANES))
    lse1 = pltpu.roll(lse, rows - t_new, 0)
    lse2 = pltpu.roll(lse, rows - 2 * t_new, 0)
    o1 = pltpu.roll(o, rows - t_new, 0)
    o2 = pltpu.roll(o, rows - 2 * t_new, 0)
    mm = jnp.maximum(jnp.maximum(lse, lse1), lse2)
    e0 = jnp.exp(lse - mm)
    e1 = jnp.exp(lse1 - mm)
    e2 = jnp.exp(lse2 - mm)
    comb = (e0 * o + e1 * o1 + e2 * o2) * (1.0 / (e0 + e1 + e2))
    o_ref[0] = comb[0:t_new]


def _attn_sample(q, cache_k, cache_v, k_new, v_new, b, t_new):
    wb = cache_k.shape[1]
    rows = 16
    qh = q.reshape(b, t_new, 3, N_KV_HEADS, HEAD_DIM).transpose(0, 3, 2, 1, 4).reshape(b, N_KV_HEADS, 3 * t_new, HEAD_DIM)
    qh = jnp.pad(qh, ((0, 0), (0, 0), (0, rows - 3 * t_new), (0, 0)))
    kn = jnp.pad(k_new.reshape(b, t_new, ATTN_DIM), ((0, 0), (0, 8 - t_new), (0, 0)))
    vn = jnp.pad(v_new.reshape(b, t_new, ATTN_DIM), ((0, 0), (0, 8 - t_new), (0, 0)))
    head = lambda bi, h: (bi, 0, h)
    o = pl.pallas_call(
        functools.partial(_attn_sample_body, wb=wb, t_new=t_new), grid=(b, N_KV_HEADS),
        in_specs=[pl.BlockSpec((1, 1, rows, HEAD_DIM), lambda bi, h: (bi, h, 0, 0)),
                  pl.BlockSpec((1, wb, HEAD_DIM), head), pl.BlockSpec((1, wb, HEAD_DIM), head),
                  pl.BlockSpec((1, 8, HEAD_DIM), head), pl.BlockSpec((1, 8, HEAD_DIM), head)],
        out_specs=pl.BlockSpec((1, t_new, HEAD_DIM), head),
        out_shape=jax.ShapeDtypeStruct((b, t_new, ATTN_DIM), F32),
        compiler_params=_params(("parallel", "arbitrary")), name="attn_sample")(qh, cache_k, cache_v, kn, vn)
    return o.reshape(b * t_new, ATTN_DIM)


def _rope_tables(pos0, l, reps):
    inv_freq = ROPE_THETA ** (-jnp.arange(0, HEAD_DIM, 2, dtype=F32) / HEAD_DIM)
    pos = pos0 + jnp.arange(l, dtype=jnp.int32)
    ang = pos.astype(F32)[:, None] * inv_freq[None, :]
    cos = jnp.cos(ang)
    sin = jnp.sin(ang)
    cosf = jnp.concatenate([cos, cos], axis=-1)
    sinf = jnp.concatenate([-sin, sin], axis=-1)
    if reps > 1:
        cosf = jnp.tile(cosf, (reps, 1))
        sinf = jnp.tile(sinf, (reps, 1))
    return cosf, sinf


def _prep_weights(prm):
    w = {}
    w['in_proj'] = prm['ssm_in_proj'].astype(BF16)
    w['in_dt'] = jnp.pad(prm['ssm_in_proj'][:, :, ZXBC_DIM:], ((0, 0), (0, 0), (0, LANES - SSM_HEADS))).astype(BF16)
    w['out_proj'] = prm['ssm_out_proj'].astype(BF16)
    w['w_kv'] = prm['w_kv'].astype(BF16)
    w['w_q'] = prm['attn_w_q'].astype(BF16)
    w['w_o'] = prm['attn_w_o'].astype(BF16)
    w['ple_gate'] = prm['ple_w_gate'].astype(BF16)
    w['ple_proj'] = prm['ple_w_proj'].astype(BF16)
    wr = jnp.concatenate([jnp.swapaxes(prm['moe_w_rg'], 1, 2),
                          jnp.zeros((DEPTH, 8 - N_EXPERT_GROUPS, D_MODEL), F32),
                          jnp.swapaxes(prm['moe_w_re'], 1, 2)], axis=1)
    w['router'] = jnp.stack(_split3(wr), axis=1)
    w['router_b'] = jnp.concatenate([prm['moe_b_rg'], jnp.zeros((DEPTH, 8 - N_EXPERT_GROUPS), F32),
                                     prm['moe_b_re']], axis=1).reshape(DEPTH, ROUTER_ROWS, 1)
    return w


def _trunk(x, p, pos0, conv_state, ssm_state, cache_k, cache_v, prm, w):
    b, l, _ = x.shape
    n = b * l
    sample = cache_k is not None
    tm = min(512, n)
    moe_tm = 256 if n >= 4096 else 32
    xf = x.reshape(n, D_MODEL)
    rope = _rope_tables(pos0, l, 1 if l >= tm else tm // l)
    new_conv, new_ssm = [], []
    k32 = v32 = None
    y_final = None
    for i in range(DEPTH):
        gain = prm['norm_mix'][i]
        if i < N_A_LAYERS:
            zx = _proj(xf, w['in_proj'][i], gain=gain, nout=ZXBC_DIM, tm=tm, tn=1024)
            dtr = _proj(xf, w['in_dt'][i], gain=gain, tm=tm, tn=LANES)
            y, cbuf, hs = _ssd(zx.reshape(b, l, ZXBC_DIM), dtr.reshape(b, l, LANES), prm['ssm_conv_w'][i],
                               prm['ssm_conv_b'][i], prm['ssm_dt_bias'][i], prm['ssm_a_log'][i], prm['ssm_d'][i],
                               prm['ssm_norm_y'][i], conv_state[i], ssm_state[i])
            new_conv.append(cbuf)
            new_ssm.append(hs)
            xf = _proj(y.reshape(n, D_INNER), w['out_proj'][i], residual=xf, tm=tm, tn=1024)
        else:
            j = i - N_A_LAYERS
            q = _proj(xf, w['w_q'][j], gain=gain, rope=rope, tm=tm, tn=1024)
            if sample:
                att = _attn_sample(q, cache_k, cache_v, k32, v32, b, l)
            else:
                att = _attn_prompt(q, k32, v32, b, l)
            xf = _proj(att, w['w_o'][j], residual=xf, tm=tm, tn=1024)
        xn, slab, cnt = _router(xf, prm['norm_ffn'][i], w['router'][i], w['router_b'][i], tm)
        dest, tile_e, n_used, slots = _moe_plan(slab, cnt, moe_tm, tm)
        xs = _dispatch(xn, dest, slots, tm)
        ys = _moe_experts(xs, tile_e, n_used, prm['moe_w_gate'][i], prm['moe_w_up'][i], prm['moe_w_down'][i], moe_tm)
        xf, y_final = _ple(xf, ys, dest, slab.T, p[i].reshape(n, -1), prm['ple_norm'][i], w['ple_gate'][i],
                           w['ple_proj'][i], prm['norm_final'] if i == DEPTH - 1 else None, tm)
        if i == N_A_LAYERS - 1:
            k32 = _proj(xf, w['w_kv'], gain=prm['norm_kv'], rope=rope, nout=ATTN_DIM, tm=tm, tn=1024)
            v32 = _proj(xf, w['w_kv'][:, ATTN_DIM:], gain=prm['norm_kv'], tm=tm, tn=1024)
    return (y_final.reshape(b, l, D_MODEL), jnp.stack(new_conv), jnp.stack(new_ssm),
            k32.reshape(b, l, N_KV_HEADS, HEAD_DIM), v32.reshape(b, l, N_KV_HEADS, HEAD_DIM))


def kernel(x_prompt, x_sample, state_ssm, state_conv, cache_k, cache_v, p_prompt, p_sample, norm_mix, ssm_in_proj, ssm_conv_w, ssm_conv_b, ssm_dt_bias, ssm_a_log, ssm_d, ssm_norm_y, ssm_out_proj, norm_kv, w_kv, attn_w_q, attn_w_o, norm_ffn, moe_w_rg, moe_b_rg, moe_w_re, moe_b_re, moe_w_gate, moe_w_up, moe_w_down, ple_w_proj, ple_w_gate, ple_norm, norm_final):
    prm = dict(norm_mix=norm_mix, ssm_in_proj=ssm_in_proj, ssm_conv_w=ssm_conv_w, ssm_conv_b=ssm_conv_b,
               ssm_dt_bias=ssm_dt_bias, ssm_a_log=ssm_a_log, ssm_d=ssm_d, ssm_norm_y=ssm_norm_y,
               ssm_out_proj=ssm_out_proj, norm_kv=norm_kv, w_kv=w_kv, attn_w_q=attn_w_q, attn_w_o=attn_w_o,
               norm_ffn=norm_ffn, moe_w_rg=moe_w_rg, moe_b_rg=moe_b_rg, moe_w_re=moe_w_re, moe_b_re=moe_b_re,
               moe_w_gate=moe_w_gate, moe_w_up=moe_w_up, moe_w_down=moe_w_down, ple_w_proj=ple_w_proj,
               ple_w_gate=ple_w_gate, ple_norm=ple_norm, norm_final=norm_final)
    w = _prep_weights(prm)
    b, s, _ = x_prompt.shape
    db, dl, _ = x_sample.shape
    past_len = 8192
    conv0 = jnp.zeros((N_A_LAYERS, b, 3, CONV_DIM), x_prompt.dtype)
    ssm0 = jnp.zeros((N_A_LAYERS, b, SSM_HEADS, SSM_HEADDIM, D_STATE), state_ssm.dtype)
    y_p, conv_p, ssm_p, k_p, v_p = _trunk(x_prompt, p_prompt, 0, conv0, ssm0, None, None, prm, w)
    keep = min(2048, s)
    k_p = k_p[:, s - keep:]
    v_p = v_p[:, s - keep:]
    wb = cache_k.shape[1]
    y_s, conv_s, ssm_s, k_s, v_s = _trunk(x_sample, p_sample, past_len, state_conv, state_ssm,
                                          cache_k.reshape(db, wb, ATTN_DIM), cache_v.reshape(db, wb, ATTN_DIM), prm, w)
    return (y_p, y_s, ssm_p, conv_p, k_p, v_p, ssm_s, conv_s, k_s, v_s)
```

```python
import functools

import jax
import jax.numpy as jnp
from jax import lax
from jax.experimental import pallas as pl
from jax.experimental.pallas import tpu as pltpu

F32 = jnp.float32
BF16 = jnp.bfloat16
I32 = jnp.int32

EPS = 1e-6
D_MODEL = 1024
D_INNER = 2048
SSM_HEADS = 32
SSM_HEADDIM = 64
SSM_GROUPS = 4
D_STATE = 128
CONV_DIM = 3072
ZXBC_DIM = D_INNER + CONV_DIM
CHUNK = 128
HEAD_DIM = 128
N_KV_HEADS = 8
ATTN_DIM = 1024
PATTERNS = ((128, 1), (512, 4), (2048, 16))
ATTN_SPAN = 2048
ROPE_THETA = 10000.0
N_EXPERTS = 32
N_EXPERT_GROUPS = 4
EXPERTS_PER_GROUP = 8
D_EXPERT = 512
N_A_LAYERS = 2
DEPTH = 4
ROUTER_ROWS = 8 + N_EXPERTS
LANES = 128
SUBLANES = 8
ROW_TILES = D_MODEL // LANES
VMEM_LIMIT = 56 * 1024 * 1024


def _params(sem):
    return pltpu.CompilerParams(dimension_semantics=sem, vmem_limit_bytes=VMEM_LIMIT)


def _iota(shape, dim):
    return lax.broadcasted_iota(I32, shape, dim)


def _rms(x, gain):
    ms = jnp.mean(x * x, axis=-1, keepdims=True)
    return x * lax.rsqrt(ms + EPS) * gain


def _silu(x):
    return x * jax.nn.sigmoid(x)


def _dot(a, b):
    return jnp.dot(a, b, preferred_element_type=F32)


def _dot_nt(a, b):
    return lax.dot_general(a, b, (((1,), (1,)), ((), ())), preferred_element_type=F32)


def _split3(x):
    hi = x.astype(BF16)
    r = x - hi.astype(F32)
    mid = r.astype(BF16)
    lo = (r - mid.astype(F32)).astype(BF16)
    return hi, mid, lo


def _rows_to_tiles(ref, x):
    m = x.shape[0]
    for s in range(ROW_TILES):
        ref[pl.ds(s, m, stride=ROW_TILES), :] = x[:, s * LANES:(s + 1) * LANES]


def _tiles_to_rows(ref, m):
    return jnp.concatenate([ref[pl.ds(s, m, stride=ROW_TILES), :] for s in range(ROW_TILES)], axis=1)


def _proj_body(*refs, has_norm, has_rope, has_res, n_out):
    it = iter(refs)
    x_ref = next(it)
    w_ref = next(it)
    g_ref = next(it) if has_norm else None
    cos_ref = next(it) if has_rope else None
    sin_ref = next(it) if has_rope else None
    r_ref = next(it) if has_res else None
    o_refs = [next(it) for _ in range(n_out)]
    xb_ref = next(it)

    @pl.when(pl.program_id(1) == 0)
    def _():
        x = x_ref[...].astype(F32)
        if has_norm:
            x = _rms(x, g_ref[...])
        xb_ref[...] = x.astype(BF16)

    y = _dot(xb_ref[...], w_ref[...])
    if has_rope:
        cos = cos_ref[...]
        sin = sin_ref[...]
        parts = []
        for h in range(y.shape[1] // HEAD_DIM):
            t = y[:, h * HEAD_DIM:(h + 1) * HEAD_DIM]
            parts.append(t * cos + pltpu.roll(t, HEAD_DIM // 2, 1) * sin)
        y = parts[0] if len(parts) == 1 else jnp.concatenate(parts, axis=1)
    if has_res:
        y = y + r_ref[...]
    for o in o_refs:
        o[...] = y.astype(o.dtype)


def _proj(x, w, *, gain=None, rope=None, residual=None, out_dtypes=(F32,), nout=None, tm, tn):
    n, k = x.shape
    nout = w.shape[1] if nout is None else nout
    tn = min(tn, nout)
    grid = (n // tm, nout // tn)
    in_specs = [pl.BlockSpec((tm, k), lambda i, j: (i, 0)),
                pl.BlockSpec((k, tn), lambda i, j: (0, j))]
    args = [x, w]
    if gain is not None:
        in_specs.append(pl.BlockSpec((1, k), lambda i, j: (0, 0)))
        args.append(gain.reshape(1, k))
    if rope is not None:
        cos, sin = rope
        nblk = cos.shape[0] // tm
        for t in (cos, sin):
            in_specs.append(pl.BlockSpec((tm, HEAD_DIM), lambda i, j: (i % nblk, 0)))
            args.append(t)
    if residual is not None:
        in_specs.append(pl.BlockSpec((tm, tn), lambda i, j: (i, j)))
        args.append(residual)
    out_shape = [jax.ShapeDtypeStruct((n, nout), dt) for dt in out_dtypes]
    out_specs = [pl.BlockSpec((tm, tn), lambda i, j: (i, j)) for _ in out_dtypes]
    body = functools.partial(_proj_body, has_norm=gain is not None, has_rope=rope is not None,
                             has_res=residual is not None, n_out=len(out_dtypes))
    outs = pl.pallas_call(
        body, grid=grid, in_specs=in_specs, out_specs=out_specs, out_shape=out_shape,
        scratch_shapes=[pltpu.VMEM((tm, k), BF16)],
        compiler_params=_params(("parallel", "arbitrary")), name="proj")(*args)
    return outs[0] if len(outs) == 1 else outs


def _ssd_body(zx_ref, dt_ref, cw_ref, cb_ref, dtb_ref, alog_ref, d_ref, ny_ref, conv0_ref, ssm0_ref,
              y_ref, convo_ref, ssmo_ref, st_ref, tail_ref, xbc_ref, yacc_ref, *pad_refs, valid, nchunks):
    L = CHUNK
    c = pl.program_id(1)
    npair = SSM_HEADS // 2

    @pl.when(c == 0)
    def _():
        for j in range(npair):
            st_ref[:, j * LANES:(j + 1) * LANES] = ssm0_ref[0, j * LANES:(j + 1) * LANES, :].T
        tail_ref[...] = conv0_ref[0]

    if valid < L:
        pzx_ref, pdt_ref = pad_refs
        pzx_ref[...] = jnp.zeros_like(pzx_ref)
        pdt_ref[...] = jnp.zeros_like(pdt_ref)
        pzx_ref[0:valid, :] = zx_ref[0]
        pdt_ref[0:valid, :] = dt_ref[0]
        z = pzx_ref[:, :D_INNER]
        u = pzx_ref[:, D_INNER:]
        dtr = pdt_ref[...]
    else:
        z = zx_ref[0, :, :D_INNER]
        u = zx_ref[0, :, D_INNER:]
        dtr = dt_ref[0]

    tail8 = tail_ref[...]
    row8 = _iota((8, 1), 0)
    acc = cb_ref[...] + cw_ref[3:4, :] * u
    for k in range(1, 4):
        rk = pltpu.roll(u, k, 0)
        tk = pltpu.roll(tail8, k, 0)
        first = jnp.where(row8 < k, tk, rk[:8])
        sh = jnp.concatenate([first, rk[8:]], axis=0)
        acc = acc + cw_ref[3 - k:4 - k, :] * sh
    if valid >= 8:
        tail_ref[...] = u[valid - 8:valid]
    else:
        s = 8 - valid
        tail_ref[...] = jnp.where(row8 < s, pltpu.roll(tail8, s, 0), pltpu.roll(u[:8], s, 0))
    xbc_ref[...] = _silu(acc)

    lane = _iota((1, LANES), 1)
    rowl = _iota((L, 1), 0)
    xdt = dtr + dtb_ref[...]
    dt = jnp.maximum(xdt, 0.0) + jnp.log1p(jnp.exp(-jnp.abs(xdt)))
    dt = jnp.where((lane < SSM_HEADS) & (rowl < valid), dt, 0.0)
    a_neg = -jnp.exp(alog_ref[...])
    da = dt * a_neg
    rr = _iota((L, L), 0)
    cc = _iota((L, L), 1)
    causal = rr >= cc
    tri = jnp.where(causal, 1.0, 0.0).astype(BF16)
    hi, mid, lo = _split3(da)
    a_cs = _dot(tri, hi) + _dot(tri, mid) + _dot(tri, lo)
    a_cs_t = a_cs.T
    dt_t = dt.T
    tdt_t = dt_t * jnp.exp(a_cs_t[:, L - 1:L] - a_cs_t)
    half = _iota((1, LANES), 1) < SSM_HEADDIM

    for g in range(SSM_GROUPS):
        b_g = xbc_ref[:, D_INNER + g * D_STATE:D_INNER + (g + 1) * D_STATE]
        c_g = xbc_ref[:, D_INNER + SSM_GROUPS * D_STATE + g * D_STATE:
                      D_INNER + SSM_GROUPS * D_STATE + (g + 1) * D_STATE]
        cb = _dot_nt(c_g.astype(BF16), b_g.astype(BF16))
        b_t = b_g.T
        for jj in range(npair // SSM_GROUPS):
            j = g * (npair // SSM_GROUPS) + jj
            cols = slice(j * LANES, (j + 1) * LANES)
            xs_b = xbc_ref[:, cols].astype(BF16)
            st_p = st_ref[:, cols]
            rhs = jnp.concatenate([xs_b, st_p.astype(BF16)], axis=0)
            ys, ss, cds = [], [], []
            for h in (2 * j, 2 * j + 1):
                colb = jnp.broadcast_to(a_cs[:, h:h + 1], (L, LANES))
                seg = colb - a_cs_t[h:h + 1, :]
                dec = jnp.exp(jnp.where(causal, seg, -jnp.inf))
                m = cb * dec * dt_t[h:h + 1, :]
                ce = c_g * jnp.exp(colb)
                lhs = jnp.concatenate([m.astype(BF16), ce.astype(BF16)], axis=1)
                ys.append(_dot(lhs, rhs))
                ss.append(_dot((b_t * tdt_t[h:h + 1, :]).astype(BF16), xs_b))
                cds.append(jnp.exp(colb[L - 1:L, :]))
            yacc_ref[:, cols] = jnp.where(half, ys[0], ys[1])
            st_ref[:, cols] = jnp.where(half, st_p * cds[0] + ss[0], st_p * cds[1] + ss[1])

    y = yacc_ref[...] + d_ref[...] * xbc_ref[:, :D_INNER]
    yn = _rms(y * _silu(z), ny_ref[...])
    if valid < L:
        y_ref[0] = yn[0:valid].astype(y_ref.dtype)
    else:
        y_ref[0] = yn.astype(y_ref.dtype)

    @pl.when(c == nchunks - 1)
    def _():
        convo_ref[0] = tail_ref[...]
        for j in range(npair):
            ssmo_ref[0, j * LANES:(j + 1) * LANES, :] = st_ref[:, j * LANES:(j + 1) * LANES].T


def _ssd(zx, dtr, conv_w, conv_b, dt_bias, a_log, d_skip, norm_y, conv0, ssm0):
    b, l, _ = zx.shape
    if l >= CHUNK:
        lb, valid, nchunks = CHUNK, CHUNK, l // CHUNK
    else:
        lb, valid, nchunks = l, l, 1
    pad_lanes = LANES - SSM_HEADS
    dtb = jnp.pad(dt_bias.astype(F32), (0, pad_lanes)).reshape(1, LANES)
    alog = jnp.pad(a_log.astype(F32), (0, pad_lanes)).reshape(1, LANES)
    d_exp = jnp.repeat(d_skip.astype(F32), SSM_HEADDIM).reshape(1, D_INNER)
    conv0p = jnp.pad(conv0, ((0, 0), (5, 0), (0, 0)))
    ssm0f = ssm0.reshape(b, SSM_HEADS * SSM_HEADDIM, D_STATE)
    const = lambda bi, c: (0, 0)
    in_specs = [
        pl.BlockSpec((1, lb, ZXBC_DIM), lambda bi, c: (bi, c, 0)),
        pl.BlockSpec((1, lb, LANES), lambda bi, c: (bi, c, 0)),
        pl.BlockSpec((4, CONV_DIM), const),
        pl.BlockSpec((1, CONV_DIM), const),
        pl.BlockSpec((1, LANES), const),
        pl.BlockSpec((1, LANES), const),
        pl.BlockSpec((1, D_INNER), const),
        pl.BlockSpec((1, D_INNER), const),
        pl.BlockSpec((1, 8, CONV_DIM), lambda bi, c: (bi, 0, 0)),
        pl.BlockSpec((1, D_INNER, D_STATE), lambda bi, c: (bi, 0, 0)),
    ]
    out_shape = [jax.ShapeDtypeStruct((b, l, D_INNER), BF16),
                 jax.ShapeDtypeStruct((b, 8, CONV_DIM), F32),
                 jax.ShapeDtypeStruct((b, D_INNER, D_STATE), F32)]
    out_specs = [pl.BlockSpec((1, lb, D_INNER), lambda bi, c: (bi, c, 0)),
                 pl.BlockSpec((1, 8, CONV_DIM), lambda bi, c: (bi, 0, 0)),
                 pl.BlockSpec((1, D_INNER, D_STATE), lambda bi, c: (bi, 0, 0))]
    scratch = [pltpu.VMEM((D_STATE, D_INNER), F32), pltpu.VMEM((8, CONV_DIM), F32),
               pltpu.VMEM((CHUNK, CONV_DIM), F32), pltpu.VMEM((CHUNK, D_INNER), F32)]
    if valid < CHUNK:
        scratch += [pltpu.VMEM((CHUNK, ZXBC_DIM), F32), pltpu.VMEM((CHUNK, LANES), F32)]
    y, convo, ssmo = pl.pallas_call(
        functools.partial(_ssd_body, valid=valid, nchunks=nchunks),
        grid=(b, nchunks), in_specs=in_specs, out_specs=out_specs, out_shape=out_shape,
        scratch_shapes=scratch, compiler_params=_params(("parallel", "arbitrary")), name="ssd")(
            zx, dtr, conv_w, conv_b.reshape(1, CONV_DIM), dtb, alog, d_exp, norm_y.reshape(1, D_INNER),
            conv0p, ssm0f)
    return y, convo[:, 5:], ssmo.reshape(b, SSM_HEADS, SSM_HEADDIM, D_STATE)


def _router_body(x_ref, g_ref, w_ref, b_ref, xn_ref, slab_ref, cnt_ref, base_ref, su_ref):
    tm = x_ref.shape[0]

    @pl.when(pl.program_id(0) == 0)
    def _():
        base_ref[...] = jnp.zeros_like(base_ref)
        su_ref[...] = jnp.where(_iota((tm, tm), 0) < _iota((tm, tm), 1), 1.0, 0.0).astype(BF16)

    xn = _rms(x_ref[...], g_ref[...])
    _rows_to_tiles(xn_ref, xn)
    xh, xm, xl = _split3(xn)
    wh = w_ref[0]
    wm = w_ref[1]
    wl = w_ref[2]
    logits = (_dot_nt(wh, xh) + _dot_nt(wh, xm) + _dot_nt(wm, xh)
              + _dot_nt(wh, xl) + _dot_nt(wm, xm) + _dot_nt(wl, xh)) + b_ref[...]
    gl = [logits[i:i + 1, :] for i in range(N_EXPERT_GROUPS)]
    best = gl[0]
    sel = jnp.zeros((1, tm), I32)
    for i in range(1, N_EXPERT_GROUPS):
        upd = gl[i] > best
        sel = jnp.where(upd, i, sel)
        best = jnp.where(upd, gl[i], best)
    den = jnp.exp(gl[0] - best)
    for i in range(1, N_EXPERT_GROUPS):
        den = den + jnp.exp(gl[i] - best)
    g_w = 1.0 / den
    chosen = logits[8:8 + EXPERTS_PER_GROUP, :]
    for i in range(1, N_EXPERT_GROUPS):
        chosen = jnp.where(sel == i, logits[8 + i * EXPERTS_PER_GROUP:8 + (i + 1) * EXPERTS_PER_GROUP, :], chosen)
    idx = _iota((EXPERTS_PER_GROUP, tm), 0)
    m1 = jnp.max(chosen, axis=0, keepdims=True)
    i1 = jnp.min(jnp.where(chosen == m1, idx, EXPERTS_PER_GROUP), axis=0, keepdims=True)
    rest = jnp.where(idx == i1, -jnp.inf, chosen)
    m2 = jnp.max(rest, axis=0, keepdims=True)
    i2 = jnp.min(jnp.where(rest == m2, idx, EXPERTS_PER_GROUP), axis=0, keepdims=True)
    t = jnp.exp(m2 - m1)
    inv = 1.0 / (1.0 + t)
    w0 = inv * g_w
    w1 = (t * inv) * g_w
    e0 = sel * EXPERTS_PER_GROUP + i1
    e1 = sel * EXPERTS_PER_GROUP + i2

    eidx = _iota((N_EXPERTS, tm), 0)
    oh0 = eidx == e0
    oh1 = eidx == e1
    oh = jnp.where(oh0, 1.0, jnp.where(oh1, 1.0, 0.0))
    base = base_ref[...]
    before = _dot(oh.astype(BF16), su_ref[...]) + jnp.concatenate([base] * (tm // LANES), axis=1)
    rank0 = jnp.sum(jnp.where(oh0, before, 0.0), axis=0, keepdims=True)
    rank1 = jnp.sum(jnp.where(oh1, before, 0.0), axis=0, keepdims=True)
    new_base = base + jnp.sum(oh, axis=1, keepdims=True)
    base_ref[...] = new_base
    cnt_ref[...] = new_base

    r = _iota((8, tm), 0)
    rows = (e0.astype(F32), e1.astype(F32), w0, w1, rank0, rank1)
    slab = jnp.zeros((8, tm), F32)
    for i, v in enumerate(rows):
        slab = jnp.where(r == i, v, slab)
    slab_ref[...] = slab


def _router(x, gain, w_split, bias, tm):
    n = x.shape[0]
    xn, slab, cnt = pl.pallas_call(
        _router_body, grid=(n // tm,),
        in_specs=[pl.BlockSpec((tm, D_MODEL), lambda i: (i, 0)),
                  pl.BlockSpec((1, D_MODEL), lambda i: (0, 0)),
                  pl.BlockSpec((3, ROUTER_ROWS, D_MODEL), lambda i: (0, 0, 0)),
                  pl.BlockSpec((ROUTER_ROWS, 1), lambda i: (0, 0))],
        out_specs=[pl.BlockSpec((tm * ROW_TILES, LANES), lambda i: (i, 0)),
                   pl.BlockSpec((8, tm), lambda i: (0, i)),
                   pl.BlockSpec((N_EXPERTS, LANES), lambda i: (0, 0))],
        out_shape=[jax.ShapeDtypeStruct((n * ROW_TILES, LANES), F32), jax.ShapeDtypeStruct((8, n), F32),
                   jax.ShapeDtypeStruct((N_EXPERTS, LANES), F32)],
        scratch_shapes=[pltpu.VMEM((N_EXPERTS, LANES), F32), pltpu.VMEM((tm, tm), BF16)],
        compiler_params=_params(("arbitrary",)), name="router")(x, gain.reshape(1, D_MODEL), w_split, bias)
    return xn, slab, cnt


def _moe_plan(slab, cnt, tm, blk):
    n = slab.shape[1]
    e_ids = slab[0:2].astype(I32)
    ranks = slab[4:6].astype(I32)
    counts = cnt[:, 0].astype(I32)
    padded = ((counts + tm - 1) // tm) * tm
    pend = jnp.cumsum(padded)
    pstart = pend - padded
    onehot = e_ids[..., None] == jnp.arange(N_EXPERTS, dtype=I32)
    dest = jnp.sum(jnp.where(onehot, pstart, 0), axis=-1) + ranks
    dest = dest.T.reshape(n // blk, 1, 2 * blk)
    slots = 2 * n + N_EXPERTS * tm
    n_tiles = slots // tm
    n_used = (pend[-1] // tm).astype(I32)
    tile_start = jnp.arange(n_tiles, dtype=I32) * tm
    tile_e = jnp.sum((pend[None, :] <= tile_start[:, None]).astype(I32), axis=1)
    tile_e = jnp.minimum(tile_e, N_EXPERTS - 1)
    last_e = jnp.sum(jnp.where(jnp.arange(n_tiles) == n_used - 1, tile_e, 0))
    tile_e = jnp.where(jnp.arange(n_tiles) < n_used, tile_e, last_e).astype(I32)
    return dest, tile_e, n_used.reshape(1), slots


def _dispatch_body(dest_ref, xn_ref, xs_in_hbm, xs_hbm, sem, *, blk):
    del xs_in_hbm

    def issue(i, carry):
        src = xn_ref.at[pl.ds(pl.multiple_of(i * ROW_TILES, ROW_TILES), ROW_TILES)]
        pltpu.make_async_copy(src, xs_hbm.at[dest_ref[0, 0, 2 * i]], sem).start()
        pltpu.make_async_copy(src, xs_hbm.at[dest_ref[0, 0, 2 * i + 1]], sem).start()
        return carry

    lax.fori_loop(0, blk, issue, 0, unroll=8)

    def drain(i, carry):
        pltpu.make_async_copy(xn_ref.at[pl.ds(0, ROW_TILES)], xs_hbm.at[0], sem).wait()
        return carry

    lax.fori_loop(0, 2 * blk, drain, 0, unroll=8)


def _dispatch(xn, dest, slots, blk):
    n = xn.shape[0] // ROW_TILES
    xs0 = jnp.zeros((slots, ROW_TILES, LANES), F32)
    xs = pl.pallas_call(
        functools.partial(_dispatch_body, blk=blk), grid=(n // blk,),
        in_specs=[pl.BlockSpec((1, 1, 2 * blk), lambda i: (i, 0, 0), memory_space=pltpu.SMEM),
                  pl.BlockSpec((blk * ROW_TILES, LANES), lambda i: (i, 0)), pl.BlockSpec(memory_space=pl.ANY)],
        out_specs=pl.BlockSpec(memory_space=pl.ANY),
        out_shape=jax.ShapeDtypeStruct((slots, ROW_TILES, LANES), F32),
        scratch_shapes=[pltpu.SemaphoreType.DMA(())],
        input_output_aliases={2: 0},
        compiler_params=_params(("arbitrary",)), name="dispatch")(dest, xn, xs0)
    return xs.reshape(slots * ROW_TILES, LANES)


def _moe_body(te_ref, nu_ref, xs_ref, wg_ref, wu_ref, wd_ref, o_ref, wgb_ref, wub_ref, wdb_ref, *, tm):
    t = pl.program_id(0)
    e = te_ref[t]
    prev = te_ref[jnp.maximum(t - 1, 0)]

    @pl.when((t == 0) | (e != prev))
    def _():
        wgb_ref[...] = wg_ref[0].astype(BF16)
        wub_ref[...] = wu_ref[0].astype(BF16)
        wdb_ref[...] = wd_ref[0].astype(BF16)

    @pl.when(t < nu_ref[0])
    def _():
        x = _tiles_to_rows(xs_ref, tm).astype(BF16)
        g = _dot(x, wgb_ref[...])
        u = _dot(x, wub_ref[...])
        hid = _silu(g) * u
        _rows_to_tiles(o_ref, _dot(hid.astype(BF16), wdb_ref[...]))


def _moe_experts(xs, tile_e, n_used, w_gate, w_up, w_down, tm):
    slots = xs.shape[0] // ROW_TILES
    n_tiles = slots // tm
    row = lambda t, te, nu: (jnp.minimum(t, nu[0] - 1), 0)
    wsel = lambda t, te, nu: (te[t], 0, 0)
    grid_spec = pltpu.PrefetchScalarGridSpec(
        num_scalar_prefetch=2, grid=(n_tiles,),
        in_specs=[pl.BlockSpec((tm * ROW_TILES, LANES), row),
                  pl.BlockSpec((1, D_MODEL, D_EXPERT), wsel),
                  pl.BlockSpec((1, D_MODEL, D_EXPERT), wsel),
                  pl.BlockSpec((1, D_EXPERT, D_MODEL), wsel)],
        out_specs=pl.BlockSpec((tm * ROW_TILES, LANES), row),
        scratch_shapes=[pltpu.VMEM((D_MODEL, D_EXPERT), BF16), pltpu.VMEM((D_MODEL, D_EXPERT), BF16),
                        pltpu.VMEM((D_EXPERT, D_MODEL), BF16)])
    return pl.pallas_call(
        functools.partial(_moe_body, tm=tm), grid_spec=grid_spec,
        out_shape=jax.ShapeDtypeStruct((slots * ROW_TILES, LANES), F32),
        compiler_params=_params(("arbitrary",)), name="moe_experts")(tile_e, n_used, xs, w_gate, w_up, w_down)


def _ple_body(*refs, final, tm):
    dest_ref, x_ref, gw_ref, p_ref, g_ref, wg_ref, wp_ref = refs[:7]
    if final:
        fg_ref, ys_hbm, o_ref, f_ref, y0_ref, y1_ref, sem = refs[7:]
    else:
        ys_hbm, o_ref, y0_ref, y1_ref, sem = refs[7:]

    def issue(i, carry):
        row = pl.multiple_of(i * ROW_TILES, ROW_TILES)
        pltpu.make_async_copy(ys_hbm.at[dest_ref[0, 0, 2 * i]], y0_ref.at[pl.ds(row, ROW_TILES)], sem).start()
        pltpu.make_async_copy(ys_hbm.at[dest_ref[0, 0, 2 * i + 1]], y1_ref.at[pl.ds(row, ROW_TILES)], sem).start()
        return carry

    lax.fori_loop(0, tm, issue, 0, unroll=8)
    pe = _dot(p_ref[...].astype(BF16), wp_ref[...])

    def drain(i, carry):
        pltpu.make_async_copy(ys_hbm.at[0], y0_ref.at[pl.ds(0, ROW_TILES)], sem).wait()
        return carry

    lax.fori_loop(0, 2 * tm, drain, 0, unroll=8)
    gw = gw_ref[...]
    x2 = x_ref[...] + gw[:, 2:3] * _tiles_to_rows(y0_ref, tm) + gw[:, 3:4] * _tiles_to_rows(y1_ref, tm)
    xn = _rms(x2, g_ref[...])
    gate = jax.nn.sigmoid(_dot(xn.astype(BF16), wg_ref[...]))
    out = x2 + pe * gate
    o_ref[...] = out
    if final:
        f_ref[...] = _rms(out, fg_ref[...])


def _ple(x, ys, dest, gw, p, gain, w_gate, w_proj, final_gain, tm):
    n = x.shape[0]
    dp = p.shape[1]
    final = final_gain is not None
    rowspec = pl.BlockSpec((tm, D_MODEL), lambda i: (i, 0))
    vec = pl.BlockSpec((1, D_MODEL), lambda i: (0, 0))
    in_specs = [pl.BlockSpec((1, 1, 2 * tm), lambda i: (i, 0, 0), memory_space=pltpu.SMEM),
                rowspec, pl.BlockSpec((tm, 8), lambda i: (i, 0)), pl.BlockSpec((tm, dp), lambda i: (i, 0)), vec,
                pl.BlockSpec((D_MODEL, D_MODEL), lambda i: (0, 0)),
                pl.BlockSpec((dp, D_MODEL), lambda i: (0, 0))]
    args = [dest, x, gw, p, gain.reshape(1, D_MODEL), w_gate, w_proj]
    out_shape = [jax.ShapeDtypeStruct((n, D_MODEL), F32)]
    out_specs = [rowspec]
    if final:
        in_specs.append(vec)
        args.append(final_gain.reshape(1, D_MODEL))
        out_shape.append(jax.ShapeDtypeStruct((n, D_MODEL), F32))
        out_specs.append(rowspec)
    in_specs.append(pl.BlockSpec(memory_space=pl.ANY))
    args.append(ys.reshape(-1, ROW_TILES, LANES))
    outs = pl.pallas_call(
        functools.partial(_ple_body, final=final, tm=tm), grid=(n // tm,), in_specs=in_specs,
        out_specs=out_specs, out_shape=out_shape,
        scratch_shapes=[pltpu.VMEM((tm * ROW_TILES, LANES), F32), pltpu.VMEM((tm * ROW_TILES, LANES), F32),
                        pltpu.SemaphoreType.DMA(())],
        compiler_params=_params(("arbitrary",)), name="ple")(*args)
    return (outs[0], outs[1]) if final else (outs[0], None)


ATTN_GROUP = 4


def _attn_group(qs, kcats, vcats, first_keys):
    R = CHUNK
    row = _iota((R, 2 * R), 0)
    col = _iota((R, 2 * R), 1)
    ss = []
    for q, kcat, first_key in zip(qs, kcats, first_keys):
        band = (col >= jnp.maximum(row, first_key)) & (col <= row + R)
        ss.append(jnp.where(band, _dot_nt(q, kcat) * (HEAD_DIM ** -0.5), -jnp.inf))
    s = jnp.concatenate(ss, axis=0)
    m = jnp.max(s, axis=-1, keepdims=True)
    p = jnp.exp(s - m)
    l = jnp.sum(p, axis=-1, keepdims=True)
    pb = p.astype(BF16)
    inv = 1.0 / l
    lse = m + jnp.log(l)
    outs, lses = [], []
    for i, vcat in enumerate(vcats):
        sl = slice(i * R, (i + 1) * R)
        outs.append(_dot(pb[sl], vcat) * inv[sl])
        lses.append(jnp.broadcast_to(lse[sl], (R, LANES)))
    return outs, lses


def _attn_body(q0_ref, q1_ref, q2_ref, k_ref, v_ref, o_ref, kp_ref, vp_ref, og_ref, lg_ref):
    R = CHUNK
    first_key = jnp.where(pl.program_id(2) > 0, 0, R)

    @pl.when(pl.program_id(2) == 0)
    def _():
        kp_ref[...] = jnp.zeros_like(kp_ref)
        vp_ref[...] = jnp.zeros_like(vp_ref)

    def rows(ref, start, dil):
        idx = pl.ds(start, R) if dil == 1 else pl.ds(start, R, stride=dil)
        return ref[idx, :]

    def put(ref, start, dil, val):
        idx = pl.ds(start, R) if dil == 1 else pl.ds(start, R, stride=dil)
        ref[idx, :] = val

    def group(g, q_ref, dil, blocks):
        nblk = ATTN_SPAN // (R * dil)
        qs, kcats, vcats, fks, starts = [], [], [], [], []
        for r, cb in blocks:
            start = r + dil * R * cb
            if cb == 0:
                prev_k, prev_v, prev_start, fk = kp_ref, vp_ref, r + dil * R * (nblk - 1), first_key
            else:
                prev_k, prev_v, prev_start, fk = k_ref, v_ref, start - dil * R, 0
            kcats.append(jnp.concatenate([rows(prev_k, prev_start, dil).astype(BF16),
                                          rows(k_ref, start, dil).astype(BF16)], axis=0))
            vcats.append(jnp.concatenate([rows(prev_v, prev_start, dil).astype(BF16),
                                          rows(v_ref, start, dil).astype(BF16)], axis=0))
            qs.append(rows(q_ref, start, dil).astype(BF16))
            fks.append(fk)
            starts.append(start)
        outs, lses = _attn_group(qs, kcats, vcats, fks)
        for start, o, lse in zip(starts, outs, lses):
            put(og_ref.at[g], start, dil, o)
            put(lg_ref.at[g], start, dil, lse)

    for g, (q_ref, (_, dil)) in enumerate(zip((q0_ref, q1_ref, q2_ref), PATTERNS)):
        nblk = ATTN_SPAN // (R * dil)
        if nblk >= ATTN_GROUP:
            def lane(r, carry, q_ref=q_ref, dil=dil, nblk=nblk, g=g):
                for c0 in range(0, nblk, ATTN_GROUP):
                    group(g, q_ref, dil, [(r, cb) for cb in range(c0, c0 + ATTN_GROUP)])
                return carry
            n_iter = dil
        else:
            def lane(i, carry, q_ref=q_ref, dil=dil, nblk=nblk, g=g):
                for cb in range(nblk):
                    group(g, q_ref, dil, [(i * ATTN_GROUP + j, cb) for j in range(ATTN_GROUP)])
                return carry
            n_iter = dil // ATTN_GROUP
        if n_iter == 1:
            lane(0, 0)
        else:
            lax.fori_loop(0, n_iter, lane, 0)

    step = 256
    for i in range(ATTN_SPAN // step):
        sl = pl.ds(i * step, step)
        l0 = lg_ref[0, sl, :]
        l1 = lg_ref[1, sl, :]
        l2 = lg_ref[2, sl, :]
        m = jnp.maximum(jnp.maximum(l0, l1), l2)
        e0 = jnp.exp(l0 - m)
        e1 = jnp.exp(l1 - m)
        e2 = jnp.exp(l2 - m)
        num = e0 * og_ref[0, sl, :] + e1 * og_ref[1, sl, :] + e2 * og_ref[2, sl, :]
        o_ref[sl, :] = num * (1.0 / (e0 + e1 + e2))
    kp_ref[...] = k_ref[...]
    vp_ref[...] = v_ref[...]


def _attn_prompt(q, k, v, b, s):
    nspan = s // ATTN_SPAN
    blk = (ATTN_SPAN, HEAD_DIM)
    qspec = lambda g: pl.BlockSpec(blk, lambda bi, h, c: (bi * nspan + c, g * N_KV_HEADS + h))
    kvspec = pl.BlockSpec(blk, lambda bi, h, c: (bi * nspan + c, h))
    return pl.pallas_call(
        _attn_body, grid=(b, N_KV_HEADS, nspan),
        in_specs=[qspec(0), qspec(1), qspec(2), kvspec, kvspec],
        out_specs=kvspec, out_shape=jax.ShapeDtypeStruct((b * s, ATTN_DIM), F32),
        scratch_shapes=[pltpu.VMEM(blk, F32), pltpu.VMEM(blk, F32),
                        pltpu.VMEM((3,) + blk, F32), pltpu.VMEM((3,) + blk, F32)],
        compiler_params=_params(("parallel", "parallel", "arbitrary")), name="attn_prompt")(q, q, q, k, v)


SAMPLE_ROWS = 16
DIL_MAX = PATTERNS[-1][1]
DENSE_TAIL = PATTERNS[1][0]


def _attn_sample_body(q_ref, kf_ref, vf_ref, kt_ref, vt_ref, kn_ref, vn_ref, bias_ref, o_ref, *, t_new):
    nq = N_KV_HEADS * SAMPLE_ROWS
    n_far = kf_ref.shape[1] * kf_ref.shape[2]
    n_tail = kt_ref.shape[1]
    q = q_ref[0].astype(BF16)
    kf = kf_ref[0].reshape(n_far, HEAD_DIM).astype(BF16)
    vf = vf_ref[0].reshape(n_far, HEAD_DIM).astype(BF16)
    s = jnp.concatenate([_dot_nt(q, kf), _dot_nt(q, kt_ref[0].astype(BF16)), _dot_nt(q, kn_ref[0].astype(BF16))],
                        axis=1) * (HEAD_DIM ** -0.5) + bias_ref[...]
    m = jnp.max(s, axis=-1, keepdims=True)
    p = jnp.exp(s - m)
    l = jnp.sum(p, axis=-1, keepdims=True)
    pb = p.astype(BF16)
    o = (_dot(pb[:, :n_far], vf) + _dot(pb[:, n_far:n_far + n_tail], vt_ref[0].astype(BF16))
         + _dot(pb[:, n_far + n_tail:], vn_ref[0].astype(BF16))) * (1.0 / l)
    lse = jnp.broadcast_to(m + jnp.log(l), (nq, LANES))
    lse1 = pltpu.roll(lse, nq - t_new, 0)
    lse2 = pltpu.roll(lse, nq - 2 * t_new, 0)
    o1 = pltpu.roll(o, nq - t_new, 0)
    o2 = pltpu.roll(o, nq - 2 * t_new, 0)
    mm = jnp.maximum(jnp.maximum(lse, lse1), lse2)
    e0 = jnp.exp(lse - mm)
    e1 = jnp.exp(lse1 - mm)
    e2 = jnp.exp(lse2 - mm)
    comb = (e0 * o + e1 * o1 + e2 * o2) * (1.0 / (e0 + e1 + e2))
    for h in range(N_KV_HEADS):
        o_ref[0, :, h * HEAD_DIM:(h + 1) * HEAD_DIM] = comb[h * SAMPLE_ROWS:h * SAMPLE_ROWS + t_new]


def _sample_bias(wb, t_new, far):
    rho = jnp.arange(N_KV_HEADS * SAMPLE_ROWS, dtype=I32)
    h = rho // SAMPLE_ROWS
    gt = rho % SAMPLE_ROWS
    g = jnp.where(gt < len(PATTERNS) * t_new, gt // t_new, 0)
    t = gt % t_new
    win = jnp.asarray([w for w, _ in PATTERNS], I32)[g]
    dil = jnp.asarray([d for _, d in PATTERNS], I32)[g]
    far_rows = t_new * N_KV_HEADS
    c = jnp.arange(far // DIL_MAX * far_rows, dtype=I32)
    pos_far = DIL_MAX * (c // far_rows) + (c % far_rows) // N_KV_HEADS
    c = jnp.arange((wb - far) * N_KV_HEADS, dtype=I32)
    pos_tail = far + c // N_KV_HEADS
    c = jnp.arange(LANES, dtype=I32)
    pos_new = jnp.where(c < far_rows, wb + c // N_KV_HEADS, -1)
    pos = jnp.concatenate([pos_far, pos_tail, pos_new])
    head = jnp.concatenate([jnp.arange(n, dtype=I32) % N_KV_HEADS for n in (pos_far.shape[0], pos_tail.shape[0], LANES)])
    dist = wb + t[:, None] - pos[None, :]
    valid = ((head[None, :] == h[:, None]) & (pos[None, :] >= 0) & (dist >= 0) & (dist <= win[:, None])
             & (dist % dil[:, None] == 0))
    return jnp.where(valid, 0.0, -jnp.inf).astype(F32)


def _attn_sample(q, cache_k, cache_v, k_new, v_new, b, t_new):
    wb = cache_k.shape[1]
    tail = min(DENSE_TAIL, wb)
    far = wb - tail
    far_rows = t_new * N_KV_HEADS
    nq = N_KV_HEADS * SAMPLE_ROWS
    qh = q.reshape(b, t_new, 3, N_KV_HEADS, HEAD_DIM).transpose(0, 3, 2, 1, 4).reshape(b, N_KV_HEADS, 3 * t_new, HEAD_DIM)
    qh = jnp.pad(qh, ((0, 0), (0, 0), (0, SAMPLE_ROWS - 3 * t_new), (0, 0))).reshape(b, nq, HEAD_DIM)
    new = lambda x: jnp.pad(x.reshape(b, far_rows, HEAD_DIM), ((0, 0), (0, LANES - far_rows), (0, 0)))
    grouped = lambda x: x.reshape(b, wb // DIL_MAX, DIL_MAX * N_KV_HEADS, HEAD_DIM)
    flat = lambda x: x.reshape(b, wb * N_KV_HEADS, HEAD_DIM)
    bias = _sample_bias(wb, t_new, far)
    far_spec = pl.BlockSpec((1, far // DIL_MAX, far_rows, HEAD_DIM), lambda bi: (bi, 0, 0, 0))
    tail_spec = pl.BlockSpec((1, tail * N_KV_HEADS, HEAD_DIM), lambda bi: (bi, wb // tail - 1, 0))
    sq_spec = pl.BlockSpec((1, LANES, HEAD_DIM), lambda bi: (bi, 0, 0))
    o = pl.pallas_call(
        functools.partial(_attn_sample_body, t_new=t_new), grid=(b,),
        in_specs=[sq_spec, far_spec, far_spec, tail_spec, tail_spec, sq_spec, sq_spec,
                  pl.BlockSpec(bias.shape, lambda bi: (0, 0))],
        out_specs=pl.BlockSpec((1, t_new, ATTN_DIM), lambda bi: (bi, 0, 0)),
        out_shape=jax.ShapeDtypeStruct((b, t_new, ATTN_DIM), F32),
        compiler_params=_params(("parallel",)), name="attn_sample")(
            qh, grouped(cache_k), grouped(cache_v), flat(cache_k), flat(cache_v), new(k_new), new(v_new), bias)
    return o.reshape(b * t_new, ATTN_DIM)


def _rope_tables(pos0, l, reps):
    inv_freq = ROPE_THETA ** (-jnp.arange(0, HEAD_DIM, 2, dtype=F32) / HEAD_DIM)
    pos = pos0 + jnp.arange(l, dtype=jnp.int32)
    ang = pos.astype(F32)[:, None] * inv_freq[None, :]
    cos = jnp.cos(ang)
    sin = jnp.sin(ang)
    cosf = jnp.concatenate([cos, cos], axis=-1)
    sinf = jnp.concatenate([-sin, sin], axis=-1)
    if reps > 1:
        cosf = jnp.tile(cosf, (reps, 1))
        sinf = jnp.tile(sinf, (reps, 1))
    return cosf, sinf


def _prep_weights(prm):
    w = {}
    w['in_proj'] = prm['ssm_in_proj'].astype(BF16)
    w['in_dt'] = jnp.pad(prm['ssm_in_proj'][:, :, ZXBC_DIM:], ((0, 0), (0, 0), (0, LANES - SSM_HEADS))).astype(BF16)
    w['out_proj'] = prm['ssm_out_proj'].astype(BF16)
    w['w_kv'] = prm['w_kv'].astype(BF16)
    w['w_q'] = prm['attn_w_q'].astype(BF16)
    w['w_o'] = prm['attn_w_o'].astype(BF16)
    w['ple_gate'] = prm['ple_w_gate'].astype(BF16)
    w['ple_proj'] = prm['ple_w_proj'].astype(BF16)
    wr = jnp.concatenate([jnp.swapaxes(prm['moe_w_rg'], 1, 2),
                          jnp.zeros((DEPTH, 8 - N_EXPERT_GROUPS, D_MODEL), F32),
                          jnp.swapaxes(prm['moe_w_re'], 1, 2)], axis=1)
    w['router'] = jnp.stack(_split3(wr), axis=1)
    w['router_b'] = jnp.concatenate([prm['moe_b_rg'], jnp.zeros((DEPTH, 8 - N_EXPERT_GROUPS), F32),
                                     prm['moe_b_re']], axis=1).reshape(DEPTH, ROUTER_ROWS, 1)
    return w


def _trunk(x, p, pos0, conv_state, ssm_state, cache_k, cache_v, prm, w):
    b, l, _ = x.shape
    n = b * l
    sample = cache_k is not None
    tm = min(512, n)
    moe_tm = 256 if n >= 4096 else 32
    xf = x.reshape(n, D_MODEL)
    rope = _rope_tables(pos0, l, 1 if l >= tm else tm // l)
    new_conv, new_ssm = [], []
    k32 = v32 = None
    y_final = None
    for i in range(DEPTH):
        gain = prm['norm_mix'][i]
        if i < N_A_LAYERS:
            zx = _proj(xf, w['in_proj'][i], gain=gain, nout=ZXBC_DIM, tm=tm, tn=1024)
            dtr = _proj(xf, w['in_dt'][i], gain=gain, tm=tm, tn=LANES)
            y, cbuf, hs = _ssd(zx.reshape(b, l, ZXBC_DIM), dtr.reshape(b, l, LANES), prm['ssm_conv_w'][i],
                               prm['ssm_conv_b'][i], prm['ssm_dt_bias'][i], prm['ssm_a_log'][i], prm['ssm_d'][i],
                               prm['ssm_norm_y'][i], conv_state[i], ssm_state[i])
            new_conv.append(cbuf)
            new_ssm.append(hs)
            xf = _proj(y.reshape(n, D_INNER), w['out_proj'][i], residual=xf, tm=tm, tn=1024)
        else:
            j = i - N_A_LAYERS
            q = _proj(xf, w['w_q'][j], gain=gain, rope=rope, tm=tm, tn=1024)
            if sample:
                att = _attn_sample(q, cache_k, cache_v, k32, v32, b, l)
            else:
                att = _attn_prompt(q, k32, v32, b, l)
            xf = _proj(att, w['w_o'][j], residual=xf, tm=tm, tn=1024)
        xn, slab, cnt = _router(xf, prm['norm_ffn'][i], w['router'][i], w['router_b'][i], tm)
        dest, tile_e, n_used, slots = _moe_plan(slab, cnt, moe_tm, tm)
        xs = _dispatch(xn, dest, slots, tm)
        ys = _moe_experts(xs, tile_e, n_used, prm['moe_w_gate'][i], prm['moe_w_up'][i], prm['moe_w_down'][i], moe_tm)
        xf, y_final = _ple(xf, ys, dest, slab.T, p[i].reshape(n, -1), prm['ple_norm'][i], w['ple_gate'][i],
                           w['ple_proj'][i], prm['norm_final'] if i == DEPTH - 1 else None, tm)
        if i == N_A_LAYERS - 1:
            k32 = _proj(xf, w['w_kv'], gain=prm['norm_kv'], rope=rope, nout=ATTN_DIM, tm=tm, tn=1024)
            v32 = _proj(xf, w['w_kv'][:, ATTN_DIM:], gain=prm['norm_kv'], tm=tm, tn=1024)
    return (y_final.reshape(b, l, D_MODEL), jnp.stack(new_conv), jnp.stack(new_ssm),
            k32.reshape(b, l, ATTN_DIM), v32.reshape(b, l, ATTN_DIM))


def kernel(x_prompt, x_sample, state_ssm, state_conv, cache_k, cache_v, p_prompt, p_sample, norm_mix, ssm_in_proj, ssm_conv_w, ssm_conv_b, ssm_dt_bias, ssm_a_log, ssm_d, ssm_norm_y, ssm_out_proj, norm_kv, w_kv, attn_w_q, attn_w_o, norm_ffn, moe_w_rg, moe_b_rg, moe_w_re, moe_b_re, moe_w_gate, moe_w_up, moe_w_down, ple_w_proj, ple_w_gate, ple_norm, norm_final):
    prm = dict(norm_mix=norm_mix, ssm_in_proj=ssm_in_proj, ssm_conv_w=ssm_conv_w, ssm_conv_b=ssm_conv_b,
               ssm_dt_bias=ssm_dt_bias, ssm_a_log=ssm_a_log, ssm_d=ssm_d, ssm_norm_y=ssm_norm_y,
               ssm_out_proj=ssm_out_proj, norm_kv=norm_kv, w_kv=w_kv, attn_w_q=attn_w_q, attn_w_o=attn_w_o,
               norm_ffn=norm_ffn, moe_w_rg=moe_w_rg, moe_b_rg=moe_b_rg, moe_w_re=moe_w_re, moe_b_re=moe_b_re,
               moe_w_gate=moe_w_gate, moe_w_up=moe_w_up, moe_w_down=moe_w_down, ple_w_proj=ple_w_proj,
               ple_w_gate=ple_w_gate, ple_norm=ple_norm, norm_final=norm_final)
    w = _prep_weights(prm)
    b, s, _ = x_prompt.shape
    db, dl, _ = x_sample.shape
    past_len = 8192
    conv0 = jnp.zeros((N_A_LAYERS, b, 3, CONV_DIM), x_prompt.dtype)
    ssm0 = jnp.zeros((N_A_LAYERS, b, SSM_HEADS, SSM_HEADDIM, D_STATE), state_ssm.dtype)
    y_p, conv_p, ssm_p, k_p, v_p = _trunk(x_prompt, p_prompt, 0, conv0, ssm0, None, None, prm, w)
    keep = min(2048, s)
    heads = lambda t: t.reshape(t.shape[0], t.shape[1], N_KV_HEADS, HEAD_DIM)
    k_p = heads(k_p[:, s - keep:])
    v_p = heads(v_p[:, s - keep:])
    y_s, conv_s, ssm_s, k_s, v_s = _trunk(x_sample, p_sample, past_len, state_conv, state_ssm,
                                          cache_k, cache_v, prm, w)
    return (y_p, y_s, ssm_p, conv_p, k_p, v_p, ssm_s, conv_s, heads(k_s), heads(v_s))
```

```python
import functools

import jax
import jax.numpy as jnp
from jax import lax
from jax.experimental import pallas as pl
from jax.experimental.pallas import tpu as pltpu

F32 = jnp.float32
BF16 = jnp.bfloat16
I32 = jnp.int32

EPS = 1e-6
D_MODEL = 1024
D_INNER = 2048
SSM_HEADS = 32
SSM_HEADDIM = 64
SSM_GROUPS = 4
D_STATE = 128
CONV_DIM = 3072
ZXBC_DIM = D_INNER + CONV_DIM
CHUNK = 128
HEAD_DIM = 128
N_KV_HEADS = 8
ATTN_DIM = 1024
PATTERNS = ((128, 1), (512, 4), (2048, 16))
ATTN_SPAN = 2048
ROPE_THETA = 10000.0
N_EXPERTS = 32
N_EXPERT_GROUPS = 4
EXPERTS_PER_GROUP = 8
D_EXPERT = 512
N_A_LAYERS = 2
DEPTH = 4
PRECISE_MIXERS = 1
ROUTER_ROWS = 8 + N_EXPERTS
LANES = 128
SUBLANES = 8
ROW_TILES = D_MODEL // LANES
VMEM_LIMIT = 56 * 1024 * 1024


def _params(sem):
    return pltpu.CompilerParams(dimension_semantics=sem, vmem_limit_bytes=VMEM_LIMIT)


def _iota(shape, dim):
    return lax.broadcasted_iota(I32, shape, dim)


def _rms(x, gain):
    ms = jnp.mean(x * x, axis=-1, keepdims=True)
    return x * lax.rsqrt(ms + EPS) * gain


def _sigmoid(x):
    return 0.5 * jnp.tanh(0.5 * x) + 0.5


def _silu(x):
    return x * _sigmoid(x)


def _dot(a, b):
    return jnp.dot(a, b, preferred_element_type=F32)


def _dot_nt(a, b):
    return lax.dot_general(a, b, (((1,), (1,)), ((), ())), preferred_element_type=F32)


def _split3(x):
    hi = x.astype(BF16)
    r = x - hi.astype(F32)
    mid = r.astype(BF16)
    lo = (r - mid.astype(F32)).astype(BF16)
    return hi, mid, lo


def _split2(x):
    hi = x.astype(BF16)
    return hi, (x - hi.astype(F32)).astype(BF16)


def _mm(a, b, dot):
    y = dot(a[0], b[0])
    if len(a) > 1:
        y = y + dot(a[1], b[0]) + dot(a[0], b[1])
    return y


def _rows_to_tiles(ref, x):
    m = x.shape[0]
    for s in range(ROW_TILES):
        ref[pl.ds(s, m, stride=ROW_TILES), :] = x[:, s * LANES:(s + 1) * LANES]


def _tiles_to_rows(ref, m):
    return jnp.concatenate([ref[pl.ds(s, m, stride=ROW_TILES), :] for s in range(ROW_TILES)], axis=1)


def _proj_body(*refs, has_norm, has_rope, has_res, x_heads, out_heads, precise):
    it = iter(refs)
    x_ref = next(it)
    w_ref = next(it)
    wlo_ref = next(it) if precise else None
    g_ref = next(it) if has_norm else None
    cos_ref = next(it) if has_rope else None
    sin_ref = next(it) if has_rope else None
    r_ref = next(it) if has_res else None
    o_refs = [next(it) for _ in out_heads]
    xb_ref = next(it)
    xlo_ref = next(it) if precise else None

    @pl.when(pl.program_id(1) == 0)
    def _():
        if x_heads:
            x = jnp.concatenate([x_ref[h] for h in range(x_ref.shape[0])], axis=1).astype(F32)
        else:
            x = x_ref[...].astype(F32)
        if has_norm:
            x = _rms(x, g_ref[...])
        if precise:
            xb_ref[...], xlo_ref[...] = _split2(x)
        else:
            xb_ref[...] = x.astype(BF16)

    if precise:
        y = _mm((xb_ref[...], xlo_ref[...]), (w_ref[...], wlo_ref[...]), _dot)
    else:
        y = _dot(xb_ref[...], w_ref[...])
    if has_rope:
        cos = cos_ref[...]
        sin = sin_ref[...]
        parts = []
        for h in range(y.shape[1] // HEAD_DIM):
            t = y[:, h * HEAD_DIM:(h + 1) * HEAD_DIM]
            parts.append(t * cos + pltpu.roll(t, HEAD_DIM // 2, 1) * sin)
        y = parts[0] if len(parts) == 1 else jnp.concatenate(parts, axis=1)
    if has_res:
        y = y + r_ref[...]
    for o, heads in zip(o_refs, out_heads):
        if heads:
            for h in range(o.shape[0]):
                o[h] = y[:, h * HEAD_DIM:(h + 1) * HEAD_DIM].astype(o.dtype)
        else:
            o[...] = y.astype(o.dtype)


def _proj(x, w, *, w_lo=None, layer=None, col0=0, gain=None, rope=None, residual=None, out_dtypes=(F32,),
          out_heads=None, nout=None, tm, tn):
    x_heads = x.ndim == 3
    n = x.shape[-2]
    k = x.shape[0] * x.shape[2] if x_heads else x.shape[1]
    nout = w.shape[-1] if nout is None else nout
    tn = min(tn, nout)
    out_heads = (False,) * len(out_dtypes) if out_heads is None else out_heads
    grid = (n // tm, nout // tn)
    jb = col0 // tn
    precise = w_lo is not None
    wspec = (pl.BlockSpec((k, tn), lambda i, j: (0, j + jb)) if layer is None else
             pl.BlockSpec((None, k, tn), lambda i, j: (layer, 0, j + jb)))
    in_specs = [pl.BlockSpec((k // HEAD_DIM, tm, HEAD_DIM), lambda i, j: (0, i, 0)) if x_heads else
                pl.BlockSpec((tm, k), lambda i, j: (i, 0)), wspec]
    args = [x, w]
    if precise:
        in_specs.append(wspec)
        args.append(w_lo)
    if gain is not None:
        in_specs.append(pl.BlockSpec((1, k), lambda i, j: (0, 0)))
        args.append(gain.reshape(1, k))
    if rope is not None:
        cos, sin = rope
        nblk = cos.shape[0] // tm
        for t in (cos, sin):
            in_specs.append(pl.BlockSpec((tm, HEAD_DIM), lambda i, j: (i % nblk, 0)))
            args.append(t)
    if residual is not None:
        in_specs.append(pl.BlockSpec((tm, tn), lambda i, j: (i, j)))
        args.append(residual)
    out_shape = [jax.ShapeDtypeStruct((nout // HEAD_DIM, n, HEAD_DIM) if hm else (n, nout), dt)
                 for dt, hm in zip(out_dtypes, out_heads)]
    out_specs = [pl.BlockSpec((tn // HEAD_DIM, tm, HEAD_DIM), lambda i, j: (j, i, 0)) if hm else
                 pl.BlockSpec((tm, tn), lambda i, j: (i, j)) for hm in out_heads]
    body = functools.partial(_proj_body, has_norm=gain is not None, has_rope=rope is not None,
                             has_res=residual is not None, x_heads=x_heads, out_heads=tuple(out_heads),
                             precise=precise)
    outs = pl.pallas_call(
        body, grid=grid, in_specs=in_specs, out_specs=out_specs, out_shape=out_shape,
        scratch_shapes=[pltpu.VMEM((tm, k), BF16)] * (2 if precise else 1),
        compiler_params=_params(("parallel", "arbitrary")), name="proj")(*args)
    return outs[0] if len(outs) == 1 else outs


def _ssd_body(zx_ref, dt_ref, cw_ref, cb_ref, dtb_ref, alog_ref, d_ref, ny_ref, conv0_ref, ssm0_ref,
              y_ref, convo_ref, ssmo_ref, st_ref, tail_ref, xbc_ref, yacc_ref, *pad_refs, valid, nchunks, precise):
    L = CHUNK
    c = pl.program_id(1)
    npair = SSM_HEADS // 2
    parts = _split2 if precise else (lambda x: (x.astype(BF16),))

    @pl.when(c == 0)
    def _():
        for j in range(npair):
            st_ref[:, j * LANES:(j + 1) * LANES] = ssm0_ref[0, j * LANES:(j + 1) * LANES, :].T
        tail_ref[0:8, :] = conv0_ref[0]

    if valid < L:
        pzx_ref, pdt_ref = pad_refs
        pzx_ref[...] = jnp.zeros_like(pzx_ref)
        pdt_ref[...] = jnp.zeros_like(pdt_ref)
        pzx_ref[0:valid, :] = zx_ref[0]
        pdt_ref[0:valid, :] = dt_ref[0]
        z = pzx_ref[:, :D_INNER]
        u = pzx_ref[:, D_INNER:]
        dtr = pdt_ref[...]
    else:
        z = zx_ref[0, :, :D_INNER]
        u = zx_ref[0, :, D_INNER:]
        dtr = dt_ref[0]

    tail_ref[8:8 + L, :] = u
    acc = cb_ref[...] + cw_ref[3:4, :] * u
    for k in range(1, 4):
        acc = acc + cw_ref[3 - k:4 - k, :] * tail_ref[8 - k:8 - k + L, :]
    tail_ref[0:8, :] = tail_ref[valid:valid + 8, :]
    xbc_ref[...] = _silu(acc)

    lane = _iota((1, LANES), 1)
    rowl = _iota((L, 1), 0)
    xdt = dtr + dtb_ref[...]
    dt = jnp.maximum(xdt, 0.0) + jnp.log(1.0 + jnp.exp(-jnp.abs(xdt)))
    dt = jnp.where((lane < SSM_HEADS) & (rowl < valid), dt, 0.0)
    a_neg = -jnp.exp(alog_ref[...])
    da = dt * a_neg
    rr = _iota((L, L), 0)
    cc = _iota((L, L), 1)
    causal = rr >= cc
    tri = jnp.where(causal, 1.0, 0.0).astype(BF16)
    hi, mid, lo = _split3(da)
    a_cs = _dot(tri, hi) + _dot(tri, mid) + _dot(tri, lo)
    a_cs_t = a_cs.T
    dt_t = dt.T
    tdt_t = dt_t * jnp.exp(a_cs_t[:, L - 1:L] - a_cs_t)
    ea = jnp.exp(a_cs)
    half = _iota((1, LANES), 1) < SSM_HEADDIM

    for g in range(SSM_GROUPS):
        b_g = xbc_ref[:, D_INNER + g * D_STATE:D_INNER + (g + 1) * D_STATE]
        c_g = xbc_ref[:, D_INNER + SSM_GROUPS * D_STATE + g * D_STATE:
                      D_INNER + SSM_GROUPS * D_STATE + (g + 1) * D_STATE]
        cb = _mm(parts(c_g), parts(b_g), _dot_nt)
        b_t = b_g.T
        for jj in range(npair // SSM_GROUPS):
            j = g * (npair // SSM_GROUPS) + jj
            cols = slice(j * LANES, (j + 1) * LANES)
            xs_b = parts(xbc_ref[:, cols])
            st_p = st_ref[:, cols]
            rhs = tuple(jnp.concatenate([a, b], axis=0) for a, b in zip(xs_b, parts(st_p)))
            ys, ss, cds = [], [], []
            for h in (2 * j, 2 * j + 1):
                colb = jnp.broadcast_to(a_cs[:, h:h + 1], (L, LANES))
                seg = colb - a_cs_t[h:h + 1, :]
                dec = jnp.exp(jnp.where(causal, seg, -jnp.inf))
                m = cb * dec * dt_t[h:h + 1, :]
                ecol = jnp.broadcast_to(ea[:, h:h + 1], (L, LANES))
                ce = c_g * ecol
                lhs = tuple(jnp.concatenate([a, b], axis=1) for a, b in zip(parts(m), parts(ce)))
                ys.append(_mm(lhs, rhs, _dot))
                ss.append(_mm(parts(b_t * tdt_t[h:h + 1, :]), xs_b, _dot))
                cds.append(ecol[L - 1:L, :])
            yacc_ref[:, cols] = jnp.where(half, ys[0], ys[1])
            st_ref[:, cols] = jnp.where(half, st_p * cds[0] + ss[0], st_p * cds[1] + ss[1])

    y = yacc_ref[...] + d_ref[...] * xbc_ref[:, :D_INNER]
    yn = _rms(y * _silu(z), ny_ref[...])
    if valid < L:
        y_ref[0] = yn[0:valid].astype(y_ref.dtype)
    else:
        y_ref[0] = yn.astype(y_ref.dtype)

    @pl.when(c == nchunks - 1)
    def _():
        convo_ref[0] = tail_ref[0:8, :]
        for j in range(npair):
            ssmo_ref[0, j * LANES:(j + 1) * LANES, :] = st_ref[:, j * LANES:(j + 1) * LANES].T


def _ssd(zx, dtr, conv_w, conv_b, dt_bias, a_log, d_skip, norm_y, conv0, ssm0, layer, precise):
    b, l, _ = zx.shape
    if l >= CHUNK:
        lb, valid, nchunks = CHUNK, CHUNK, l // CHUNK
    else:
        lb, valid, nchunks = l, l, 1
    pad_lanes = LANES - SSM_HEADS
    dtb = jnp.pad(dt_bias.astype(F32), (0, pad_lanes)).reshape(1, LANES)
    alog = jnp.pad(a_log.astype(F32), (0, pad_lanes)).reshape(1, LANES)
    d_exp = jnp.repeat(d_skip.astype(F32), SSM_HEADDIM).reshape(1, D_INNER)
    conv0p = jnp.pad(conv0, ((0, 0), (5, 0), (0, 0)))
    ssm0f = ssm0.reshape(ssm0.shape[0], b, SSM_HEADS * SSM_HEADDIM, D_STATE)
    const = lambda bi, c: (0, 0)
    in_specs = [
        pl.BlockSpec((1, lb, ZXBC_DIM), lambda bi, c: (bi, c, 0)),
        pl.BlockSpec((1, lb, LANES), lambda bi, c: (bi, c, 0)),
        pl.BlockSpec((4, CONV_DIM), const),
        pl.BlockSpec((1, CONV_DIM), const),
        pl.BlockSpec((1, LANES), const),
        pl.BlockSpec((1, LANES), const),
        pl.BlockSpec((1, D_INNER), const),
        pl.BlockSpec((1, D_INNER), const),
        pl.BlockSpec((1, 8, CONV_DIM), lambda bi, c: (bi, 0, 0)),
        pl.BlockSpec((None, 1, D_INNER, D_STATE), lambda bi, c: (layer, bi, 0, 0)),
    ]
    out_shape = [jax.ShapeDtypeStruct((b, l, D_INNER), F32 if precise else BF16),
                 jax.ShapeDtypeStruct((b, 8, CONV_DIM), F32),
                 jax.ShapeDtypeStruct((b, D_INNER, D_STATE), F32)]
    out_specs = [pl.BlockSpec((1, lb, D_INNER), lambda bi, c: (bi, c, 0)),
                 pl.BlockSpec((1, 8, CONV_DIM), lambda bi, c: (bi, 0, 0)),
                 pl.BlockSpec((1, D_INNER, D_STATE), lambda bi, c: (bi, 0, 0))]
    scratch = [pltpu.VMEM((D_STATE, D_INNER), F32), pltpu.VMEM((8 + CHUNK, CONV_DIM), F32),
               pltpu.VMEM((CHUNK, CONV_DIM), F32), pltpu.VMEM((CHUNK, D_INNER), F32)]
    if valid < CHUNK:
        scratch += [pltpu.VMEM((CHUNK, ZXBC_DIM), F32), pltpu.VMEM((CHUNK, LANES), F32)]
    y, convo, ssmo = pl.pallas_call(
        functools.partial(_ssd_body, valid=valid, nchunks=nchunks, precise=precise),
        grid=(b, nchunks), in_specs=in_specs, out_specs=out_specs, out_shape=out_shape,
        scratch_shapes=scratch, compiler_params=_params(("parallel", "arbitrary")), name="ssd")(
            zx, dtr, conv_w, conv_b.reshape(1, CONV_DIM), dtb, alog, d_exp, norm_y.reshape(1, D_INNER),
            conv0p, ssm0f)
    return y, convo[:, 5:], ssmo.reshape(b, SSM_HEADS, SSM_HEADDIM, D_STATE)


def _router_body(x_ref, g_ref, w_ref, b_ref, xn_ref, slab_ref, cnt_ref, base_ref, su_ref):
    tm = x_ref.shape[0]

    @pl.when(pl.program_id(0) == 0)
    def _():
        base_ref[...] = jnp.zeros_like(base_ref)
        su_ref[...] = jnp.where(_iota((tm, tm), 0) < _iota((tm, tm), 1), 1.0, 0.0).astype(BF16)

    xn = _rms(x_ref[...], g_ref[...])
    _rows_to_tiles(xn_ref, xn)
    xh, xm, xl = _split3(xn)
    wh = w_ref[0]
    wm = w_ref[1]
    wl = w_ref[2]
    logits = (_dot_nt(wh, xh) + _dot_nt(wh, xm) + _dot_nt(wm, xh)
              + _dot_nt(wh, xl) + _dot_nt(wm, xm) + _dot_nt(wl, xh)) + b_ref[...]
    gl = [logits[i:i + 1, :] for i in range(N_EXPERT_GROUPS)]
    best = gl[0]
    sel = jnp.zeros((1, tm), I32)
    for i in range(1, N_EXPERT_GROUPS):
        upd = gl[i] > best
        sel = jnp.where(upd, i, sel)
        best = jnp.where(upd, gl[i], best)
    den = jnp.exp(gl[0] - best)
    for i in range(1, N_EXPERT_GROUPS):
        den = den + jnp.exp(gl[i] - best)
    g_w = 1.0 / den
    chosen = logits[8:8 + EXPERTS_PER_GROUP, :]
    for i in range(1, N_EXPERT_GROUPS):
        chosen = jnp.where(sel == i, logits[8 + i * EXPERTS_PER_GROUP:8 + (i + 1) * EXPERTS_PER_GROUP, :], chosen)
    idx = _iota((EXPERTS_PER_GROUP, tm), 0)
    m1 = jnp.max(chosen, axis=0, keepdims=True)
    i1 = jnp.min(jnp.where(chosen == m1, idx, EXPERTS_PER_GROUP), axis=0, keepdims=True)
    rest = jnp.where(idx == i1, -jnp.inf, chosen)
    m2 = jnp.max(rest, axis=0, keepdims=True)
    i2 = jnp.min(jnp.where(rest == m2, idx, EXPERTS_PER_GROUP), axis=0, keepdims=True)
    t = jnp.exp(m2 - m1)
    inv = 1.0 / (1.0 + t)
    w0 = inv * g_w
    w1 = (t * inv) * g_w
    e0 = sel * EXPERTS_PER_GROUP + i1
    e1 = sel * EXPERTS_PER_GROUP + i2

    eidx = _iota((N_EXPERTS, tm), 0)
    oh0 = eidx == e0
    oh1 = eidx == e1
    oh = jnp.where(oh0, 1.0, jnp.where(oh1, 1.0, 0.0))
    base = base_ref[...]
    before = _dot(oh.astype(BF16), su_ref[...]) + jnp.concatenate([base] * (tm // LANES), axis=1)
    rank0 = jnp.sum(jnp.where(oh0, before, 0.0), axis=0, keepdims=True)
    rank1 = jnp.sum(jnp.where(oh1, before, 0.0), axis=0, keepdims=True)
    new_base = base + jnp.sum(oh, axis=1, keepdims=True)
    base_ref[...] = new_base
    cnt_ref[...] = new_base

    r = _iota((8, tm), 0)
    rows = (e0.astype(F32), e1.astype(F32), w0, w1, rank0, rank1)
    slab = jnp.zeros((8, tm), F32)
    for i, v in enumerate(rows):
        slab = jnp.where(r == i, v, slab)
    slab_ref[...] = slab


def _router(x, gain, w_split, bias, layer, tm):
    n = x.shape[0]
    xn, slab, cnt = pl.pallas_call(
        _router_body, grid=(n // tm,),
        in_specs=[pl.BlockSpec((tm, D_MODEL), lambda i: (i, 0)),
                  pl.BlockSpec((1, D_MODEL), lambda i: (0, 0)),
                  pl.BlockSpec((None, 3, ROUTER_ROWS, D_MODEL), lambda i: (layer, 0, 0, 0)),
                  pl.BlockSpec((None, ROUTER_ROWS, 1), lambda i: (layer, 0, 0))],
        out_specs=[pl.BlockSpec((tm * ROW_TILES, LANES), lambda i: (i, 0)),
                   pl.BlockSpec((8, tm), lambda i: (0, i)),
                   pl.BlockSpec((N_EXPERTS, LANES), lambda i: (0, 0))],
        out_shape=[jax.ShapeDtypeStruct((n * ROW_TILES, LANES), F32), jax.ShapeDtypeStruct((8, n), F32),
                   jax.ShapeDtypeStruct((N_EXPERTS, LANES), F32)],
        scratch_shapes=[pltpu.VMEM((N_EXPERTS, LANES), F32), pltpu.VMEM((tm, tm), BF16)],
        compiler_params=_params(("arbitrary",)), name="router")(x, gain.reshape(1, D_MODEL), w_split, bias)
    return xn, slab, cnt


def _moe_plan(slab, cnt, tm, blk):
    n = slab.shape[1]
    e_ids = slab[0:2].astype(I32)
    ranks = slab[4:6].astype(I32)
    counts = cnt[:, 0].astype(I32)
    padded = ((counts + tm - 1) // tm) * tm
    pend = jnp.cumsum(padded)
    pstart = pend - padded
    onehot = e_ids[..., None] == jnp.arange(N_EXPERTS, dtype=I32)
    dest = jnp.sum(jnp.where(onehot, pstart, 0), axis=-1) + ranks
    dest = dest.T.reshape(n // blk, 1, 2 * blk)
    slots = 2 * n + N_EXPERTS * tm
    n_tiles = slots // tm
    n_used = (pend[-1] // tm).astype(I32)
    tile_start = jnp.arange(n_tiles, dtype=I32) * tm
    tile_e = jnp.sum((pend[None, :] <= tile_start[:, None]).astype(I32), axis=1)
    tile_e = jnp.minimum(tile_e, N_EXPERTS - 1)
    last_e = jnp.sum(jnp.where(jnp.arange(n_tiles) == n_used - 1, tile_e, 0))
    tile_e = jnp.where(jnp.arange(n_tiles) < n_used, tile_e, last_e).astype(I32)
    return dest, tile_e, n_used.reshape(1), slots


def _dispatch_body(dest_ref, xn_ref, xs_in_hbm, xs_hbm, sem, *, blk):
    del xs_in_hbm

    def issue(i, carry):
        src = xn_ref.at[pl.ds(pl.multiple_of(i * ROW_TILES, ROW_TILES), ROW_TILES)]
        pltpu.make_async_copy(src, xs_hbm.at[dest_ref[0, 0, 2 * i]], sem).start()
        pltpu.make_async_copy(src, xs_hbm.at[dest_ref[0, 0, 2 * i + 1]], sem).start(priority=1)
        return carry

    lax.fori_loop(0, blk, issue, 0, unroll=8)

    def drain(i, carry):
        pltpu.make_async_copy(xn_ref.at[pl.ds(0, ROW_TILES)], xs_hbm.at[0], sem).wait()
        return carry

    lax.fori_loop(0, 2 * blk, drain, 0, unroll=8)


def _dispatch(xn, dest, slots, blk):
    n = xn.shape[0] // ROW_TILES
    xs0 = jnp.zeros((slots, ROW_TILES, LANES), F32)
    xs = pl.pallas_call(
        functools.partial(_dispatch_body, blk=blk), grid=(n // blk,),
        in_specs=[pl.BlockSpec((1, 1, 2 * blk), lambda i: (i, 0, 0), memory_space=pltpu.SMEM),
                  pl.BlockSpec((blk * ROW_TILES, LANES), lambda i: (i, 0)), pl.BlockSpec(memory_space=pl.ANY)],
        out_specs=pl.BlockSpec(memory_space=pl.ANY),
        out_shape=jax.ShapeDtypeStruct((slots, ROW_TILES, LANES), F32),
        scratch_shapes=[pltpu.SemaphoreType.DMA(())],
        input_output_aliases={2: 0},
        compiler_params=_params(("arbitrary",)), name="dispatch")(dest, xn, xs0)
    return xs.reshape(slots * ROW_TILES, LANES)


def _moe_body(te_ref, nu_ref, xs_ref, wg_ref, wu_ref, wd_ref, o_ref, wgb_ref, wub_ref, wdb_ref, *, tm):
    t = pl.program_id(0)
    e = te_ref[t]
    prev = te_ref[jnp.maximum(t - 1, 0)]

    @pl.when((t == 0) | (e != prev))
    def _():
        wgb_ref[...] = wg_ref[0].astype(BF16)
        wub_ref[...] = wu_ref[0].astype(BF16)
        wdb_ref[...] = wd_ref[0].astype(BF16)

    @pl.when(t < nu_ref[0])
    def _():
        x = _tiles_to_rows(xs_ref, tm).astype(BF16)
        g = _dot(x, wgb_ref[...])
        u = _dot(x, wub_ref[...])
        hid = _silu(g) * u
        _rows_to_tiles(o_ref, _dot(hid.astype(BF16), wdb_ref[...]))


def _moe_experts(xs, tile_e, n_used, w_gate, w_up, w_down, layer, tm):
    slots = xs.shape[0] // ROW_TILES
    n_tiles = slots // tm
    row = lambda t, te, nu: (jnp.minimum(t, nu[0] - 1), 0)
    wsel = lambda t, te, nu: (layer, te[t], 0, 0)
    grid_spec = pltpu.PrefetchScalarGridSpec(
        num_scalar_prefetch=2, grid=(n_tiles,),
        in_specs=[pl.BlockSpec((tm * ROW_TILES, LANES), row),
                  pl.BlockSpec((None, 1, D_MODEL, D_EXPERT), wsel),
                  pl.BlockSpec((None, 1, D_MODEL, D_EXPERT), wsel),
                  pl.BlockSpec((None, 1, D_EXPERT, D_MODEL), wsel)],
        out_specs=pl.BlockSpec((tm * ROW_TILES, LANES), row),
        scratch_shapes=[pltpu.VMEM((D_MODEL, D_EXPERT), BF16), pltpu.VMEM((D_MODEL, D_EXPERT), BF16),
                        pltpu.VMEM((D_EXPERT, D_MODEL), BF16)])
    return pl.pallas_call(
        functools.partial(_moe_body, tm=tm), grid_spec=grid_spec,
        out_shape=jax.ShapeDtypeStruct((slots * ROW_TILES, LANES), F32),
        compiler_params=_params(("arbitrary",)), name="moe_experts")(tile_e, n_used, xs, w_gate, w_up, w_down)


def _ple_body(*refs, final, tm):
    dest_ref, x_ref, gw_ref, p_ref, g_ref, wg_ref, wp_ref = refs[:7]
    if final:
        fg_ref, ys_hbm, o_ref, f_ref, y0_ref, y1_ref, sem = refs[7:]
    else:
        ys_hbm, o_ref, y0_ref, y1_ref, sem = refs[7:]

    def issue(i, carry):
        row = pl.multiple_of(i * ROW_TILES, ROW_TILES)
        pltpu.make_async_copy(ys_hbm.at[dest_ref[0, 0, 2 * i]], y0_ref.at[pl.ds(row, ROW_TILES)], sem).start()
        pltpu.make_async_copy(ys_hbm.at[dest_ref[0, 0, 2 * i + 1]], y1_ref.at[pl.ds(row, ROW_TILES)], sem).start(priority=1)
        return carry

    lax.fori_loop(0, tm, issue, 0, unroll=8)
    pe = _dot(p_ref[...].astype(BF16), wp_ref[...])

    def drain(i, carry):
        pltpu.make_async_copy(ys_hbm.at[0], y0_ref.at[pl.ds(0, ROW_TILES)], sem).wait()
        return carry

    lax.fori_loop(0, 2 * tm, drain, 0, unroll=8)
    gw = gw_ref[...]
    x2 = x_ref[...] + gw[:, 2:3] * _tiles_to_rows(y0_ref, tm) + gw[:, 3:4] * _tiles_to_rows(y1_ref, tm)
    xn = _rms(x2, g_ref[...])
    gate = _sigmoid(_dot(xn.astype(BF16), wg_ref[...]))
    out = x2 + pe * gate
    o_ref[...] = out
    if final:
        f_ref[...] = _rms(out, fg_ref[...])


def _ple(x, ys, dest, gw, p, gain, w_gate, w_proj, layer, final_gain, tm):
    n = x.shape[0]
    dp = p.shape[-1]
    final = final_gain is not None
    rowspec = pl.BlockSpec((tm, D_MODEL), lambda i: (i, 0))
    vec = pl.BlockSpec((1, D_MODEL), lambda i: (0, 0))
    in_specs = [pl.BlockSpec((1, 1, 2 * tm), lambda i: (i, 0, 0), memory_space=pltpu.SMEM),
                rowspec, pl.BlockSpec((tm, 8), lambda i: (i, 0)),
                pl.BlockSpec((None, tm, dp), lambda i: (layer, i, 0)), vec,
                pl.BlockSpec((None, D_MODEL, D_MODEL), lambda i: (layer, 0, 0)),
                pl.BlockSpec((None, dp, D_MODEL), lambda i: (layer, 0, 0))]
    args = [dest, x, gw, p, gain.reshape(1, D_MODEL), w_gate, w_proj]
    out_shape = [jax.ShapeDtypeStruct((n, D_MODEL), F32)]
    out_specs = [rowspec]
    if final:
        in_specs.append(vec)
        args.append(final_gain.reshape(1, D_MODEL))
        out_shape.append(jax.ShapeDtypeStruct((n, D_MODEL), F32))
        out_specs.append(rowspec)
    in_specs.append(pl.BlockSpec(memory_space=pl.ANY))
    args.append(ys.reshape(-1, ROW_TILES, LANES))
    outs = pl.pallas_call(
        functools.partial(_ple_body, final=final, tm=tm), grid=(n // tm,), in_specs=in_specs,
        out_specs=out_specs, out_shape=out_shape,
        scratch_shapes=[pltpu.VMEM((tm * ROW_TILES, LANES), F32), pltpu.VMEM((tm * ROW_TILES, LANES), F32),
                        pltpu.SemaphoreType.DMA(())],
        compiler_params=_params(("arbitrary",)), name="ple")(*args)
    return (outs[0], outs[1]) if final else (outs[0], None)


ATTN_GROUP = 4


def _attn_group(qs, kcats, vcats, first_keys):
    R = CHUNK
    row = _iota((R, 2 * R), 0)
    col = _iota((R, 2 * R), 1)
    ss = []
    for q, kcat, first_key in zip(qs, kcats, first_keys):
        band = (col >= jnp.maximum(row, first_key)) & (col <= row + R)
        ss.append(jnp.where(band, _dot_nt(q, kcat) * (HEAD_DIM ** -0.5), -jnp.inf))
    s = jnp.concatenate(ss, axis=0)
    m = jnp.max(s, axis=-1, keepdims=True)
    p = jnp.exp(s - m)
    l = jnp.sum(p, axis=-1, keepdims=True)
    pb = p.astype(BF16)
    inv = 1.0 / l
    lse = m + jnp.log(l)
    outs, lses = [], []
    for i, vcat in enumerate(vcats):
        sl = slice(i * R, (i + 1) * R)
        outs.append(_dot(pb[sl], vcat) * inv[sl])
        lses.append(jnp.broadcast_to(lse[sl], (R, LANES)))
    return outs, lses


def _attn_body(q0_ref, q1_ref, q2_ref, k_ref, v_ref, o_ref, kp_ref, vp_ref, og_ref, lg_ref):
    R = CHUNK
    first_key = jnp.where(pl.program_id(2) > 0, 0, R)

    @pl.when(pl.program_id(2) == 0)
    def _():
        kp_ref[...] = jnp.zeros_like(kp_ref)
        vp_ref[...] = jnp.zeros_like(vp_ref)

    def rows(ref, start, dil):
        idx = pl.ds(start, R) if dil == 1 else pl.ds(start, R, stride=dil)
        return ref[idx, :]

    def put(ref, start, dil, val):
        idx = pl.ds(start, R) if dil == 1 else pl.ds(start, R, stride=dil)
        ref[idx, :] = val

    def group(g, q_ref, dil, blocks):
        nblk = ATTN_SPAN // (R * dil)
        qs, kcats, vcats, fks, starts = [], [], [], [], []
        for r, cb in blocks:
            start = r + dil * R * cb
            if cb == 0:
                prev_k, prev_v, prev_start, fk = kp_ref, vp_ref, r + dil * R * (nblk - 1), first_key
            else:
                prev_k, prev_v, prev_start, fk = k_ref, v_ref, start - dil * R, 0
            kcats.append(jnp.concatenate([rows(prev_k, prev_start, dil).astype(BF16),
                                          rows(k_ref, start, dil).astype(BF16)], axis=0))
            vcats.append(jnp.concatenate([rows(prev_v, prev_start, dil).astype(BF16),
                                          rows(v_ref, start, dil).astype(BF16)], axis=0))
            qs.append(rows(q_ref, start, dil).astype(BF16))
            fks.append(fk)
            starts.append(start)
        outs, lses = _attn_group(qs, kcats, vcats, fks)
        for start, o, lse in zip(starts, outs, lses):
            put(og_ref.at[g], start, dil, o)
            put(lg_ref.at[g], start, dil, lse)

    for g, (q_ref, (_, dil)) in enumerate(zip((q0_ref, q1_ref, q2_ref), PATTERNS)):
        nblk = ATTN_SPAN // (R * dil)
        if nblk >= ATTN_GROUP:
            def lane(r, carry, q_ref=q_ref, dil=dil, nblk=nblk, g=g):
                for c0 in range(0, nblk, ATTN_GROUP):
                    group(g, q_ref, dil, [(r, cb) for cb in range(c0, c0 + ATTN_GROUP)])
                return carry
            n_iter = dil
        else:
            def lane(i, carry, q_ref=q_ref, dil=dil, nblk=nblk, g=g):
                for cb in range(nblk):
                    group(g, q_ref, dil, [(i * ATTN_GROUP + j, cb) for j in range(ATTN_GROUP)])
                return carry
            n_iter = dil // ATTN_GROUP
        if n_iter == 1:
            lane(0, 0)
        else:
            lax.fori_loop(0, n_iter, lane, 0)

    step = 256
    for i in range(ATTN_SPAN // step):
        sl = pl.ds(i * step, step)
        l0 = lg_ref[0, sl, :]
        l1 = lg_ref[1, sl, :]
        l2 = lg_ref[2, sl, :]
        m = jnp.maximum(jnp.maximum(l0, l1), l2)
        e0 = jnp.exp(l0 - m)
        e1 = jnp.exp(l1 - m)
        e2 = jnp.exp(l2 - m)
        num = e0 * og_ref[0, sl, :] + e1 * og_ref[1, sl, :] + e2 * og_ref[2, sl, :]
        o_ref[sl, :] = num * (1.0 / (e0 + e1 + e2))
    kp_ref[...] = k_ref[...]
    vp_ref[...] = v_ref[...]


def _attn_prompt(q, k, v, b, s):
    nspan = s // ATTN_SPAN
    blk = (ATTN_SPAN, HEAD_DIM)
    hblk = (None,) + blk
    qspec = lambda g: pl.BlockSpec(hblk, lambda bi, h, c: (g * N_KV_HEADS + h, bi * nspan + c, 0))
    kvspec = pl.BlockSpec(hblk, lambda bi, h, c: (h, bi * nspan + c, 0))
    return pl.pallas_call(
        _attn_body, grid=(b, N_KV_HEADS, nspan),
        in_specs=[qspec(0), qspec(1), qspec(2), kvspec, kvspec],
        out_specs=kvspec, out_shape=jax.ShapeDtypeStruct((N_KV_HEADS, b * s, HEAD_DIM), F32),
        scratch_shapes=[pltpu.VMEM(blk, F32), pltpu.VMEM(blk, F32),
                        pltpu.VMEM((3,) + blk, F32), pltpu.VMEM((3,) + blk, F32)],
        compiler_params=_params(("parallel", "parallel", "arbitrary")), name="attn_prompt")(q, q, q, k, v)


SAMPLE_ROWS = 16
DIL_MAX = PATTERNS[-1][1]
DENSE_TAIL = PATTERNS[1][0]


def _attn_sample_body(q_ref, kf_ref, vf_ref, kt_ref, vt_ref, kn_ref, vn_ref, bias_ref, o_ref, *, t_new):
    nq = N_KV_HEADS * SAMPLE_ROWS
    n_far = kf_ref.shape[1] * kf_ref.shape[2]
    n_tail = kt_ref.shape[1]
    q = q_ref[0].astype(BF16)
    kf = kf_ref[0].reshape(n_far, HEAD_DIM).astype(BF16)
    vf = vf_ref[0].reshape(n_far, HEAD_DIM).astype(BF16)
    s = jnp.concatenate([_dot_nt(q, kf), _dot_nt(q, kt_ref[0].astype(BF16)), _dot_nt(q, kn_ref[0].astype(BF16))],
                        axis=1) * (HEAD_DIM ** -0.5) + bias_ref[...]
    m = jnp.max(s, axis=-1, keepdims=True)
    p = jnp.exp(s - m)
    l = jnp.sum(p, axis=-1, keepdims=True)
    pb = p.astype(BF16)
    o = (_dot(pb[:, :n_far], vf) + _dot(pb[:, n_far:n_far + n_tail], vt_ref[0].astype(BF16))
         + _dot(pb[:, n_far + n_tail:], vn_ref[0].astype(BF16))) * (1.0 / l)
    lse = jnp.broadcast_to(m + jnp.log(l), (nq, LANES))
    lse1 = pltpu.roll(lse, nq - t_new, 0)
    lse2 = pltpu.roll(lse, nq - 2 * t_new, 0)
    o1 = pltpu.roll(o, nq - t_new, 0)
    o2 = pltpu.roll(o, nq - 2 * t_new, 0)
    mm = jnp.maximum(jnp.maximum(lse, lse1), lse2)
    e0 = jnp.exp(lse - mm)
    e1 = jnp.exp(lse1 - mm)
    e2 = jnp.exp(lse2 - mm)
    comb = (e0 * o + e1 * o1 + e2 * o2) * (1.0 / (e0 + e1 + e2))
    for h in range(N_KV_HEADS):
        o_ref[0, :, h * HEAD_DIM:(h + 1) * HEAD_DIM] = comb[h * SAMPLE_ROWS:h * SAMPLE_ROWS + t_new]


def _sample_bias(wb, t_new, far):
    rho = jnp.arange(N_KV_HEADS * SAMPLE_ROWS, dtype=I32)
    h = rho // SAMPLE_ROWS
    gt = rho % SAMPLE_ROWS
    g = jnp.where(gt < len(PATTERNS) * t_new, gt // t_new, 0)
    t = gt % t_new
    win = jnp.asarray([w for w, _ in PATTERNS], I32)[g]
    dil = jnp.asarray([d for _, d in PATTERNS], I32)[g]
    far_rows = t_new * N_KV_HEADS
    c = jnp.arange(far // DIL_MAX * far_rows, dtype=I32)
    pos_far = DIL_MAX * (c // far_rows) + (c % far_rows) // N_KV_HEADS
    c = jnp.arange((wb - far) * N_KV_HEADS, dtype=I32)
    pos_tail = far + c // N_KV_HEADS
    c = jnp.arange(LANES, dtype=I32)
    pos_new = jnp.where(c < far_rows, wb + c // N_KV_HEADS, -1)
    pos = jnp.concatenate([pos_far, pos_tail, pos_new])
    head = jnp.concatenate([jnp.arange(n, dtype=I32) % N_KV_HEADS for n in (pos_far.shape[0], pos_tail.shape[0], LANES)])
    dist = wb + t[:, None] - pos[None, :]
    valid = ((head[None, :] == h[:, None]) & (pos[None, :] >= 0) & (dist >= 0) & (dist <= win[:, None])
             & (dist % dil[:, None] == 0))
    return jnp.where(valid, 0.0, -jnp.inf).astype(F32)


def _attn_sample(q, cache_k, cache_v, k_new, v_new, b, t_new):
    wb = cache_k.shape[1]
    tail = min(DENSE_TAIL, wb)
    far = wb - tail
    far_rows = t_new * N_KV_HEADS
    nq = N_KV_HEADS * SAMPLE_ROWS
    qh = q.reshape(b, t_new, 3, N_KV_HEADS, HEAD_DIM).transpose(0, 3, 2, 1, 4).reshape(b, N_KV_HEADS, 3 * t_new, HEAD_DIM)
    qh = jnp.pad(qh, ((0, 0), (0, 0), (0, SAMPLE_ROWS - 3 * t_new), (0, 0))).reshape(b, nq, HEAD_DIM)
    new = lambda x: jnp.pad(x.reshape(b, far_rows, HEAD_DIM), ((0, 0), (0, LANES - far_rows), (0, 0)))
    grouped = lambda x: x.reshape(b, wb // DIL_MAX, DIL_MAX * N_KV_HEADS, HEAD_DIM)
    flat = lambda x: x.reshape(b, wb * N_KV_HEADS, HEAD_DIM)
    bias = _sample_bias(wb, t_new, far)
    far_spec = pl.BlockSpec((1, far // DIL_MAX, far_rows, HEAD_DIM), lambda bi: (bi, 0, 0, 0))
    tail_spec = pl.BlockSpec((1, tail * N_KV_HEADS, HEAD_DIM), lambda bi: (bi, wb // tail - 1, 0))
    sq_spec = pl.BlockSpec((1, LANES, HEAD_DIM), lambda bi: (bi, 0, 0))
    o = pl.pallas_call(
        functools.partial(_attn_sample_body, t_new=t_new), grid=(b,),
        in_specs=[sq_spec, far_spec, far_spec, tail_spec, tail_spec, sq_spec, sq_spec,
                  pl.BlockSpec(bias.shape, lambda bi: (0, 0))],
        out_specs=pl.BlockSpec((1, t_new, ATTN_DIM), lambda bi: (bi, 0, 0)),
        out_shape=jax.ShapeDtypeStruct((b, t_new, ATTN_DIM), F32),
        compiler_params=_params(("parallel",)), name="attn_sample")(
            qh, grouped(cache_k), grouped(cache_v), flat(cache_k), flat(cache_v), new(k_new), new(v_new), bias)
    return o.reshape(b * t_new, ATTN_DIM)


def _rope_tables(pos0, l, reps):
    inv_freq = ROPE_THETA ** (-jnp.arange(0, HEAD_DIM, 2, dtype=F32) / HEAD_DIM)
    pos = pos0 + jnp.arange(l, dtype=jnp.int32)
    ang = pos.astype(F32)[:, None] * inv_freq[None, :]
    cos = jnp.cos(ang)
    sin = jnp.sin(ang)
    cosf = jnp.concatenate([cos, cos], axis=-1)
    sinf = jnp.concatenate([-sin, sin], axis=-1)
    if reps > 1:
        cosf = jnp.tile(cosf, (reps, 1))
        sinf = jnp.tile(sinf, (reps, 1))
    return cosf, sinf


def _prep_weights(prm):
    w = {}
    w['in_proj'], w['in_proj_lo'] = _split2(prm['ssm_in_proj'])
    w['in_dt'], w['in_dt_lo'] = _split2(
        jnp.pad(prm['ssm_in_proj'][:, :, ZXBC_DIM:], ((0, 0), (0, 0), (0, LANES - SSM_HEADS))))
    w['out_proj'], w['out_proj_lo'] = _split2(prm['ssm_out_proj'])
    w['w_kv'] = prm['w_kv'].astype(BF16)
    w['w_q'] = prm['attn_w_q'].astype(BF16)
    w['w_o'] = prm['attn_w_o'].astype(BF16)
    w['ple_gate'] = prm['ple_w_gate'].astype(BF16)
    w['ple_proj'] = prm['ple_w_proj'].astype(BF16)
    wr = jnp.concatenate([jnp.swapaxes(prm['moe_w_rg'], 1, 2),
                          jnp.zeros((DEPTH, 8 - N_EXPERT_GROUPS, D_MODEL), F32),
                          jnp.swapaxes(prm['moe_w_re'], 1, 2)], axis=1)
    w['router'] = jnp.stack(_split3(wr), axis=1)
    w['router_b'] = jnp.concatenate([prm['moe_b_rg'], jnp.zeros((DEPTH, 8 - N_EXPERT_GROUPS), F32),
                                     prm['moe_b_re']], axis=1).reshape(DEPTH, ROUTER_ROWS, 1)
    return w


def _trunk(x, p, pos0, conv_state, ssm_state, cache_k, cache_v, prm, w):
    b, l, _ = x.shape
    n = b * l
    sample = cache_k is not None
    tm = min(512, n)
    moe_tm = 256 if n >= 4096 else 32
    xf = x.reshape(n, D_MODEL)
    rope = _rope_tables(pos0, l, 1 if l >= tm else tm // l)
    new_conv, new_ssm = [], []
    k32 = v32 = kh = vh = None
    y_final = None
    for i in range(DEPTH):
        gain = prm['norm_mix'][i]
        if i < N_A_LAYERS:
            precise = i < PRECISE_MIXERS
            lo = (lambda name: w[name + '_lo']) if precise else (lambda name: None)
            zx = _proj(xf, w['in_proj'], w_lo=lo('in_proj'), layer=i, gain=gain, nout=ZXBC_DIM, tm=tm, tn=1024)
            dtr = _proj(xf, w['in_dt'], w_lo=lo('in_dt'), layer=i, gain=gain, tm=tm, tn=LANES)
            y, cbuf, hs = _ssd(zx.reshape(b, l, ZXBC_DIM), dtr.reshape(b, l, LANES), prm['ssm_conv_w'][i],
                               prm['ssm_conv_b'][i], prm['ssm_dt_bias'][i], prm['ssm_a_log'][i], prm['ssm_d'][i],
                               prm['ssm_norm_y'][i], conv_state[i], ssm_state, i, precise)
            new_conv.append(cbuf)
            new_ssm.append(hs)
            xf = _proj(y.reshape(n, D_INNER), w['out_proj'], w_lo=lo('out_proj'), layer=i, residual=xf, tm=tm, tn=1024)
        else:
            j = i - N_A_LAYERS
            q = _proj(xf, w['w_q'], layer=j, gain=gain, rope=rope, out_heads=(not sample,), tm=tm, tn=1024)
            if sample:
                att = _attn_sample(q, cache_k, cache_v, k32, v32, b, l)
            else:
                att = _attn_prompt(q, kh, vh, b, l)
            xf = _proj(att, w['w_o'], layer=j, residual=xf, tm=tm, tn=1024)
        xn, slab, cnt = _router(xf, prm['norm_ffn'][i], w['router'], w['router_b'], i, tm)
        dest, tile_e, n_used, slots = _moe_plan(slab, cnt, moe_tm, tm)
        xs = _dispatch(xn, dest, slots, tm)
        ys = _moe_experts(xs, tile_e, n_used, prm['moe_w_gate'], prm['moe_w_up'], prm['moe_w_down'], i, moe_tm)
        xf, y_final = _ple(xf, ys, dest, slab.T, p.reshape(DEPTH, n, -1), prm['ple_norm'][i], w['ple_gate'],
                           w['ple_proj'], i, prm['norm_final'] if i == DEPTH - 1 else None, tm)
        if i == N_A_LAYERS - 1:
            kv_out = dict(out_dtypes=(F32,), out_heads=(False,)) if sample else \
                dict(out_dtypes=(F32, F32), out_heads=(False, True))
            k_outs = _proj(xf, w['w_kv'], gain=prm['norm_kv'], rope=rope, nout=ATTN_DIM, tm=tm, tn=1024, **kv_out)
            v_outs = _proj(xf, w['w_kv'], col0=ATTN_DIM, gain=prm['norm_kv'], nout=ATTN_DIM, tm=tm, tn=1024, **kv_out)
            (k32, kh), (v32, vh) = ((k_outs, None), (v_outs, None)) if sample else (k_outs, v_outs)
    return (y_final.reshape(b, l, D_MODEL), jnp.stack(new_conv), jnp.stack(new_ssm),
            k32.reshape(b, l, ATTN_DIM), v32.reshape(b, l, ATTN_DIM))


def kernel(x_prompt, x_sample, state_ssm, state_conv, cache_k, cache_v, p_prompt, p_sample, norm_mix, ssm_in_proj, ssm_conv_w, ssm_conv_b, ssm_dt_bias, ssm_a_log, ssm_d, ssm_norm_y, ssm_out_proj, norm_kv, w_kv, attn_w_q, attn_w_o, norm_ffn, moe_w_rg, moe_b_rg, moe_w_re, moe_b_re, moe_w_gate, moe_w_up, moe_w_down, ple_w_proj, ple_w_gate, ple_norm, norm_final):
    prm = dict(norm_mix=norm_mix, ssm_in_proj=ssm_in_proj, ssm_conv_w=ssm_conv_w, ssm_conv_b=ssm_conv_b,
               ssm_dt_bias=ssm_dt_bias, ssm_a_log=ssm_a_log, ssm_d=ssm_d, ssm_norm_y=ssm_norm_y,
               ssm_out_proj=ssm_out_proj, norm_kv=norm_kv, w_kv=w_kv, attn_w_q=attn_w_q, attn_w_o=attn_w_o,
               norm_ffn=norm_ffn, moe_w_rg=moe_w_rg, moe_b_rg=moe_b_rg, moe_w_re=moe_w_re, moe_b_re=moe_b_re,
               moe_w_gate=moe_w_gate, moe_w_up=moe_w_up, moe_w_down=moe_w_down, ple_w_proj=ple_w_proj,
               ple_w_gate=ple_w_gate, ple_norm=ple_norm, norm_final=norm_final)
    w = _prep_weights(prm)
    b, s, _ = x_prompt.shape
    db, dl, _ = x_sample.shape
    past_len = 8192
    conv0 = jnp.zeros((N_A_LAYERS, b, 3, CONV_DIM), x_prompt.dtype)
    ssm0 = jnp.zeros((N_A_LAYERS, b, SSM_HEADS, SSM_HEADDIM, D_STATE), state_ssm.dtype)
    y_p, conv_p, ssm_p, k_p, v_p = _trunk(x_prompt, p_prompt, 0, conv0, ssm0, None, None, prm, w)
    keep = min(2048, s)
    heads = lambda t: t.reshape(t.shape[0], t.shape[1], N_KV_HEADS, HEAD_DIM)
    k_p = heads(k_p[:, s - keep:])
    v_p = heads(v_p[:, s - keep:])
    y_s, conv_s, ssm_s, k_s, v_s = _trunk(x_sample, p_sample, past_len, state_conv, state_ssm,
                                          cache_k, cache_v, prm, w)
    return (y_p, y_s, ssm_p, conv_p, k_p, v_p, ssm_s, conv_s, heads(k_s), heads(v_s))
```

```python
import functools

import jax
import jax.numpy as jnp
from jax import lax
from jax.experimental import pallas as pl
from jax.experimental.pallas import tpu as pltpu

F32 = jnp.float32
BF16 = jnp.bfloat16
I32 = jnp.int32

EPS = 1e-6
D_MODEL = 1024
D_INNER = 2048
SSM_HEADS = 32
SSM_HEADDIM = 64
SSM_GROUPS = 4
D_STATE = 128
CONV_DIM = 3072
ZXBC_DIM = D_INNER + CONV_DIM
CHUNK = 128
HEAD_DIM = 128
N_KV_HEADS = 8
ATTN_DIM = 1024
PATTERNS = ((128, 1), (512, 4), (2048, 16))
ATTN_SPAN = 2048
ROPE_THETA = 10000.0
N_EXPERTS = 32
N_EXPERT_GROUPS = 4
EXPERTS_PER_GROUP = 8
D_EXPERT = 512
N_A_LAYERS = 2
DEPTH = 4
PRECISE_MIXERS = 1
MOE_TILE = 256
ROUTER_ROWS = 8 + N_EXPERTS
LANES = 128
SUBLANES = 8
ROW_TILES = D_MODEL // LANES
VMEM_LIMIT = 56 * 1024 * 1024


def _params(sem):
    return pltpu.CompilerParams(dimension_semantics=sem, vmem_limit_bytes=VMEM_LIMIT)


def _iota(shape, dim):
    return lax.broadcasted_iota(I32, shape, dim)


def _rms(x, gain):
    ms = jnp.mean(x * x, axis=-1, keepdims=True)
    return x * lax.rsqrt(ms + EPS) * gain


def _sigmoid(x):
    return 0.5 * jnp.tanh(0.5 * x) + 0.5


def _silu(x):
    return x * _sigmoid(x)


def _dot(a, b):
    return jnp.dot(a, b, preferred_element_type=F32)


def _dot_nt(a, b):
    return lax.dot_general(a, b, (((1,), (1,)), ((), ())), preferred_element_type=F32)


def _split3(x):
    hi = x.astype(BF16)
    r = x - hi.astype(F32)
    mid = r.astype(BF16)
    lo = (r - mid.astype(F32)).astype(BF16)
    return hi, mid, lo


def _split2(x):
    hi = x.astype(BF16)
    return hi, (x - hi.astype(F32)).astype(BF16)


def _mm(a, b, dot):
    y = dot(a[0], b[0])
    if len(a) > 1:
        y = y + dot(a[1], b[0]) + dot(a[0], b[1])
    return y


def _rows_to_tiles(ref, x):
    m = x.shape[0]
    for s in range(ROW_TILES):
        ref[pl.ds(s, m, stride=ROW_TILES), :] = x[:, s * LANES:(s + 1) * LANES]


def _tiles_to_rows(ref, m):
    return jnp.concatenate([ref[pl.ds(s, m, stride=ROW_TILES), :] for s in range(ROW_TILES)], axis=1)


def _proj_body(*refs, has_norm, has_rope, has_res, x_heads, out_heads, precise):
    it = iter(refs)
    x_ref = next(it)
    w_ref = next(it)
    wlo_ref = next(it) if precise else None
    g_ref = next(it) if has_norm else None
    cos_ref = next(it) if has_rope else None
    sin_ref = next(it) if has_rope else None
    r_ref = next(it) if has_res else None
    o_refs = [next(it) for _ in out_heads]
    xb_ref = next(it)
    xlo_ref = next(it) if precise else None

    @pl.when(pl.program_id(1) == 0)
    def _():
        if x_heads:
            x = jnp.concatenate([x_ref[h] for h in range(x_ref.shape[0])], axis=1).astype(F32)
        else:
            x = x_ref[...].astype(F32)
        if has_norm:
            x = _rms(x, g_ref[...])
        if precise:
            xb_ref[...], xlo_ref[...] = _split2(x)
        else:
            xb_ref[...] = x.astype(BF16)

    if precise:
        y = _mm((xb_ref[...], xlo_ref[...]), (w_ref[...], wlo_ref[...]), _dot)
    else:
        y = _dot(xb_ref[...], w_ref[...])
    if has_rope:
        cos = cos_ref[...]
        sin = sin_ref[...]
        parts = []
        for h in range(y.shape[1] // HEAD_DIM):
            t = y[:, h * HEAD_DIM:(h + 1) * HEAD_DIM]
            parts.append(t * cos + pltpu.roll(t, HEAD_DIM // 2, 1) * sin)
        y = parts[0] if len(parts) == 1 else jnp.concatenate(parts, axis=1)
    if has_res:
        y = y + r_ref[...]
    for o, heads in zip(o_refs, out_heads):
        if heads:
            for h in range(o.shape[0]):
                o[h] = y[:, h * HEAD_DIM:(h + 1) * HEAD_DIM].astype(o.dtype)
        else:
            o[...] = y.astype(o.dtype)


def _proj(x, w, *, w_lo=None, layer=None, col0=0, gain=None, rope=None, residual=None, out_dtypes=(F32,),
          out_heads=None, nout=None, tm, tn):
    x_heads = x.ndim == 3
    n = x.shape[-2]
    k = x.shape[0] * x.shape[2] if x_heads else x.shape[1]
    nout = w.shape[-1] if nout is None else nout
    tn = min(tn, nout)
    out_heads = (False,) * len(out_dtypes) if out_heads is None else out_heads
    grid = (n // tm, nout // tn)
    jb = col0 // tn
    precise = w_lo is not None
    wspec = (pl.BlockSpec((k, tn), lambda i, j: (0, j + jb)) if layer is None else
             pl.BlockSpec((None, k, tn), lambda i, j: (layer, 0, j + jb)))
    in_specs = [pl.BlockSpec((k // HEAD_DIM, tm, HEAD_DIM), lambda i, j: (0, i, 0)) if x_heads else
                pl.BlockSpec((tm, k), lambda i, j: (i, 0)), wspec]
    args = [x, w]
    if precise:
        in_specs.append(wspec)
        args.append(w_lo)
    if gain is not None:
        in_specs.append(pl.BlockSpec((1, k), lambda i, j: (0, 0)))
        args.append(gain.reshape(1, k))
    if rope is not None:
        cos, sin = rope
        nblk = cos.shape[0] // tm
        for t in (cos, sin):
            in_specs.append(pl.BlockSpec((tm, HEAD_DIM), lambda i, j: (i % nblk, 0)))
            args.append(t)
    if residual is not None:
        in_specs.append(pl.BlockSpec((tm, tn), lambda i, j: (i, j)))
        args.append(residual)
    out_shape = [jax.ShapeDtypeStruct((nout // HEAD_DIM, n, HEAD_DIM) if hm else (n, nout), dt)
                 for dt, hm in zip(out_dtypes, out_heads)]
    out_specs = [pl.BlockSpec((tn // HEAD_DIM, tm, HEAD_DIM), lambda i, j: (j, i, 0)) if hm else
                 pl.BlockSpec((tm, tn), lambda i, j: (i, j)) for hm in out_heads]
    body = functools.partial(_proj_body, has_norm=gain is not None, has_rope=rope is not None,
                             has_res=residual is not None, x_heads=x_heads, out_heads=tuple(out_heads),
                             precise=precise)
    outs = pl.pallas_call(
        body, grid=grid, in_specs=in_specs, out_specs=out_specs, out_shape=out_shape,
        scratch_shapes=[pltpu.VMEM((tm, k), BF16)] * (2 if precise else 1),
        compiler_params=_params(("parallel", "arbitrary")), name="proj")(*args)
    return outs[0] if len(outs) == 1 else outs


def _ssd_body(zx_ref, dt_ref, cw_ref, cb_ref, dtb_ref, alog_ref, d_ref, ny_ref, conv0_ref, ssm0_ref,
              y_ref, convo_ref, ssmo_ref, st_ref, tail_ref, xbc_ref, yacc_ref, *pad_refs, valid, nchunks, precise):
    L = CHUNK
    c = pl.program_id(1)
    npair = SSM_HEADS // 2
    parts = _split2 if precise else (lambda x: (x.astype(BF16),))

    @pl.when(c == 0)
    def _():
        for j in range(npair):
            st_ref[:, j * LANES:(j + 1) * LANES] = ssm0_ref[0, j * LANES:(j + 1) * LANES, :].T
        tail_ref[0:8, :] = conv0_ref[0]

    if valid < L:
        pzx_ref, pdt_ref = pad_refs
        pzx_ref[...] = jnp.zeros_like(pzx_ref)
        pdt_ref[...] = jnp.zeros_like(pdt_ref)
        pzx_ref[0:valid, :] = zx_ref[0]
        pdt_ref[0:valid, :] = dt_ref[0]
        z = pzx_ref[:, :D_INNER]
        u = pzx_ref[:, D_INNER:]
        dtr = pdt_ref[...]
    else:
        z = zx_ref[0, :, :D_INNER]
        u = zx_ref[0, :, D_INNER:]
        dtr = dt_ref[0]

    tail_ref[8:8 + L, :] = u
    acc = cb_ref[...] + cw_ref[3:4, :] * u
    for k in range(1, 4):
        acc = acc + cw_ref[3 - k:4 - k, :] * tail_ref[8 - k:8 - k + L, :]
    tail_ref[0:8, :] = tail_ref[valid:valid + 8, :]
    xbc_ref[...] = _silu(acc)

    lane = _iota((1, LANES), 1)
    rowl = _iota((L, 1), 0)
    xdt = dtr + dtb_ref[...]
    dt = jnp.maximum(xdt, 0.0) + jnp.log(1.0 + jnp.exp(-jnp.abs(xdt)))
    dt = jnp.where((lane < SSM_HEADS) & (rowl < valid), dt, 0.0)
    a_neg = -jnp.exp(alog_ref[...])
    da = dt * a_neg
    rr = _iota((L, L), 0)
    cc = _iota((L, L), 1)
    causal = rr >= cc
    tri = jnp.where(causal, 1.0, 0.0).astype(BF16)
    hi, mid, lo = _split3(da)
    a_cs = _dot(tri, hi) + _dot(tri, mid) + _dot(tri, lo)
    a_cs_t = a_cs.T
    dt_t = dt.T
    tdt_t = dt_t * jnp.exp(a_cs_t[:, L - 1:L] - a_cs_t)
    ea = jnp.exp(a_cs)
    half = _iota((1, LANES), 1) < SSM_HEADDIM

    for g in range(SSM_GROUPS):
        b_g = xbc_ref[:, D_INNER + g * D_STATE:D_INNER + (g + 1) * D_STATE]
        c_g = xbc_ref[:, D_INNER + SSM_GROUPS * D_STATE + g * D_STATE:
                      D_INNER + SSM_GROUPS * D_STATE + (g + 1) * D_STATE]
        cb = _mm(parts(c_g), parts(b_g), _dot_nt)
        b_t = b_g.T
        for jj in range(npair // SSM_GROUPS):
            j = g * (npair // SSM_GROUPS) + jj
            cols = slice(j * LANES, (j + 1) * LANES)
            xs_b = parts(xbc_ref[:, cols])
            st_p = st_ref[:, cols]
            rhs = tuple(jnp.concatenate([a, b], axis=0) for a, b in zip(xs_b, parts(st_p)))
            ys, ss, cds = [], [], []
            for h in (2 * j, 2 * j + 1):
                colb = jnp.broadcast_to(a_cs[:, h:h + 1], (L, LANES))
                seg = colb - a_cs_t[h:h + 1, :]
                dec = jnp.exp(jnp.where(causal, seg, -jnp.inf))
                m = cb * dec * dt_t[h:h + 1, :]
                ecol = jnp.broadcast_to(ea[:, h:h + 1], (L, LANES))
                ce = c_g * ecol
                lhs = tuple(jnp.concatenate([a, b], axis=1) for a, b in zip(parts(m), parts(ce)))
                ys.append(_mm(lhs, rhs, _dot))
                ss.append(_mm(parts(b_t * tdt_t[h:h + 1, :]), xs_b, _dot))
                cds.append(ecol[L - 1:L, :])
            yacc_ref[:, cols] = jnp.where(half, ys[0], ys[1])
            st_ref[:, cols] = jnp.where(half, st_p * cds[0] + ss[0], st_p * cds[1] + ss[1])

    y = yacc_ref[...] + d_ref[...] * xbc_ref[:, :D_INNER]
    yn = _rms(y * _silu(z), ny_ref[...])
    if valid < L:
        y_ref[0] = yn[0:valid].astype(y_ref.dtype)
    else:
        y_ref[0] = yn.astype(y_ref.dtype)

    @pl.when(c == nchunks - 1)
    def _():
        convo_ref[0] = tail_ref[0:8, :]
        for j in range(npair):
            ssmo_ref[0, j * LANES:(j + 1) * LANES, :] = st_ref[:, j * LANES:(j + 1) * LANES].T


def _ssd(zx, dtr, conv_w, conv_b, dt_bias, a_log, d_skip, norm_y, conv0, ssm0, layer, precise):
    b, l, _ = zx.shape
    if l >= CHUNK:
        lb, valid, nchunks = CHUNK, CHUNK, l // CHUNK
    else:
        lb, valid, nchunks = l, l, 1
    pad_lanes = LANES - SSM_HEADS
    dtb = jnp.pad(dt_bias.astype(F32), (0, pad_lanes)).reshape(1, LANES)
    alog = jnp.pad(a_log.astype(F32), (0, pad_lanes)).reshape(1, LANES)
    d_exp = jnp.repeat(d_skip.astype(F32), SSM_HEADDIM).reshape(1, D_INNER)
    conv0p = jnp.pad(conv0, ((0, 0), (5, 0), (0, 0)))
    ssm0f = ssm0.reshape(ssm0.shape[0], b, SSM_HEADS * SSM_HEADDIM, D_STATE)
    const = lambda bi, c: (0, 0)
    in_specs = [
        pl.BlockSpec((1, lb, ZXBC_DIM), lambda bi, c: (bi, c, 0)),
        pl.BlockSpec((1, lb, LANES), lambda bi, c: (bi, c, 0)),
        pl.BlockSpec((4, CONV_DIM), const),
        pl.BlockSpec((1, CONV_DIM), const),
        pl.BlockSpec((1, LANES), const),
        pl.BlockSpec((1, LANES), const),
        pl.BlockSpec((1, D_INNER), const),
        pl.BlockSpec((1, D_INNER), const),
        pl.BlockSpec((1, 8, CONV_DIM), lambda bi, c: (bi, 0, 0)),
        pl.BlockSpec((None, 1, D_INNER, D_STATE), lambda bi, c: (layer, bi, 0, 0)),
    ]
    out_shape = [jax.ShapeDtypeStruct((b, l, D_INNER), F32 if precise else BF16),
                 jax.ShapeDtypeStruct((b, 8, CONV_DIM), F32),
                 jax.ShapeDtypeStruct((b, D_INNER, D_STATE), F32)]
    out_specs = [pl.BlockSpec((1, lb, D_INNER), lambda bi, c: (bi, c, 0)),
                 pl.BlockSpec((1, 8, CONV_DIM), lambda bi, c: (bi, 0, 0)),
                 pl.BlockSpec((1, D_INNER, D_STATE), lambda bi, c: (bi, 0, 0))]
    scratch = [pltpu.VMEM((D_STATE, D_INNER), F32), pltpu.VMEM((8 + CHUNK, CONV_DIM), F32),
               pltpu.VMEM((CHUNK, CONV_DIM), F32), pltpu.VMEM((CHUNK, D_INNER), F32)]
    if valid < CHUNK:
        scratch += [pltpu.VMEM((CHUNK, ZXBC_DIM), F32), pltpu.VMEM((CHUNK, LANES), F32)]
    y, convo, ssmo = pl.pallas_call(
        functools.partial(_ssd_body, valid=valid, nchunks=nchunks, precise=precise),
        grid=(b, nchunks), in_specs=in_specs, out_specs=out_specs, out_shape=out_shape,
        scratch_shapes=scratch, compiler_params=_params(("parallel", "arbitrary")), name="ssd")(
            zx, dtr, conv_w, conv_b.reshape(1, CONV_DIM), dtb, alog, d_exp, norm_y.reshape(1, D_INNER),
            conv0p, ssm0f)
    return y, convo[:, 5:], ssmo.reshape(b, SSM_HEADS, SSM_HEADDIM, D_STATE)


def _router_body(x_ref, g_ref, w_ref, b_ref, xn_ref, slab_ref, cnt_ref, base_ref, su_ref):
    tm = x_ref.shape[0]

    @pl.when(pl.program_id(0) == 0)
    def _():
        base_ref[...] = jnp.zeros_like(base_ref)
        su_ref[...] = jnp.where(_iota((tm, tm), 0) < _iota((tm, tm), 1), 1.0, 0.0).astype(BF16)

    xn = _rms(x_ref[...], g_ref[...])
    _rows_to_tiles(xn_ref, xn)
    xh, xm, xl = _split3(xn)
    wh = w_ref[0]
    wm = w_ref[1]
    wl = w_ref[2]
    logits = (_dot_nt(wh, xh) + _dot_nt(wh, xm) + _dot_nt(wm, xh)
              + _dot_nt(wh, xl) + _dot_nt(wm, xm) + _dot_nt(wl, xh)) + b_ref[...]
    gl = [logits[i:i + 1, :] for i in range(N_EXPERT_GROUPS)]
    best = gl[0]
    sel = jnp.zeros((1, tm), I32)
    for i in range(1, N_EXPERT_GROUPS):
        upd = gl[i] > best
        sel = jnp.where(upd, i, sel)
        best = jnp.where(upd, gl[i], best)
    den = jnp.exp(gl[0] - best)
    for i in range(1, N_EXPERT_GROUPS):
        den = den + jnp.exp(gl[i] - best)
    g_w = 1.0 / den
    chosen = logits[8:8 + EXPERTS_PER_GROUP, :]
    for i in range(1, N_EXPERT_GROUPS):
        chosen = jnp.where(sel == i, logits[8 + i * EXPERTS_PER_GROUP:8 + (i + 1) * EXPERTS_PER_GROUP, :], chosen)
    idx = _iota((EXPERTS_PER_GROUP, tm), 0)
    m1 = jnp.max(chosen, axis=0, keepdims=True)
    i1 = jnp.min(jnp.where(chosen == m1, idx, EXPERTS_PER_GROUP), axis=0, keepdims=True)
    rest = jnp.where(idx == i1, -jnp.inf, chosen)
    m2 = jnp.max(rest, axis=0, keepdims=True)
    i2 = jnp.min(jnp.where(rest == m2, idx, EXPERTS_PER_GROUP), axis=0, keepdims=True)
    t = jnp.exp(m2 - m1)
    inv = 1.0 / (1.0 + t)
    w0 = inv * g_w
    w1 = (t * inv) * g_w
    e0 = sel * EXPERTS_PER_GROUP + i1
    e1 = sel * EXPERTS_PER_GROUP + i2

    eidx = _iota((N_EXPERTS, tm), 0)
    oh0 = eidx == e0
    oh1 = eidx == e1
    oh = jnp.where(oh0, 1.0, jnp.where(oh1, 1.0, 0.0))
    base = base_ref[...]
    before = _dot(oh.astype(BF16), su_ref[...]) + jnp.concatenate([base] * (tm // LANES), axis=1)
    rank0 = jnp.sum(jnp.where(oh0, before, 0.0), axis=0, keepdims=True)
    rank1 = jnp.sum(jnp.where(oh1, before, 0.0), axis=0, keepdims=True)
    new_base = base + jnp.sum(oh, axis=1, keepdims=True)
    base_ref[...] = new_base
    cnt_ref[...] = new_base

    r = _iota((8, tm), 0)
    rows = (e0.astype(F32), e1.astype(F32), w0, w1, rank0, rank1)
    slab = jnp.zeros((8, tm), F32)
    for i, v in enumerate(rows):
        slab = jnp.where(r == i, v, slab)
    slab_ref[...] = slab


def _router(x, gain, w_split, bias, layer, tm):
    n = x.shape[0]
    xn, slab, cnt = pl.pallas_call(
        _router_body, grid=(n // tm,),
        in_specs=[pl.BlockSpec((tm, D_MODEL), lambda i: (i, 0)),
                  pl.BlockSpec((1, D_MODEL), lambda i: (0, 0)),
                  pl.BlockSpec((None, 3, ROUTER_ROWS, D_MODEL), lambda i: (layer, 0, 0, 0)),
                  pl.BlockSpec((None, ROUTER_ROWS, 1), lambda i: (layer, 0, 0))],
        out_specs=[pl.BlockSpec((tm * ROW_TILES, LANES), lambda i: (i, 0)),
                   pl.BlockSpec((8, tm), lambda i: (0, i)),
                   pl.BlockSpec((N_EXPERTS, LANES), lambda i: (0, 0))],
        out_shape=[jax.ShapeDtypeStruct((n * ROW_TILES, LANES), F32), jax.ShapeDtypeStruct((8, n), F32),
                   jax.ShapeDtypeStruct((N_EXPERTS, LANES), F32)],
        scratch_shapes=[pltpu.VMEM((N_EXPERTS, LANES), F32), pltpu.VMEM((tm, tm), BF16)],
        compiler_params=_params(("arbitrary",)), name="router")(x, gain.reshape(1, D_MODEL), w_split, bias)
    return xn, slab, cnt


def _moe_plan(slabs, cnts, tm, blks):
    counts = [cnt[:, 0].astype(I32) for cnt in cnts]
    total = sum(counts)
    padded = ((total + tm - 1) // tm) * tm
    pend = jnp.cumsum(padded)
    pstart = pend - padded
    dests = []
    base = pstart
    for slab, count, blk in zip(slabs, counts, blks):
        n = slab.shape[1]
        e_ids = slab[0:2].astype(I32)
        ranks = slab[4:6].astype(I32)
        onehot = e_ids[..., None] == jnp.arange(N_EXPERTS, dtype=I32)
        dest = jnp.sum(jnp.where(onehot, base, 0), axis=-1) + ranks
        dests.append(dest.T.reshape(n // blk, 1, 2 * blk))
        base = base + count
    slots = 2 * sum(slab.shape[1] for slab in slabs) + N_EXPERTS * tm
    n_tiles = slots // tm
    n_used = (pend[-1] // tm).astype(I32)
    tile_start = jnp.arange(n_tiles, dtype=I32) * tm
    tile_e = jnp.sum((pend[None, :] <= tile_start[:, None]).astype(I32), axis=1)
    tile_e = jnp.minimum(tile_e, N_EXPERTS - 1)
    last_e = jnp.sum(jnp.where(jnp.arange(n_tiles) == n_used - 1, tile_e, 0))
    tile_e = jnp.where(jnp.arange(n_tiles) < n_used, tile_e, last_e).astype(I32)
    return dests, tile_e, n_used.reshape(1), slots, (pstart + total).astype(I32), pend.astype(I32)


def _dispatch_body(dest_ref, xn_ref, *rest, blk):
    xs_hbm, sem = rest[-2:]

    def issue(i, carry):
        src = xn_ref.at[pl.ds(pl.multiple_of(i * ROW_TILES, ROW_TILES), ROW_TILES)]
        pltpu.make_async_copy(src, xs_hbm.at[dest_ref[0, 0, 2 * i]], sem).start()
        pltpu.make_async_copy(src, xs_hbm.at[dest_ref[0, 0, 2 * i + 1]], sem).start(priority=1)
        return carry

    lax.fori_loop(0, blk, issue, 0, unroll=8)

    def drain(i, carry):
        pltpu.make_async_copy(xn_ref.at[pl.ds(0, ROW_TILES)], xs_hbm.at[0], sem).wait()
        return carry

    lax.fori_loop(0, 2 * blk, drain, 0, unroll=8)


def _dispatch(xn, dest, slots, blk, xs=None):
    n = xn.shape[0] // ROW_TILES
    in_specs = [pl.BlockSpec((1, 1, 2 * blk), lambda i: (i, 0, 0), memory_space=pltpu.SMEM),
                pl.BlockSpec((blk * ROW_TILES, LANES), lambda i: (i, 0))]
    args = [dest, xn]
    if xs is not None:
        in_specs.append(pl.BlockSpec(memory_space=pl.ANY))
        args.append(xs)
    return pl.pallas_call(
        functools.partial(_dispatch_body, blk=blk), grid=(n // blk,), in_specs=in_specs,
        out_specs=pl.BlockSpec(memory_space=pl.ANY),
        out_shape=jax.ShapeDtypeStruct((slots, ROW_TILES, LANES), F32),
        scratch_shapes=[pltpu.SemaphoreType.DMA(())],
        input_output_aliases={} if xs is None else {2: 0},
        compiler_params=_params(("arbitrary",)), name="dispatch")(*args)


def _pad_zero_body(lo_ref, hi_ref, xs_in_hbm, xs_hbm, z_ref, sem):
    del xs_in_hbm
    z_ref[...] = jnp.zeros_like(z_ref)

    def per_expert(e, total):
        def issue(r, carry):
            pltpu.make_async_copy(z_ref, xs_hbm.at[r], sem).start()
            return carry

        lax.fori_loop(lo_ref[e], hi_ref[e], issue, 0)
        return total + hi_ref[e] - lo_ref[e]

    total = lax.fori_loop(0, N_EXPERTS, per_expert, 0)

    def drain(i, carry):
        pltpu.make_async_copy(z_ref, xs_hbm.at[0], sem).wait()
        return carry

    lax.fori_loop(0, total, drain, 0)


def _pad_zero(xs, lo, hi):
    grid_spec = pltpu.PrefetchScalarGridSpec(
        num_scalar_prefetch=2, grid=(1,),
        in_specs=[pl.BlockSpec(memory_space=pl.ANY)], out_specs=pl.BlockSpec(memory_space=pl.ANY),
        scratch_shapes=[pltpu.VMEM((ROW_TILES, LANES), F32), pltpu.SemaphoreType.DMA(())])
    return pl.pallas_call(
        _pad_zero_body, grid_spec=grid_spec, out_shape=jax.ShapeDtypeStruct(xs.shape, xs.dtype),
        input_output_aliases={2: 0},
        compiler_params=_params(("arbitrary",)), name="pad_zero")(lo, hi, xs)


def _moe_body(te_ref, nu_ref, xs_ref, wg_ref, wu_ref, wd_ref, o_ref, wgb_ref, wub_ref, wdb_ref, *, tm):
    t = pl.program_id(0)
    e = te_ref[t]
    prev = te_ref[jnp.maximum(t - 1, 0)]

    @pl.when((t == 0) | (e != prev))
    def _():
        wgb_ref[...] = wg_ref[0].astype(BF16)
        wub_ref[...] = wu_ref[0].astype(BF16)
        wdb_ref[...] = wd_ref[0].astype(BF16)

    @pl.when(t < nu_ref[0])
    def _():
        x = _tiles_to_rows(xs_ref, tm).astype(BF16)
        g = _dot(x, wgb_ref[...])
        u = _dot(x, wub_ref[...])
        hid = _silu(g) * u
        _rows_to_tiles(o_ref, _dot(hid.astype(BF16), wdb_ref[...]))


def _moe_experts(xs, tile_e, n_used, w_gate, w_up, w_down, layer, tm):
    slots = xs.shape[0] // ROW_TILES
    n_tiles = slots // tm
    row = lambda t, te, nu: (jnp.minimum(t, nu[0] - 1), 0)
    wsel = lambda t, te, nu: (layer, te[t], 0, 0)
    grid_spec = pltpu.PrefetchScalarGridSpec(
        num_scalar_prefetch=2, grid=(n_tiles,),
        in_specs=[pl.BlockSpec((tm * ROW_TILES, LANES), row),
                  pl.BlockSpec((None, 1, D_MODEL, D_EXPERT), wsel),
                  pl.BlockSpec((None, 1, D_MODEL, D_EXPERT), wsel),
                  pl.BlockSpec((None, 1, D_EXPERT, D_MODEL), wsel)],
        out_specs=pl.BlockSpec((tm * ROW_TILES, LANES), row),
        scratch_shapes=[pltpu.VMEM((D_MODEL, D_EXPERT), BF16), pltpu.VMEM((D_MODEL, D_EXPERT), BF16),
                        pltpu.VMEM((D_EXPERT, D_MODEL), BF16)])
    return pl.pallas_call(
        functools.partial(_moe_body, tm=tm), grid_spec=grid_spec,
        out_shape=jax.ShapeDtypeStruct((slots * ROW_TILES, LANES), F32),
        compiler_params=_params(("arbitrary",)), name="moe_experts")(tile_e, n_used, xs, w_gate, w_up, w_down)


def _ple_body(*refs, final, tm):
    dest_ref, x_ref, gw_ref, p_ref, g_ref, wg_ref, wp_ref = refs[:7]
    if final:
        fg_ref, ys_hbm, o_ref, f_ref, y0_ref, y1_ref, sem = refs[7:]
    else:
        ys_hbm, o_ref, y0_ref, y1_ref, sem = refs[7:]

    def issue(i, carry):
        row = pl.multiple_of(i * ROW_TILES, ROW_TILES)
        pltpu.make_async_copy(ys_hbm.at[dest_ref[0, 0, 2 * i]], y0_ref.at[pl.ds(row, ROW_TILES)], sem).start()
        pltpu.make_async_copy(ys_hbm.at[dest_ref[0, 0, 2 * i + 1]], y1_ref.at[pl.ds(row, ROW_TILES)], sem).start(priority=1)
        return carry

    lax.fori_loop(0, tm, issue, 0, unroll=8)
    pe = _dot(p_ref[...].astype(BF16), wp_ref[...])

    def drain(i, carry):
        pltpu.make_async_copy(ys_hbm.at[0], y0_ref.at[pl.ds(0, ROW_TILES)], sem).wait()
        return carry

    lax.fori_loop(0, 2 * tm, drain, 0, unroll=8)
    gw = gw_ref[...]
    x2 = x_ref[...] + gw[:, 2:3] * _tiles_to_rows(y0_ref, tm) + gw[:, 3:4] * _tiles_to_rows(y1_ref, tm)
    xn = _rms(x2, g_ref[...])
    gate = _sigmoid(_dot(xn.astype(BF16), wg_ref[...]))
    out = x2 + pe * gate
    o_ref[...] = out
    if final:
        f_ref[...] = _rms(out, fg_ref[...])


def _ple(x, ys, dest, gw, p, gain, w_gate, w_proj, layer, final_gain, tm):
    n = x.shape[0]
    dp = p.shape[-1]
    final = final_gain is not None
    rowspec = pl.BlockSpec((tm, D_MODEL), lambda i: (i, 0))
    vec = pl.BlockSpec((1, D_MODEL), lambda i: (0, 0))
    in_specs = [pl.BlockSpec((1, 1, 2 * tm), lambda i: (i, 0, 0), memory_space=pltpu.SMEM),
                rowspec, pl.BlockSpec((tm, 8), lambda i: (i, 0)),
                pl.BlockSpec((None, tm, dp), lambda i: (layer, i, 0)), vec,
                pl.BlockSpec((None, D_MODEL, D_MODEL), lambda i: (layer, 0, 0)),
                pl.BlockSpec((None, dp, D_MODEL), lambda i: (layer, 0, 0))]
    args = [dest, x, gw, p, gain.reshape(1, D_MODEL), w_gate, w_proj]
    out_shape = [jax.ShapeDtypeStruct((n, D_MODEL), F32)]
    out_specs = [rowspec]
    if final:
        in_specs.append(vec)
        args.append(final_gain.reshape(1, D_MODEL))
        out_shape.append(jax.ShapeDtypeStruct((n, D_MODEL), F32))
        out_specs.append(rowspec)
    in_specs.append(pl.BlockSpec(memory_space=pl.ANY))
    args.append(ys.reshape(-1, ROW_TILES, LANES))
    outs = pl.pallas_call(
        functools.partial(_ple_body, final=final, tm=tm), grid=(n // tm,), in_specs=in_specs,
        out_specs=out_specs, out_shape=out_shape,
        scratch_shapes=[pltpu.VMEM((tm * ROW_TILES, LANES), F32), pltpu.VMEM((tm * ROW_TILES, LANES), F32),
                        pltpu.SemaphoreType.DMA(())],
        compiler_params=_params(("arbitrary",)), name="ple")(*args)
    return (outs[0], outs[1]) if final else (outs[0], None)


ATTN_GROUP = 4


def _attn_group(qs, kcats, vcats, biases):
    R = CHUNK
    s = jnp.concatenate([_dot_nt(q, kcat) * (HEAD_DIM ** -0.5) + bias
                         for q, kcat, bias in zip(qs, kcats, biases)], axis=0)
    m = jnp.max(s, axis=-1, keepdims=True)
    p = jnp.exp(s - m)
    l = jnp.sum(p, axis=-1, keepdims=True)
    pb = p.astype(BF16)
    inv = 1.0 / l
    lse = m + jnp.log(l)
    outs, lses = [], []
    for i, vcat in enumerate(vcats):
        sl = slice(i * R, (i + 1) * R)
        outs.append(_dot(pb[sl], vcat) * inv[sl])
        lses.append(jnp.broadcast_to(lse[sl], (R, LANES)))
    return outs, lses


def _attn_body(q0_ref, q1_ref, q2_ref, k_ref, v_ref, o_ref, kp_ref, vp_ref, og_ref, lg_ref, bias_ref):
    R = CHUNK
    row = _iota((R, 2 * R), 0)
    col = _iota((R, 2 * R), 1)
    first_key = jnp.where(pl.program_id(2) > 0, 0, R)
    bias_ref[0] = jnp.where((col >= row) & (col <= row + R), 0.0, -jnp.inf)
    bias_ref[1] = jnp.where((col >= jnp.maximum(row, first_key)) & (col <= row + R), 0.0, -jnp.inf)

    @pl.when(pl.program_id(2) == 0)
    def _():
        kp_ref[...] = jnp.zeros_like(kp_ref)
        vp_ref[...] = jnp.zeros_like(vp_ref)

    def rows(ref, start, dil):
        idx = pl.ds(start, R) if dil == 1 else pl.ds(start, R, stride=dil)
        return ref[idx, :]

    def put(ref, start, dil, val):
        idx = pl.ds(start, R) if dil == 1 else pl.ds(start, R, stride=dil)
        ref[idx, :] = val

    def group(g, q_ref, dil, blocks):
        nblk = ATTN_SPAN // (R * dil)
        qs, kcats, vcats, fks, starts = [], [], [], [], []
        for r, cb in blocks:
            start = r + dil * R * cb
            if cb == 0:
                prev_k, prev_v, prev_start, fk = kp_ref, vp_ref, r + dil * R * (nblk - 1), bias_ref[1]
            else:
                prev_k, prev_v, prev_start, fk = k_ref, v_ref, start - dil * R, bias_ref[0]
            kcats.append(jnp.concatenate([rows(prev_k, prev_start, dil).astype(BF16),
                                          rows(k_ref, start, dil).astype(BF16)], axis=0))
            vcats.append(jnp.concatenate([rows(prev_v, prev_start, dil).astype(BF16),
                                          rows(v_ref, start, dil).astype(BF16)], axis=0))
            qs.append(rows(q_ref, start, dil).astype(BF16))
            fks.append(fk)
            starts.append(start)
        outs, lses = _attn_group(qs, kcats, vcats, fks)
        for start, o, lse in zip(starts, outs, lses):
            put(og_ref.at[g], start, dil, o)
            put(lg_ref.at[g], start, dil, lse)

    for g, (q_ref, (_, dil)) in enumerate(zip((q0_ref, q1_ref, q2_ref), PATTERNS)):
        nblk = ATTN_SPAN // (R * dil)
        if nblk >= ATTN_GROUP:
            def lane(r, carry, q_ref=q_ref, dil=dil, nblk=nblk, g=g):
                for c0 in range(0, nblk, ATTN_GROUP):
                    group(g, q_ref, dil, [(r, cb) for cb in range(c0, c0 + ATTN_GROUP)])
                return carry
            n_iter = dil
        else:
            def lane(i, carry, q_ref=q_ref, dil=dil, nblk=nblk, g=g):
                for cb in range(nblk):
                    group(g, q_ref, dil, [(i * ATTN_GROUP + j, cb) for j in range(ATTN_GROUP)])
                return carry
            n_iter = dil // ATTN_GROUP
        if n_iter == 1:
            lane(0, 0)
        else:
            lax.fori_loop(0, n_iter, lane, 0)

    step = 256
    for i in range(ATTN_SPAN // step):
        sl = pl.ds(i * step, step)
        l0 = lg_ref[0, sl, :]
        l1 = lg_ref[1, sl, :]
        l2 = lg_ref[2, sl, :]
        m = jnp.maximum(jnp.maximum(l0, l1), l2)
        e0 = jnp.exp(l0 - m)
        e1 = jnp.exp(l1 - m)
        e2 = jnp.exp(l2 - m)
        num = e0 * og_ref[0, sl, :] + e1 * og_ref[1, sl, :] + e2 * og_ref[2, sl, :]
        o_ref[sl, :] = num * (1.0 / (e0 + e1 + e2))
    kp_ref[...] = k_ref[...]
    vp_ref[...] = v_ref[...]


def _attn_prompt(q, k, v, b, s):
    nspan = s // ATTN_SPAN
    blk = (ATTN_SPAN, HEAD_DIM)
    hblk = (None,) + blk
    qspec = lambda g: pl.BlockSpec(hblk, lambda bi, h, c: (g * N_KV_HEADS + h, bi * nspan + c, 0))
    kvspec = pl.BlockSpec(hblk, lambda bi, h, c: (h, bi * nspan + c, 0))
    return pl.pallas_call(
        _attn_body, grid=(b, N_KV_HEADS, nspan),
        in_specs=[qspec(0), qspec(1), qspec(2), kvspec, kvspec],
        out_specs=kvspec, out_shape=jax.ShapeDtypeStruct((N_KV_HEADS, b * s, HEAD_DIM), F32),
        scratch_shapes=[pltpu.VMEM(blk, F32), pltpu.VMEM(blk, F32),
                        pltpu.VMEM((3,) + blk, F32), pltpu.VMEM((3,) + blk, F32),
                        pltpu.VMEM((2, CHUNK, 2 * CHUNK), F32)],
        compiler_params=_params(("parallel", "parallel", "arbitrary")), name="attn_prompt")(q, q, q, k, v)


SAMPLE_ROWS = 16
DIL_MAX = PATTERNS[-1][1]
DENSE_TAIL = PATTERNS[1][0]


def _attn_sample_body(q_ref, kf_ref, vf_ref, kt_ref, vt_ref, kn_ref, vn_ref, bias_ref, o_ref, *, t_new):
    nq = N_KV_HEADS * SAMPLE_ROWS
    n_far = kf_ref.shape[1] * kf_ref.shape[2]
    n_tail = kt_ref.shape[1]
    q = q_ref[0].astype(BF16)
    kf = kf_ref[0].reshape(n_far, HEAD_DIM).astype(BF16)
    vf = vf_ref[0].reshape(n_far, HEAD_DIM).astype(BF16)
    s = jnp.concatenate([_dot_nt(q, kf), _dot_nt(q, kt_ref[0].astype(BF16)), _dot_nt(q, kn_ref[0].astype(BF16))],
                        axis=1) * (HEAD_DIM ** -0.5) + bias_ref[...]
    m = jnp.max(s, axis=-1, keepdims=True)
    p = jnp.exp(s - m)
    l = jnp.sum(p, axis=-1, keepdims=True)
    pb = p.astype(BF16)
    o = (_dot(pb[:, :n_far], vf) + _dot(pb[:, n_far:n_far + n_tail], vt_ref[0].astype(BF16))
         + _dot(pb[:, n_far + n_tail:], vn_ref[0].astype(BF16))) * (1.0 / l)
    lse = jnp.broadcast_to(m + jnp.log(l), (nq, LANES))
    lse1 = pltpu.roll(lse, nq - t_new, 0)
    lse2 = pltpu.roll(lse, nq - 2 * t_new, 0)
    o1 = pltpu.roll(o, nq - t_new, 0)
    o2 = pltpu.roll(o, nq - 2 * t_new, 0)
    mm = jnp.maximum(jnp.maximum(lse, lse1), lse2)
    e0 = jnp.exp(lse - mm)
    e1 = jnp.exp(lse1 - mm)
    e2 = jnp.exp(lse2 - mm)
    comb = (e0 * o + e1 * o1 + e2 * o2) * (1.0 / (e0 + e1 + e2))
    for h in range(N_KV_HEADS):
        o_ref[0, :, h * HEAD_DIM:(h + 1) * HEAD_DIM] = comb[h * SAMPLE_ROWS:h * SAMPLE_ROWS + t_new]


def _sample_bias(wb, t_new, far):
    rho = jnp.arange(N_KV_HEADS * SAMPLE_ROWS, dtype=I32)
    h = rho // SAMPLE_ROWS
    gt = rho % SAMPLE_ROWS
    g = jnp.where(gt < len(PATTERNS) * t_new, gt // t_new, 0)
    t = gt % t_new
    win = jnp.asarray([w for w, _ in PATTERNS], I32)[g]
    dil = jnp.asarray([d for _, d in PATTERNS], I32)[g]
    far_rows = t_new * N_KV_HEADS
    c = jnp.arange(far // DIL_MAX * far_rows, dtype=I32)
    pos_far = DIL_MAX * (c // far_rows) + (c % far_rows) // N_KV_HEADS
    c = jnp.arange((wb - far) * N_KV_HEADS, dtype=I32)
    pos_tail = far + c // N_KV_HEADS
    c = jnp.arange(LANES, dtype=I32)
    pos_new = jnp.where(c < far_rows, wb + c // N_KV_HEADS, -1)
    pos = jnp.concatenate([pos_far, pos_tail, pos_new])
    head = jnp.concatenate([jnp.arange(n, dtype=I32) % N_KV_HEADS for n in (pos_far.shape[0], pos_tail.shape[0], LANES)])
    dist = wb + t[:, None] - pos[None, :]
    valid = ((head[None, :] == h[:, None]) & (pos[None, :] >= 0) & (dist >= 0) & (dist <= win[:, None])
             & (dist % dil[:, None] == 0))
    return jnp.where(valid, 0.0, -jnp.inf).astype(F32)


def _attn_sample(q, cache_k, cache_v, k_new, v_new, b, t_new):
    wb = cache_k.shape[1]
    tail = min(DENSE_TAIL, wb)
    far = wb - tail
    far_rows = t_new * N_KV_HEADS
    nq = N_KV_HEADS * SAMPLE_ROWS
    qh = q.reshape(b, t_new, 3, N_KV_HEADS, HEAD_DIM).transpose(0, 3, 2, 1, 4).reshape(b, N_KV_HEADS, 3 * t_new, HEAD_DIM)
    qh = jnp.pad(qh, ((0, 0), (0, 0), (0, SAMPLE_ROWS - 3 * t_new), (0, 0))).reshape(b, nq, HEAD_DIM)
    new = lambda x: jnp.pad(x.reshape(b, far_rows, HEAD_DIM), ((0, 0), (0, LANES - far_rows), (0, 0)))
    grouped = lambda x: x.reshape(b, wb // DIL_MAX, DIL_MAX * N_KV_HEADS, HEAD_DIM)
    flat = lambda x: x.reshape(b, wb * N_KV_HEADS, HEAD_DIM)
    bias = _sample_bias(wb, t_new, far)
    far_spec = pl.BlockSpec((1, far // DIL_MAX, far_rows, HEAD_DIM), lambda bi: (bi, 0, 0, 0))
    tail_spec = pl.BlockSpec((1, tail * N_KV_HEADS, HEAD_DIM), lambda bi: (bi, wb // tail - 1, 0))
    sq_spec = pl.BlockSpec((1, LANES, HEAD_DIM), lambda bi: (bi, 0, 0))
    o = pl.pallas_call(
        functools.partial(_attn_sample_body, t_new=t_new), grid=(b,),
        in_specs=[sq_spec, far_spec, far_spec, tail_spec, tail_spec, sq_spec, sq_spec,
                  pl.BlockSpec(bias.shape, lambda bi: (0, 0))],
        out_specs=pl.BlockSpec((1, t_new, ATTN_DIM), lambda bi: (bi, 0, 0)),
        out_shape=jax.ShapeDtypeStruct((b, t_new, ATTN_DIM), F32),
        compiler_params=_params(("parallel",)), name="attn_sample")(
            qh, grouped(cache_k), grouped(cache_v), flat(cache_k), flat(cache_v), new(k_new), new(v_new), bias)
    return o.reshape(b * t_new, ATTN_DIM)


def _rope_tables(pos0, l, reps):
    inv_freq = ROPE_THETA ** (-jnp.arange(0, HEAD_DIM, 2, dtype=F32) / HEAD_DIM)
    pos = pos0 + jnp.arange(l, dtype=jnp.int32)
    ang = pos.astype(F32)[:, None] * inv_freq[None, :]
    cos = jnp.cos(ang)
    sin = jnp.sin(ang)
    cosf = jnp.concatenate([cos, cos], axis=-1)
    sinf = jnp.concatenate([-sin, sin], axis=-1)
    if reps > 1:
        cosf = jnp.tile(cosf, (reps, 1))
        sinf = jnp.tile(sinf, (reps, 1))
    return cosf, sinf


def _prep_weights(prm):
    w = {}
    w['in_proj'], w['in_proj_lo'] = _split2(prm['ssm_in_proj'])
    w['in_dt'], w['in_dt_lo'] = _split2(
        jnp.pad(prm['ssm_in_proj'][:, :, ZXBC_DIM:], ((0, 0), (0, 0), (0, LANES - SSM_HEADS))))
    w['out_proj'], w['out_proj_lo'] = _split2(prm['ssm_out_proj'])
    w['w_kv'] = prm['w_kv'].astype(BF16)
    w['w_q'] = prm['attn_w_q'].astype(BF16)
    w['w_o'] = prm['attn_w_o'].astype(BF16)
    w['ple_gate'] = prm['ple_w_gate'].astype(BF16)
    w['ple_proj'] = prm['ple_w_proj'].astype(BF16)
    wr = jnp.concatenate([jnp.swapaxes(prm['moe_w_rg'], 1, 2),
                          jnp.zeros((DEPTH, 8 - N_EXPERT_GROUPS, D_MODEL), F32),
                          jnp.swapaxes(prm['moe_w_re'], 1, 2)], axis=1)
    w['router'] = jnp.stack(_split3(wr), axis=1)
    w['router_b'] = jnp.concatenate([prm['moe_b_rg'], jnp.zeros((DEPTH, 8 - N_EXPERT_GROUPS), F32),
                                     prm['moe_b_re']], axis=1).reshape(DEPTH, ROUTER_ROWS, 1)
    return w


def _trunk_state(x, p, pos0, conv_state, ssm_state, cache_k, cache_v):
    b, l, _ = x.shape
    n = b * l
    tm = min(512, n)
    return dict(b=b, l=l, n=n, tm=tm, sample=cache_k is not None, xf=x.reshape(n, D_MODEL),
                p=p.reshape(DEPTH, n, -1), rope=_rope_tables(pos0, l, 1 if l >= tm else tm // l),
                conv_state=conv_state, ssm_state=ssm_state, cache_k=cache_k, cache_v=cache_v,
                new_conv=[], new_ssm=[], k32=None, v32=None, kh=None, vh=None, y_final=None)


def _mixer(t, i, prm, w):
    b, l, n, tm, xf = t['b'], t['l'], t['n'], t['tm'], t['xf']
    gain = prm['norm_mix'][i]
    if i < N_A_LAYERS:
        precise = i < PRECISE_MIXERS
        lo = (lambda name: w[name + '_lo']) if precise else (lambda name: None)
        zx = _proj(xf, w['in_proj'], w_lo=lo('in_proj'), layer=i, gain=gain, nout=ZXBC_DIM, tm=tm, tn=1024)
        dtr = _proj(xf, w['in_dt'], w_lo=lo('in_dt'), layer=i, gain=gain, tm=tm, tn=LANES)
        y, cbuf, hs = _ssd(zx.reshape(b, l, ZXBC_DIM), dtr.reshape(b, l, LANES), prm['ssm_conv_w'][i],
                           prm['ssm_conv_b'][i], prm['ssm_dt_bias'][i], prm['ssm_a_log'][i], prm['ssm_d'][i],
                           prm['ssm_norm_y'][i], t['conv_state'][i], t['ssm_state'], i, precise)
        t['new_conv'].append(cbuf)
        t['new_ssm'].append(hs)
        t['xf'] = _proj(y.reshape(n, D_INNER), w['out_proj'], w_lo=lo('out_proj'), layer=i, residual=xf,
                        tm=tm, tn=1024)
    else:
        j = i - N_A_LAYERS
        q = _proj(xf, w['w_q'], layer=j, gain=gain, rope=t['rope'], out_heads=(not t['sample'],), tm=tm, tn=1024)
        if t['sample']:
            att = _attn_sample(q, t['cache_k'], t['cache_v'], t['k32'], t['v32'], b, l)
        else:
            att = _attn_prompt(q, t['kh'], t['vh'], b, l)
        t['xf'] = _proj(att, w['w_o'], layer=j, residual=xf, tm=tm, tn=1024)


def _shared_kv(t, prm, w):
    tm, xf = t['tm'], t['xf']
    kv_out = dict(out_dtypes=(F32,), out_heads=(False,)) if t['sample'] else \
        dict(out_dtypes=(F32, F32), out_heads=(False, True))
    k_outs = _proj(xf, w['w_kv'], gain=prm['norm_kv'], rope=t['rope'], nout=ATTN_DIM, tm=tm, tn=1024, **kv_out)
    v_outs = _proj(xf, w['w_kv'], col0=ATTN_DIM, gain=prm['norm_kv'], nout=ATTN_DIM, tm=tm, tn=1024, **kv_out)
    (t['k32'], t['kh']), (t['v32'], t['vh']) = ((k_outs, None), (v_outs, None)) if t['sample'] else (k_outs, v_outs)


def _moe_ple(trunks, i, prm, w):
    routed = [_router(t['xf'], prm['norm_ffn'][i], w['router'], w['router_b'], i, t['tm']) for t in trunks]
    dests, tile_e, n_used, slots, pad_lo, pad_hi = _moe_plan([r[1] for r in routed], [r[2] for r in routed],
                                                             MOE_TILE, [t['tm'] for t in trunks])
    xs = None
    for t, (xn, _, _), dest in zip(trunks, routed, dests):
        xs = _dispatch(xn, dest, slots, t['tm'], xs)
    xs = _pad_zero(xs, pad_lo, pad_hi)
    ys = _moe_experts(xs.reshape(slots * ROW_TILES, LANES), tile_e, n_used, prm['moe_w_gate'], prm['moe_w_up'],
                      prm['moe_w_down'], i, MOE_TILE)
    for t, (_, slab, _), dest in zip(trunks, routed, dests):
        t['xf'], t['y_final'] = _ple(t['xf'], ys, dest, slab.T, t['p'], prm['ple_norm'][i], w['ple_gate'],
                                     w['ple_proj'], i, prm['norm_final'] if i == DEPTH - 1 else None, t['tm'])


def _run(trunks, prm, w):
    for i in range(DEPTH):
        for t in trunks:
            _mixer(t, i, prm, w)
        _moe_ple(trunks, i, prm, w)
        if i == N_A_LAYERS - 1:
            for t in trunks:
                _shared_kv(t, prm, w)
    return [(t['y_final'].reshape(t['b'], t['l'], D_MODEL), jnp.stack(t['new_conv']), jnp.stack(t['new_ssm']),
             t['k32'].reshape(t['b'], t['l'], ATTN_DIM), t['v32'].reshape(t['b'], t['l'], ATTN_DIM))
            for t in trunks]


def kernel(x_prompt, x_sample, state_ssm, state_conv, cache_k, cache_v, p_prompt, p_sample, norm_mix, ssm_in_proj, ssm_conv_w, ssm_conv_b, ssm_dt_bias, ssm_a_log, ssm_d, ssm_norm_y, ssm_out_proj, norm_kv, w_kv, attn_w_q, attn_w_o, norm_ffn, moe_w_rg, moe_b_rg, moe_w_re, moe_b_re, moe_w_gate, moe_w_up, moe_w_down, ple_w_proj, ple_w_gate, ple_norm, norm_final):
    prm = dict(norm_mix=norm_mix, ssm_in_proj=ssm_in_proj, ssm_conv_w=ssm_conv_w, ssm_conv_b=ssm_conv_b,
               ssm_dt_bias=ssm_dt_bias, ssm_a_log=ssm_a_log, ssm_d=ssm_d, ssm_norm_y=ssm_norm_y,
               ssm_out_proj=ssm_out_proj, norm_kv=norm_kv, w_kv=w_kv, attn_w_q=attn_w_q, attn_w_o=attn_w_o,
               norm_ffn=norm_ffn, moe_w_rg=moe_w_rg, moe_b_rg=moe_b_rg, moe_w_re=moe_w_re, moe_b_re=moe_b_re,
               moe_w_gate=moe_w_gate, moe_w_up=moe_w_up, moe_w_down=moe_w_down, ple_w_proj=ple_w_proj,
               ple_w_gate=ple_w_gate, ple_norm=ple_norm, norm_final=norm_final)
    w = _prep_weights(prm)
    b, s, _ = x_prompt.shape
    db, dl, _ = x_sample.shape
    past_len = 8192
    conv0 = jnp.zeros((N_A_LAYERS, b, 3, CONV_DIM), x_prompt.dtype)
    ssm0 = jnp.zeros((N_A_LAYERS, b, SSM_HEADS, SSM_HEADDIM, D_STATE), state_ssm.dtype)
    trunks = [_trunk_state(x_prompt, p_prompt, 0, conv0, ssm0, None, None),
              _trunk_state(x_sample, p_sample, past_len, state_conv, state_ssm, cache_k, cache_v)]
    (y_p, conv_p, ssm_p, k_p, v_p), (y_s, conv_s, ssm_s, k_s, v_s) = _run(trunks, prm, w)
    keep = min(2048, s)
    heads = lambda t: t.reshape(t.shape[0], t.shape[1], N_KV_HEADS, HEAD_DIM)
    k_p = heads(k_p[:, s - keep:])
    v_p = heads(v_p[:, s - keep:])
    return (y_p, y_s, ssm_p, conv_p, k_p, v_p, ssm_s, conv_s, heads(k_s), heads(v_s))
```

```python
import functools

import jax
import jax.numpy as jnp
from jax import lax
from jax.experimental import pallas as pl
from jax.experimental.pallas import tpu as pltpu

F32 = jnp.float32
BF16 = jnp.bfloat16
I32 = jnp.int32

EPS = 1e-6
D_MODEL = 1024
D_INNER = 2048
SSM_HEADS = 32
SSM_HEADDIM = 64
SSM_GROUPS = 4
D_STATE = 128
CONV_DIM = 3072
ZXBC_DIM = D_INNER + CONV_DIM
CHUNK = 128
HEAD_DIM = 128
N_KV_HEADS = 8
ATTN_DIM = 1024
PATTERNS = ((128, 1), (512, 4), (2048, 16))
ATTN_SPAN = 2048
ROPE_THETA = 10000.0
N_EXPERTS = 32
N_EXPERT_GROUPS = 4
EXPERTS_PER_GROUP = 8
D_EXPERT = 512
N_A_LAYERS = 2
DEPTH = 4
PRECISE_MIXERS = 1
MOE_TILE = 256
ROUTER_ROWS = 8 + N_EXPERTS
LANES = 128
SUBLANES = 8
ROW_TILES = D_MODEL // LANES
VMEM_LIMIT = 56 * 1024 * 1024


def _params(sem):
    return pltpu.CompilerParams(dimension_semantics=sem, vmem_limit_bytes=VMEM_LIMIT)


def _iota(shape, dim):
    return lax.broadcasted_iota(I32, shape, dim)


def _rms(x, gain):
    ms = jnp.mean(x * x, axis=-1, keepdims=True)
    return x * lax.rsqrt(ms + EPS) * gain


def _sigmoid(x):
    return 0.5 * jnp.tanh(0.5 * x) + 0.5


def _silu(x):
    return x * _sigmoid(x)


def _dot(a, b):
    return jnp.dot(a, b, preferred_element_type=F32)


def _dot_nt(a, b):
    return lax.dot_general(a, b, (((1,), (1,)), ((), ())), preferred_element_type=F32)


def _split3(x):
    hi = x.astype(BF16)
    r = x - hi.astype(F32)
    mid = r.astype(BF16)
    lo = (r - mid.astype(F32)).astype(BF16)
    return hi, mid, lo


def _split2(x):
    hi = x.astype(BF16)
    return hi, (x - hi.astype(F32)).astype(BF16)


def _mm(a, b, dot):
    y = dot(a[0], b[0])
    if len(a) > 1:
        y = y + dot(a[1], b[0]) + dot(a[0], b[1])
    return y


def _rows_to_tiles(ref, x):
    m = x.shape[0]
    for s in range(ROW_TILES):
        ref[pl.ds(s, m, stride=ROW_TILES), :] = x[:, s * LANES:(s + 1) * LANES]


def _tiles_to_rows(ref, m):
    return jnp.concatenate([ref[pl.ds(s, m, stride=ROW_TILES), :] for s in range(ROW_TILES)], axis=1)


def _proj_body(*refs, has_norm, has_rope, has_res, x_heads, out_heads, precise):
    it = iter(refs)
    x_ref = next(it)
    w_ref = next(it)
    wlo_ref = next(it) if precise else None
    g_ref = next(it) if has_norm else None
    cos_ref = next(it) if has_rope else None
    sin_ref = next(it) if has_rope else None
    r_ref = next(it) if has_res else None
    o_refs = [next(it) for _ in out_heads]
    xb_ref = next(it)
    xlo_ref = next(it) if precise else None

    @pl.when(pl.program_id(1) == 0)
    def _():
        if x_heads:
            x = jnp.concatenate([x_ref[h] for h in range(x_ref.shape[0])], axis=1).astype(F32)
        else:
            x = x_ref[...].astype(F32)
        if has_norm:
            x = _rms(x, g_ref[...])
        if precise:
            xb_ref[...], xlo_ref[...] = _split2(x)
        else:
            xb_ref[...] = x.astype(BF16)

    if precise:
        y = _mm((xb_ref[...], xlo_ref[...]), (w_ref[...], wlo_ref[...]), _dot)
    else:
        y = _dot(xb_ref[...], w_ref[...])
    if has_rope:
        cos = cos_ref[...]
        sin = sin_ref[...]
        parts = []
        for h in range(y.shape[1] // HEAD_DIM):
            t = y[:, h * HEAD_DIM:(h + 1) * HEAD_DIM]
            parts.append(t * cos + pltpu.roll(t, HEAD_DIM // 2, 1) * sin)
        y = parts[0] if len(parts) == 1 else jnp.concatenate(parts, axis=1)
    if has_res:
        y = y + r_ref[...]
    for o, heads in zip(o_refs, out_heads):
        if heads:
            for h in range(o.shape[0]):
                o[h] = y[:, h * HEAD_DIM:(h + 1) * HEAD_DIM].astype(o.dtype)
        else:
            o[...] = y.astype(o.dtype)


def _proj(x, w, *, w_lo=None, layer=None, col0=0, gain=None, rope=None, residual=None, out_dtypes=(F32,),
          out_heads=None, nout=None, tm, tn):
    x_heads = x.ndim == 3
    n = x.shape[-2]
    k = x.shape[0] * x.shape[2] if x_heads else x.shape[1]
    nout = w.shape[-1] if nout is None else nout
    tn = min(tn, nout)
    out_heads = (False,) * len(out_dtypes) if out_heads is None else out_heads
    grid = (n // tm, nout // tn)
    jb = col0 // tn
    precise = w_lo is not None
    wspec = (pl.BlockSpec((k, tn), lambda i, j: (0, j + jb)) if layer is None else
             pl.BlockSpec((None, k, tn), lambda i, j: (layer, 0, j + jb)))
    in_specs = [pl.BlockSpec((k // HEAD_DIM, tm, HEAD_DIM), lambda i, j: (0, i, 0)) if x_heads else
                pl.BlockSpec((tm, k), lambda i, j: (i, 0)), wspec]
    args = [x, w]
    if precise:
        in_specs.append(wspec)
        args.append(w_lo)
    if gain is not None:
        in_specs.append(pl.BlockSpec((1, k), lambda i, j: (0, 0)))
        args.append(gain.reshape(1, k))
    if rope is not None:
        cos, sin = rope
        nblk = cos.shape[0] // tm
        for t in (cos, sin):
            in_specs.append(pl.BlockSpec((tm, HEAD_DIM), lambda i, j: (i % nblk, 0)))
            args.append(t)
    if residual is not None:
        in_specs.append(pl.BlockSpec((tm, tn), lambda i, j: (i, j)))
        args.append(residual)
    out_shape = [jax.ShapeDtypeStruct((nout // HEAD_DIM, n, HEAD_DIM) if hm else (n, nout), dt)
                 for dt, hm in zip(out_dtypes, out_heads)]
    out_specs = [pl.BlockSpec((tn // HEAD_DIM, tm, HEAD_DIM), lambda i, j: (j, i, 0)) if hm else
                 pl.BlockSpec((tm, tn), lambda i, j: (i, j)) for hm in out_heads]
    body = functools.partial(_proj_body, has_norm=gain is not None, has_rope=rope is not None,
                             has_res=residual is not None, x_heads=x_heads, out_heads=tuple(out_heads),
                             precise=precise)
    outs = pl.pallas_call(
        body, grid=grid, in_specs=in_specs, out_specs=out_specs, out_shape=out_shape,
        scratch_shapes=[pltpu.VMEM((tm, k), BF16)] * (2 if precise else 1),
        compiler_params=_params(("parallel", "arbitrary")), name="proj")(*args)
    return outs[0] if len(outs) == 1 else outs


def _ssd_body(zx_ref, dt_ref, cw_ref, cb_ref, dtb_ref, alog_ref, d_ref, ny_ref, conv0_ref, ssm0_ref,
              y_ref, convo_ref, ssmo_ref, st_ref, tail_ref, xbc_ref, yacc_ref, *pad_refs, valid, nchunks, precise):
    L = CHUNK
    c = pl.program_id(1)
    npair = SSM_HEADS // 2
    parts = _split2 if precise else (lambda x: (x.astype(BF16),))

    @pl.when(c == 0)
    def _():
        for j in range(npair):
            st_ref[:, j * LANES:(j + 1) * LANES] = ssm0_ref[0, j * LANES:(j + 1) * LANES, :].T
        tail_ref[0:8, :] = conv0_ref[0]

    if valid < L:
        pzx_ref, pdt_ref = pad_refs
        pzx_ref[...] = jnp.zeros_like(pzx_ref)
        pdt_ref[...] = jnp.zeros_like(pdt_ref)
        pzx_ref[0:valid, :] = zx_ref[0]
        pdt_ref[0:valid, :] = dt_ref[0]
        z = pzx_ref[:, :D_INNER]
        u = pzx_ref[:, D_INNER:]
        dtr = pdt_ref[...]
    else:
        z = zx_ref[0, :, :D_INNER]
        u = zx_ref[0, :, D_INNER:]
        dtr = dt_ref[0]

    tail_ref[8:8 + L, :] = u
    acc = cb_ref[...] + cw_ref[3:4, :] * u
    for k in range(1, 4):
        acc = acc + cw_ref[3 - k:4 - k, :] * tail_ref[8 - k:8 - k + L, :]
    tail_ref[0:8, :] = tail_ref[valid:valid + 8, :]
    xbc_ref[...] = _silu(acc)

    lane = _iota((1, LANES), 1)
    rowl = _iota((L, 1), 0)
    xdt = dtr + dtb_ref[...]
    dt = jnp.maximum(xdt, 0.0) + jnp.log(1.0 + jnp.exp(-jnp.abs(xdt)))
    dt = jnp.where((lane < SSM_HEADS) & (rowl < valid), dt, 0.0)
    a_neg = -jnp.exp(alog_ref[...])
    da = dt * a_neg
    rr = _iota((L, L), 0)
    cc = _iota((L, L), 1)
    causal = rr >= cc
    tri = jnp.where(causal, 1.0, 0.0).astype(BF16)
    hi, mid, lo = _split3(da)
    a_cs = _dot(tri, hi) + _dot(tri, mid) + _dot(tri, lo)
    a_cs_t = a_cs.T
    dt_t = dt.T
    tdt_t = dt_t * jnp.exp(a_cs_t[:, L - 1:L] - a_cs_t)
    ea = jnp.exp(a_cs)
    half = _iota((1, LANES), 1) < SSM_HEADDIM

    for g in range(SSM_GROUPS):
        b_g = xbc_ref[:, D_INNER + g * D_STATE:D_INNER + (g + 1) * D_STATE]
        c_g = xbc_ref[:, D_INNER + SSM_GROUPS * D_STATE + g * D_STATE:
                      D_INNER + SSM_GROUPS * D_STATE + (g + 1) * D_STATE]
        cb = _mm(parts(c_g), parts(b_g), _dot_nt)
        b_t = b_g.T
        for jj in range(npair // SSM_GROUPS):
            j = g * (npair // SSM_GROUPS) + jj
            cols = slice(j * LANES, (j + 1) * LANES)
            xs_b = parts(xbc_ref[:, cols])
            st_p = st_ref[:, cols]
            rhs = tuple(jnp.concatenate([a, b], axis=0) for a, b in zip(xs_b, parts(st_p)))
            ys, ss, cds = [], [], []
            for h in (2 * j, 2 * j + 1):
                colb = jnp.broadcast_to(a_cs[:, h:h + 1], (L, LANES))
                seg = colb - a_cs_t[h:h + 1, :]
                dec = jnp.exp(jnp.where(causal, seg, -jnp.inf))
                m = cb * dec * dt_t[h:h + 1, :]
                ecol = jnp.broadcast_to(ea[:, h:h + 1], (L, LANES))
                ce = c_g * ecol
                lhs = tuple(jnp.concatenate([a, b], axis=1) for a, b in zip(parts(m), parts(ce)))
                ys.append(_mm(lhs, rhs, _dot))
                ss.append(_mm(parts(b_t * tdt_t[h:h + 1, :]), xs_b, _dot))
                cds.append(ecol[L - 1:L, :])
            yacc_ref[:, cols] = jnp.where(half, ys[0], ys[1])
            st_ref[:, cols] = jnp.where(half, st_p * cds[0] + ss[0], st_p * cds[1] + ss[1])

    y = yacc_ref[...] + d_ref[...] * xbc_ref[:, :D_INNER]
    yn = _rms(y * _silu(z), ny_ref[...])
    if valid < L:
        y_ref[0] = yn[0:valid].astype(y_ref.dtype)
    else:
        y_ref[0] = yn.astype(y_ref.dtype)

    @pl.when(c == nchunks - 1)
    def _():
        convo_ref[0] = tail_ref[0:8, :]
        for j in range(npair):
            ssmo_ref[0, j * LANES:(j + 1) * LANES, :] = st_ref[:, j * LANES:(j + 1) * LANES].T


def _ssd(zx, dtr, conv_w, conv_b, dt_bias, a_log, d_skip, norm_y, conv0, ssm0, layer, precise):
    b, l, _ = zx.shape
    if l >= CHUNK:
        lb, valid, nchunks = CHUNK, CHUNK, l // CHUNK
    else:
        lb, valid, nchunks = l, l, 1
    pad_lanes = LANES - SSM_HEADS
    dtb = jnp.pad(dt_bias.astype(F32), (0, pad_lanes)).reshape(1, LANES)
    alog = jnp.pad(a_log.astype(F32), (0, pad_lanes)).reshape(1, LANES)
    d_exp = jnp.repeat(d_skip.astype(F32), SSM_HEADDIM).reshape(1, D_INNER)
    conv0p = jnp.pad(conv0, ((0, 0), (5, 0), (0, 0)))
    ssm0f = ssm0.reshape(ssm0.shape[0], b, SSM_HEADS * SSM_HEADDIM, D_STATE)
    const = lambda bi, c: (0, 0)
    in_specs = [
        pl.BlockSpec((1, lb, ZXBC_DIM), lambda bi, c: (bi, c, 0)),
        pl.BlockSpec((1, lb, LANES), lambda bi, c: (bi, c, 0)),
        pl.BlockSpec((4, CONV_DIM), const),
        pl.BlockSpec((1, CONV_DIM), const),
        pl.BlockSpec((1, LANES), const),
        pl.BlockSpec((1, LANES), const),
        pl.BlockSpec((1, D_INNER), const),
        pl.BlockSpec((1, D_INNER), const),
        pl.BlockSpec((1, 8, CONV_DIM), lambda bi, c: (bi, 0, 0)),
        pl.BlockSpec((None, 1, D_INNER, D_STATE), lambda bi, c: (layer, bi, 0, 0)),
    ]
    out_shape = [jax.ShapeDtypeStruct((b, l, D_INNER), F32 if precise else BF16),
                 jax.ShapeDtypeStruct((b, 8, CONV_DIM), F32),
                 jax.ShapeDtypeStruct((b, D_INNER, D_STATE), F32)]
    out_specs = [pl.BlockSpec((1, lb, D_INNER), lambda bi, c: (bi, c, 0)),
                 pl.BlockSpec((1, 8, CONV_DIM), lambda bi, c: (bi, 0, 0)),
                 pl.BlockSpec((1, D_INNER, D_STATE), lambda bi, c: (bi, 0, 0))]
    scratch = [pltpu.VMEM((D_STATE, D_INNER), F32), pltpu.VMEM((8 + CHUNK, CONV_DIM), F32),
               pltpu.VMEM((CHUNK, CONV_DIM), F32), pltpu.VMEM((CHUNK, D_INNER), F32)]
    if valid < CHUNK:
        scratch += [pltpu.VMEM((CHUNK, ZXBC_DIM), F32), pltpu.VMEM((CHUNK, LANES), F32)]
    y, convo, ssmo = pl.pallas_call(
        functools.partial(_ssd_body, valid=valid, nchunks=nchunks, precise=precise),
        grid=(b, nchunks), in_specs=in_specs, out_specs=out_specs, out_shape=out_shape,
        scratch_shapes=scratch, compiler_params=_params(("parallel", "arbitrary")), name="ssd")(
            zx, dtr, conv_w, conv_b.reshape(1, CONV_DIM), dtb, alog, d_exp, norm_y.reshape(1, D_INNER),
            conv0p, ssm0f)
    return y, convo[:, 5:], ssmo.reshape(b, SSM_HEADS, SSM_HEADDIM, D_STATE)


def _router_body(x_ref, g_ref, w_ref, b_ref, xn_ref, slab_ref, cnt_ref, base_ref, su_ref):
    tm = x_ref.shape[0]

    @pl.when(pl.program_id(0) == 0)
    def _():
        base_ref[...] = jnp.zeros_like(base_ref)
        su_ref[...] = jnp.where(_iota((tm, tm), 0) < _iota((tm, tm), 1), 1.0, 0.0).astype(BF16)

    xn = _rms(x_ref[...], g_ref[...])
    _rows_to_tiles(xn_ref, xn)
    xh, xm, xl = _split3(xn)
    wh = w_ref[0]
    wm = w_ref[1]
    wl = w_ref[2]
    logits = (_dot_nt(wh, xh) + _dot_nt(wh, xm) + _dot_nt(wm, xh)
              + _dot_nt(wh, xl) + _dot_nt(wm, xm) + _dot_nt(wl, xh)) + b_ref[...]
    gl = [logits[i:i + 1, :] for i in range(N_EXPERT_GROUPS)]
    best = gl[0]
    sel = jnp.zeros((1, tm), I32)
    for i in range(1, N_EXPERT_GROUPS):
        upd = gl[i] > best
        sel = jnp.where(upd, i, sel)
        best = jnp.where(upd, gl[i], best)
    den = jnp.exp(gl[0] - best)
    for i in range(1, N_EXPERT_GROUPS):
        den = den + jnp.exp(gl[i] - best)
    g_w = 1.0 / den
    chosen = logits[8:8 + EXPERTS_PER_GROUP, :]
    for i in range(1, N_EXPERT_GROUPS):
        chosen = jnp.where(sel == i, logits[8 + i * EXPERTS_PER_GROUP:8 + (i + 1) * EXPERTS_PER_GROUP, :], chosen)
    idx = _iota((EXPERTS_PER_GROUP, tm), 0)
    m1 = jnp.max(chosen, axis=0, keepdims=True)
    i1 = jnp.min(jnp.where(chosen == m1, idx, EXPERTS_PER_GROUP), axis=0, keepdims=True)
    rest = jnp.where(idx == i1, -jnp.inf, chosen)
    m2 = jnp.max(rest, axis=0, keepdims=True)
    i2 = jnp.min(jnp.where(rest == m2, idx, EXPERTS_PER_GROUP), axis=0, keepdims=True)
    t = jnp.exp(m2 - m1)
    inv = 1.0 / (1.0 + t)
    w0 = inv * g_w
    w1 = (t * inv) * g_w
    e0 = sel * EXPERTS_PER_GROUP + i1
    e1 = sel * EXPERTS_PER_GROUP + i2

    eidx = _iota((N_EXPERTS, tm), 0)
    oh0 = eidx == e0
    oh1 = eidx == e1
    oh = jnp.where(oh0, 1.0, jnp.where(oh1, 1.0, 0.0))
    base = base_ref[...]
    before = _dot(oh.astype(BF16), su_ref[...]) + jnp.concatenate([base] * (tm // LANES), axis=1)
    rank0 = jnp.sum(jnp.where(oh0, before, 0.0), axis=0, keepdims=True)
    rank1 = jnp.sum(jnp.where(oh1, before, 0.0), axis=0, keepdims=True)
    new_base = base + jnp.sum(oh, axis=1, keepdims=True)
    base_ref[...] = new_base
    cnt_ref[...] = new_base

    r = _iota((8, tm), 0)
    rows = (e0.astype(F32), e1.astype(F32), w0, w1, rank0, rank1)
    slab = jnp.zeros((8, tm), F32)
    for i, v in enumerate(rows):
        slab = jnp.where(r == i, v, slab)
    slab_ref[...] = slab


def _router(x, gain, w_split, bias, layer, tm):
    n = x.shape[0]
    xn, slab, cnt = pl.pallas_call(
        _router_body, grid=(n // tm,),
        in_specs=[pl.BlockSpec((tm, D_MODEL), lambda i: (i, 0)),
                  pl.BlockSpec((1, D_MODEL), lambda i: (0, 0)),
                  pl.BlockSpec((None, 3, ROUTER_ROWS, D_MODEL), lambda i: (layer, 0, 0, 0)),
                  pl.BlockSpec((None, ROUTER_ROWS, 1), lambda i: (layer, 0, 0))],
        out_specs=[pl.BlockSpec((tm * ROW_TILES, LANES), lambda i: (i, 0)),
                   pl.BlockSpec((8, tm), lambda i: (0, i)),
                   pl.BlockSpec((N_EXPERTS, LANES), lambda i: (0, 0))],
        out_shape=[jax.ShapeDtypeStruct((n * ROW_TILES, LANES), F32), jax.ShapeDtypeStruct((8, n), F32),
                   jax.ShapeDtypeStruct((N_EXPERTS, LANES), F32)],
        scratch_shapes=[pltpu.VMEM((N_EXPERTS, LANES), F32), pltpu.VMEM((tm, tm), BF16)],
        compiler_params=_params(("arbitrary",)), name="router")(x, gain.reshape(1, D_MODEL), w_split, bias)
    return xn, slab, cnt


def _moe_plan(slabs, cnts, tm, blks):
    counts = [cnt[:, 0].astype(I32) for cnt in cnts]
    total = sum(counts)
    padded = ((total + tm - 1) // tm) * tm
    pend = jnp.cumsum(padded)
    pstart = pend - padded
    dests = []
    base = pstart
    for slab, count, blk in zip(slabs, counts, blks):
        n = slab.shape[1]
        e_ids = slab[0:2].astype(I32)
        ranks = slab[4:6].astype(I32)
        onehot = e_ids[..., None] == jnp.arange(N_EXPERTS, dtype=I32)
        dest = jnp.sum(jnp.where(onehot, base, 0), axis=-1) + ranks
        dests.append(dest.T.reshape(n // blk, 1, 2 * blk))
        base = base + count
    slots = 2 * sum(slab.shape[1] for slab in slabs) + N_EXPERTS * tm
    n_tiles = slots // tm
    n_used = (pend[-1] // tm).astype(I32)
    tile_start = jnp.arange(n_tiles, dtype=I32) * tm
    tile_e = jnp.sum((pend[None, :] <= tile_start[:, None]).astype(I32), axis=1)
    tile_e = jnp.minimum(tile_e, N_EXPERTS - 1)
    last_e = jnp.sum(jnp.where(jnp.arange(n_tiles) == n_used - 1, tile_e, 0))
    tile_e = jnp.where(jnp.arange(n_tiles) < n_used, tile_e, last_e).astype(I32)
    return dests, tile_e, n_used.reshape(1), slots, (pstart + total).astype(I32), pend.astype(I32)


def _dispatch_body(dest_ref, xn_ref, *rest, blk):
    xs_hbm, sem = rest[-2:]

    def issue(i, carry):
        src = xn_ref.at[pl.ds(pl.multiple_of(i * ROW_TILES, ROW_TILES), ROW_TILES)]
        pltpu.make_async_copy(src, xs_hbm.at[dest_ref[0, 0, 2 * i]], sem).start()
        pltpu.make_async_copy(src, xs_hbm.at[dest_ref[0, 0, 2 * i + 1]], sem).start(priority=1)
        return carry

    lax.fori_loop(0, blk, issue, 0, unroll=8)

    def drain(i, carry):
        pltpu.make_async_copy(xn_ref.at[pl.ds(0, ROW_TILES)], xs_hbm.at[0], sem).wait()
        return carry

    lax.fori_loop(0, 2 * blk, drain, 0, unroll=8)


def _dispatch(xn, dest, slots, blk, xs=None):
    n = xn.shape[0] // ROW_TILES
    in_specs = [pl.BlockSpec((1, 1, 2 * blk), lambda i: (i, 0, 0), memory_space=pltpu.SMEM),
                pl.BlockSpec((blk * ROW_TILES, LANES), lambda i: (i, 0))]
    args = [dest, xn]
    if xs is not None:
        in_specs.append(pl.BlockSpec(memory_space=pl.ANY))
        args.append(xs)
    return pl.pallas_call(
        functools.partial(_dispatch_body, blk=blk), grid=(n // blk,), in_specs=in_specs,
        out_specs=pl.BlockSpec(memory_space=pl.ANY),
        out_shape=jax.ShapeDtypeStruct((slots, ROW_TILES, LANES), F32),
        scratch_shapes=[pltpu.SemaphoreType.DMA(())],
        input_output_aliases={} if xs is None else {2: 0},
        compiler_params=_params(("arbitrary",)), name="dispatch")(*args)


def _pad_zero_body(lo_ref, hi_ref, xs_in_hbm, xs_hbm, z_ref, sem):
    del xs_in_hbm
    z_ref[...] = jnp.zeros_like(z_ref)
    sizes = [1 << s for s in range(MOE_TILE.bit_length() - 2, -1, -1)]

    def copies(e, act):
        pos = lo_ref[e]
        n = hi_ref[e] - pos
        for size in sizes:
            take = (n & size) != 0

            @pl.when(take)
            def _(pos=pos, size=size):
                act(pltpu.make_async_copy(z_ref.at[pl.ds(0, size)], xs_hbm.at[pl.ds(pos, size)], sem))

            pos = pos + jnp.where(take, size, 0)

    def issue(e, carry):
        copies(e, lambda cp: cp.start())
        return carry

    def drain(e, carry):
        copies(e, lambda cp: cp.wait())
        return carry

    lax.fori_loop(0, N_EXPERTS, issue, 0)
    lax.fori_loop(0, N_EXPERTS, drain, 0)


def _pad_zero(xs, lo, hi):
    grid_spec = pltpu.PrefetchScalarGridSpec(
        num_scalar_prefetch=2, grid=(1,),
        in_specs=[pl.BlockSpec(memory_space=pl.ANY)], out_specs=pl.BlockSpec(memory_space=pl.ANY),
        scratch_shapes=[pltpu.VMEM((MOE_TILE // 2, ROW_TILES, LANES), F32), pltpu.SemaphoreType.DMA(())])
    return pl.pallas_call(
        _pad_zero_body, grid_spec=grid_spec, out_shape=jax.ShapeDtypeStruct(xs.shape, xs.dtype),
        input_output_aliases={2: 0},
        compiler_params=_params(("arbitrary",)), name="pad_zero")(lo, hi, xs)


def _moe_body(te_ref, nu_ref, xs_ref, wg_ref, wu_ref, wd_ref, o_ref, wgb_ref, wub_ref, wdb_ref, *, tm):
    t = pl.program_id(0)
    e = te_ref[t]
    prev = te_ref[jnp.maximum(t - 1, 0)]

    @pl.when((t == 0) | (e != prev))
    def _():
        wgb_ref[...] = wg_ref[0].astype(BF16)
        wub_ref[...] = wu_ref[0].astype(BF16)
        wdb_ref[...] = wd_ref[0].astype(BF16)

    @pl.when(t < nu_ref[0])
    def _():
        x = _tiles_to_rows(xs_ref, tm).astype(BF16)
        g = _dot(x, wgb_ref[...])
        u = _dot(x, wub_ref[...])
        hid = _silu(g) * u
        _rows_to_tiles(o_ref, _dot(hid.astype(BF16), wdb_ref[...]))


def _moe_experts(xs, tile_e, n_used, w_gate, w_up, w_down, layer, tm):
    slots = xs.shape[0] // ROW_TILES
    n_tiles = slots // tm
    row = lambda t, te, nu: (jnp.minimum(t, nu[0] - 1), 0)
    wsel = lambda t, te, nu: (layer, te[t], 0, 0)
    grid_spec = pltpu.PrefetchScalarGridSpec(
        num_scalar_prefetch=2, grid=(n_tiles,),
        in_specs=[pl.BlockSpec((tm * ROW_TILES, LANES), row),
                  pl.BlockSpec((None, 1, D_MODEL, D_EXPERT), wsel),
                  pl.BlockSpec((None, 1, D_MODEL, D_EXPERT), wsel),
                  pl.BlockSpec((None, 1, D_EXPERT, D_MODEL), wsel)],
        out_specs=pl.BlockSpec((tm * ROW_TILES, LANES), row),
        scratch_shapes=[pltpu.VMEM((D_MODEL, D_EXPERT), BF16), pltpu.VMEM((D_MODEL, D_EXPERT), BF16),
                        pltpu.VMEM((D_EXPERT, D_MODEL), BF16)])
    return pl.pallas_call(
        functools.partial(_moe_body, tm=tm), grid_spec=grid_spec,
        out_shape=jax.ShapeDtypeStruct((slots * ROW_TILES, LANES), F32),
        compiler_params=_params(("arbitrary",)), name="moe_experts")(tile_e, n_used, xs, w_gate, w_up, w_down)


def _ple_body(*refs, final, tm):
    dest_ref, x_ref, gw_ref, p_ref, g_ref, wg_ref, wp_ref = refs[:7]
    if final:
        fg_ref, ys_hbm, o_ref, f_ref, y0_ref, y1_ref, sem = refs[7:]
    else:
        ys_hbm, o_ref, y0_ref, y1_ref, sem = refs[7:]

    def issue(i, carry):
        row = pl.multiple_of(i * ROW_TILES, ROW_TILES)
        pltpu.make_async_copy(ys_hbm.at[dest_ref[0, 0, 2 * i]], y0_ref.at[pl.ds(row, ROW_TILES)], sem).start()
        pltpu.make_async_copy(ys_hbm.at[dest_ref[0, 0, 2 * i + 1]], y1_ref.at[pl.ds(row, ROW_TILES)], sem).start(priority=1)
        return carry

    lax.fori_loop(0, tm, issue, 0, unroll=8)
    pe = _dot(p_ref[...].astype(BF16), wp_ref[...])

    def drain(i, carry):
        pltpu.make_async_copy(ys_hbm.at[0], y0_ref.at[pl.ds(0, ROW_TILES)], sem).wait()
        return carry

    lax.fori_loop(0, 2 * tm, drain, 0, unroll=8)
    gw = gw_ref[...]
    x2 = x_ref[...] + gw[:, 2:3] * _tiles_to_rows(y0_ref, tm) + gw[:, 3:4] * _tiles_to_rows(y1_ref, tm)
    xn = _rms(x2, g_ref[...])
    gate = _sigmoid(_dot(xn.astype(BF16), wg_ref[...]))
    out = x2 + pe * gate
    o_ref[...] = out
    if final:
        f_ref[...] = _rms(out, fg_ref[...])


def _ple(x, ys, dest, gw, p, gain, w_gate, w_proj, layer, final_gain, tm):
    n = x.shape[0]
    dp = p.shape[-1]
    final = final_gain is not None
    rowspec = pl.BlockSpec((tm, D_MODEL), lambda i: (i, 0))
    vec = pl.BlockSpec((1, D_MODEL), lambda i: (0, 0))
    in_specs = [pl.BlockSpec((1, 1, 2 * tm), lambda i: (i, 0, 0), memory_space=pltpu.SMEM),
                rowspec, pl.BlockSpec((tm, 8), lambda i: (i, 0)),
                pl.BlockSpec((None, tm, dp), lambda i: (layer, i, 0)), vec,
                pl.BlockSpec((None, D_MODEL, D_MODEL), lambda i: (layer, 0, 0)),
                pl.BlockSpec((None, dp, D_MODEL), lambda i: (layer, 0, 0))]
    args = [dest, x, gw, p, gain.reshape(1, D_MODEL), w_gate, w_proj]
    out_shape = [jax.ShapeDtypeStruct((n, D_MODEL), F32)]
    out_specs = [rowspec]
    if final:
        in_specs.append(vec)
        args.append(final_gain.reshape(1, D_MODEL))
        out_shape.append(jax.ShapeDtypeStruct((n, D_MODEL), F32))
        out_specs.append(rowspec)
    in_specs.append(pl.BlockSpec(memory_space=pl.ANY))
    args.append(ys.reshape(-1, ROW_TILES, LANES))
    outs = pl.pallas_call(
        functools.partial(_ple_body, final=final, tm=tm), grid=(n // tm,), in_specs=in_specs,
        out_specs=out_specs, out_shape=out_shape,
        scratch_shapes=[pltpu.VMEM((tm * ROW_TILES, LANES), F32), pltpu.VMEM((tm * ROW_TILES, LANES), F32),
                        pltpu.SemaphoreType.DMA(())],
        compiler_params=_params(("arbitrary",)), name="ple")(*args)
    return (outs[0], outs[1]) if final else (outs[0], None)


ATTN_GROUP = 4
LOG2E = 1.4426950408889634


def _attn_group(qs, kcats, vcats, biases):
    R = CHUNK
    s = jnp.concatenate([_dot_nt(q, kcat) * (HEAD_DIM ** -0.5 * LOG2E) + bias
                         for q, kcat, bias in zip(qs, kcats, biases)], axis=0)
    m = jnp.max(s, axis=-1, keepdims=True)
    p = jnp.exp2(s - m)
    l = jnp.sum(p, axis=-1, keepdims=True)
    pb = p.astype(BF16)
    inv = 1.0 / l
    lse = m * (1.0 / LOG2E) + jnp.log(l)
    outs, lses = [], []
    for i, vcat in enumerate(vcats):
        sl = slice(i * R, (i + 1) * R)
        outs.append(_dot(pb[sl], vcat) * inv[sl])
        lses.append(jnp.broadcast_to(lse[sl], (R, LANES)))
    return outs, lses


def _attn_body(q0_ref, q1_ref, q2_ref, k_ref, v_ref, o_ref, kp_ref, vp_ref, og_ref, lg_ref, bias_ref):
    R = CHUNK
    row = _iota((R, 2 * R), 0)
    col = _iota((R, 2 * R), 1)
    first_key = jnp.where(pl.program_id(2) > 0, 0, R)
    bias_ref[0] = jnp.where((col >= row) & (col <= row + R), 0.0, -jnp.inf)
    bias_ref[1] = jnp.where((col >= jnp.maximum(row, first_key)) & (col <= row + R), 0.0, -jnp.inf)

    @pl.when(pl.program_id(2) == 0)
    def _():
        kp_ref[...] = jnp.zeros_like(kp_ref)
        vp_ref[...] = jnp.zeros_like(vp_ref)

    def rows(ref, start, dil):
        idx = pl.ds(start, R) if dil == 1 else pl.ds(start, R, stride=dil)
        return ref[idx, :]

    def put(ref, start, dil, val):
        idx = pl.ds(start, R) if dil == 1 else pl.ds(start, R, stride=dil)
        ref[idx, :] = val

    def group(g, q_ref, dil, blocks):
        nblk = ATTN_SPAN // (R * dil)
        qs, kcats, vcats, fks, starts = [], [], [], [], []
        for r, cb in blocks:
            start = r + dil * R * cb
            if cb == 0:
                prev_k, prev_v, prev_start, fk = kp_ref, vp_ref, r + dil * R * (nblk - 1), bias_ref[1]
            else:
                prev_k, prev_v, prev_start, fk = k_ref, v_ref, start - dil * R, bias_ref[0]
            kcats.append(jnp.concatenate([rows(prev_k, prev_start, dil).astype(BF16),
                                          rows(k_ref, start, dil).astype(BF16)], axis=0))
            vcats.append(jnp.concatenate([rows(prev_v, prev_start, dil).astype(BF16),
                                          rows(v_ref, start, dil).astype(BF16)], axis=0))
            qs.append(rows(q_ref, start, dil).astype(BF16))
            fks.append(fk)
            starts.append(start)
        outs, lses = _attn_group(qs, kcats, vcats, fks)
        for start, o, lse in zip(starts, outs, lses):
            put(og_ref.at[g], start, dil, o)
            put(lg_ref.at[g], start, dil, lse)

    for g, (q_ref, (_, dil)) in enumerate(zip((q0_ref, q1_ref, q2_ref), PATTERNS)):
        nblk = ATTN_SPAN // (R * dil)
        if nblk >= ATTN_GROUP:
            def lane(r, carry, q_ref=q_ref, dil=dil, nblk=nblk, g=g):
                for c0 in range(0, nblk, ATTN_GROUP):
                    group(g, q_ref, dil, [(r, cb) for cb in range(c0, c0 + ATTN_GROUP)])
                return carry
            n_iter = dil
        else:
            def lane(i, carry, q_ref=q_ref, dil=dil, nblk=nblk, g=g):
                for cb in range(nblk):
                    group(g, q_ref, dil, [(i * ATTN_GROUP + j, cb) for j in range(ATTN_GROUP)])
                return carry
            n_iter = dil // ATTN_GROUP
        if n_iter == 1:
            lane(0, 0)
        else:
            lax.fori_loop(0, n_iter, lane, 0)

    step = 256
    for i in range(ATTN_SPAN // step):
        sl = pl.ds(i * step, step)
        l0 = lg_ref[0, sl, :]
        l1 = lg_ref[1, sl, :]
        l2 = lg_ref[2, sl, :]
        m = jnp.maximum(jnp.maximum(l0, l1), l2)
        e0 = jnp.exp(l0 - m)
        e1 = jnp.exp(l1 - m)
        e2 = jnp.exp(l2 - m)
        num = e0 * og_ref[0, sl, :] + e1 * og_ref[1, sl, :] + e2 * og_ref[2, sl, :]
        o_ref[sl, :] = num * (1.0 / (e0 + e1 + e2))
    kp_ref[...] = k_ref[...]
    vp_ref[...] = v_ref[...]


def _attn_prompt(q, k, v, b, s):
    nspan = s // ATTN_SPAN
    blk = (ATTN_SPAN, HEAD_DIM)
    hblk = (None,) + blk
    qspec = lambda g: pl.BlockSpec(hblk, lambda bi, h, c: (g * N_KV_HEADS + h, bi * nspan + c, 0))
    kvspec = pl.BlockSpec(hblk, lambda bi, h, c: (h, bi * nspan + c, 0))
    return pl.pallas_call(
        _attn_body, grid=(b, N_KV_HEADS, nspan),
        in_specs=[qspec(0), qspec(1), qspec(2), kvspec, kvspec],
        out_specs=kvspec, out_shape=jax.ShapeDtypeStruct((N_KV_HEADS, b * s, HEAD_DIM), F32),
        scratch_shapes=[pltpu.VMEM(blk, F32), pltpu.VMEM(blk, F32),
                        pltpu.VMEM((3,) + blk, F32), pltpu.VMEM((3,) + blk, F32),
                        pltpu.VMEM((2, CHUNK, 2 * CHUNK), F32)],
        compiler_params=_params(("parallel", "parallel", "arbitrary")), name="attn_prompt")(q, q, q, k, v)


SAMPLE_ROWS = 16
DIL_MAX = PATTERNS[-1][1]
DENSE_TAIL = PATTERNS[1][0]


def _attn_sample_body(q_ref, kf_ref, vf_ref, kt_ref, vt_ref, kn_ref, vn_ref, bias_ref, o_ref, *, t_new):
    nq = N_KV_HEADS * SAMPLE_ROWS
    n_far = kf_ref.shape[1] * kf_ref.shape[2]
    n_tail = kt_ref.shape[1]
    q = q_ref[0].astype(BF16)
    kf = kf_ref[0].reshape(n_far, HEAD_DIM).astype(BF16)
    vf = vf_ref[0].reshape(n_far, HEAD_DIM).astype(BF16)
    s = jnp.concatenate([_dot_nt(q, kf), _dot_nt(q, kt_ref[0].astype(BF16)), _dot_nt(q, kn_ref[0].astype(BF16))],
                        axis=1) * (HEAD_DIM ** -0.5) + bias_ref[...]
    m = jnp.max(s, axis=-1, keepdims=True)
    p = jnp.exp(s - m)
    l = jnp.sum(p, axis=-1, keepdims=True)
    pb = p.astype(BF16)
    o = (_dot(pb[:, :n_far], vf) + _dot(pb[:, n_far:n_far + n_tail], vt_ref[0].astype(BF16))
         + _dot(pb[:, n_far + n_tail:], vn_ref[0].astype(BF16))) * (1.0 / l)
    lse = jnp.broadcast_to(m + jnp.log(l), (nq, LANES))
    lse1 = pltpu.roll(lse, nq - t_new, 0)
    lse2 = pltpu.roll(lse, nq - 2 * t_new, 0)
    o1 = pltpu.roll(o, nq - t_new, 0)
    o2 = pltpu.roll(o, nq - 2 * t_new, 0)
    mm = jnp.maximum(jnp.maximum(lse, lse1), lse2)
    e0 = jnp.exp(lse - mm)
    e1 = jnp.exp(lse1 - mm)
    e2 = jnp.exp(lse2 - mm)
    comb = (e0 * o + e1 * o1 + e2 * o2) * (1.0 / (e0 + e1 + e2))
    for h in range(N_KV_HEADS):
        o_ref[0, :, h * HEAD_DIM:(h + 1) * HEAD_DIM] = comb[h * SAMPLE_ROWS:h * SAMPLE_ROWS + t_new]


def _sample_bias(wb, t_new, far):
    rho = jnp.arange(N_KV_HEADS * SAMPLE_ROWS, dtype=I32)
    h = rho // SAMPLE_ROWS
    gt = rho % SAMPLE_ROWS
    g = jnp.where(gt < len(PATTERNS) * t_new, gt // t_new, 0)
    t = gt % t_new
    win = jnp.asarray([w for w, _ in PATTERNS], I32)[g]
    dil = jnp.asarray([d for _, d in PATTERNS], I32)[g]
    far_rows = t_new * N_KV_HEADS
    c = jnp.arange(far // DIL_MAX * far_rows, dtype=I32)
    pos_far = DIL_MAX * (c // far_rows) + (c % far_rows) // N_KV_HEADS
    c = jnp.arange((wb - far) * N_KV_HEADS, dtype=I32)
    pos_tail = far + c // N_KV_HEADS
    c = jnp.arange(LANES, dtype=I32)
    pos_new = jnp.where(c < far_rows, wb + c // N_KV_HEADS, -1)
    pos = jnp.concatenate([pos_far, pos_tail, pos_new])
    head = jnp.concatenate([jnp.arange(n, dtype=I32) % N_KV_HEADS for n in (pos_far.shape[0], pos_tail.shape[0], LANES)])
    dist = wb + t[:, None] - pos[None, :]
    valid = ((head[None, :] == h[:, None]) & (pos[None, :] >= 0) & (dist >= 0) & (dist <= win[:, None])
             & (dist % dil[:, None] == 0))
    return jnp.where(valid, 0.0, -jnp.inf).astype(F32)


def _attn_sample(q, cache_k, cache_v, k_new, v_new, b, t_new):
    wb = cache_k.shape[1]
    tail = min(DENSE_TAIL, wb)
    far = wb - tail
    far_rows = t_new * N_KV_HEADS
    nq = N_KV_HEADS * SAMPLE_ROWS
    qh = q.reshape(b, t_new, 3, N_KV_HEADS, HEAD_DIM).transpose(0, 3, 2, 1, 4).reshape(b, N_KV_HEADS, 3 * t_new, HEAD_DIM)
    qh = jnp.pad(qh, ((0, 0), (0, 0), (0, SAMPLE_ROWS - 3 * t_new), (0, 0))).reshape(b, nq, HEAD_DIM)
    new = lambda x: jnp.pad(x.reshape(b, far_rows, HEAD_DIM), ((0, 0), (0, LANES - far_rows), (0, 0)))
    grouped = lambda x: x.reshape(b, wb // DIL_MAX, DIL_MAX * N_KV_HEADS, HEAD_DIM)
    flat = lambda x: x.reshape(b, wb * N_KV_HEADS, HEAD_DIM)
    bias = _sample_bias(wb, t_new, far)
    far_spec = pl.BlockSpec((1, far // DIL_MAX, far_rows, HEAD_DIM), lambda bi: (bi, 0, 0, 0))
    tail_spec = pl.BlockSpec((1, tail * N_KV_HEADS, HEAD_DIM), lambda bi: (bi, wb // tail - 1, 0))
    sq_spec = pl.BlockSpec((1, LANES, HEAD_DIM), lambda bi: (bi, 0, 0))
    o = pl.pallas_call(
        functools.partial(_attn_sample_body, t_new=t_new), grid=(b,),
        in_specs=[sq_spec, far_spec, far_spec, tail_spec, tail_spec, sq_spec, sq_spec,
                  pl.BlockSpec(bias.shape, lambda bi: (0, 0))],
        out_specs=pl.BlockSpec((1, t_new, ATTN_DIM), lambda bi: (bi, 0, 0)),
        out_shape=jax.ShapeDtypeStruct((b, t_new, ATTN_DIM), F32),
        compiler_params=_params(("parallel",)), name="attn_sample")(
            qh, grouped(cache_k), grouped(cache_v), flat(cache_k), flat(cache_v), new(k_new), new(v_new), bias)
    return o.reshape(b * t_new, ATTN_DIM)


def _rope_tables(pos0, l, reps):
    inv_freq = ROPE_THETA ** (-jnp.arange(0, HEAD_DIM, 2, dtype=F32) / HEAD_DIM)
    pos = pos0 + jnp.arange(l, dtype=jnp.int32)
    ang = pos.astype(F32)[:, None] * inv_freq[None, :]
    cos = jnp.cos(ang)
    sin = jnp.sin(ang)
    cosf = jnp.concatenate([cos, cos], axis=-1)
    sinf = jnp.concatenate([-sin, sin], axis=-1)
    if reps > 1:
        cosf = jnp.tile(cosf, (reps, 1))
        sinf = jnp.tile(sinf, (reps, 1))
    return cosf, sinf


def _prep_weights(prm):
    w = {}
    w['in_proj'], w['in_proj_lo'] = _split2(prm['ssm_in_proj'])
    w['in_dt'], w['in_dt_lo'] = _split2(
        jnp.pad(prm['ssm_in_proj'][:, :, ZXBC_DIM:], ((0, 0), (0, 0), (0, LANES - SSM_HEADS))))
    w['out_proj'], w['out_proj_lo'] = _split2(prm['ssm_out_proj'])
    w['w_kv'] = prm['w_kv'].astype(BF16)
    w['w_q'] = prm['attn_w_q'].astype(BF16)
    w['w_o'] = prm['attn_w_o'].astype(BF16)
    w['ple_gate'] = prm['ple_w_gate'].astype(BF16)
    w['ple_proj'] = prm['ple_w_proj'].astype(BF16)
    wr = jnp.concatenate([jnp.swapaxes(prm['moe_w_rg'], 1, 2),
                          jnp.zeros((DEPTH, 8 - N_EXPERT_GROUPS, D_MODEL), F32),
                          jnp.swapaxes(prm['moe_w_re'], 1, 2)], axis=1)
    w['router'] = jnp.stack(_split3(wr), axis=1)
    w['router_b'] = jnp.concatenate([prm['moe_b_rg'], jnp.zeros((DEPTH, 8 - N_EXPERT_GROUPS), F32),
                                     prm['moe_b_re']], axis=1).reshape(DEPTH, ROUTER_ROWS, 1)
    return w


def _trunk_state(x, p, pos0, conv_state, ssm_state, cache_k, cache_v):
    b, l, _ = x.shape
    n = b * l
    tm = min(512, n)
    return dict(b=b, l=l, n=n, tm=tm, sample=cache_k is not None, xf=x.reshape(n, D_MODEL),
                p=p.reshape(DEPTH, n, -1), rope=_rope_tables(pos0, l, 1 if l >= tm else tm // l),
                conv_state=conv_state, ssm_state=ssm_state, cache_k=cache_k, cache_v=cache_v,
                new_conv=[], new_ssm=[], k32=None, v32=None, kh=None, vh=None, y_final=None)


def _mixer(t, i, prm, w):
    b, l, n, tm, xf = t['b'], t['l'], t['n'], t['tm'], t['xf']
    gain = prm['norm_mix'][i]
    if i < N_A_LAYERS:
        precise = i < PRECISE_MIXERS
        lo = (lambda name: w[name + '_lo']) if precise else (lambda name: None)
        zx = _proj(xf, w['in_proj'], w_lo=lo('in_proj'), layer=i, gain=gain, nout=ZXBC_DIM, tm=tm, tn=1024)
        dtr = _proj(xf, w['in_dt'], w_lo=lo('in_dt'), layer=i, gain=gain, tm=tm, tn=LANES)
        y, cbuf, hs = _ssd(zx.reshape(b, l, ZXBC_DIM), dtr.reshape(b, l, LANES), prm['ssm_conv_w'][i],
                           prm['ssm_conv_b'][i], prm['ssm_dt_bias'][i], prm['ssm_a_log'][i], prm['ssm_d'][i],
                           prm['ssm_norm_y'][i], t['conv_state'][i], t['ssm_state'], i, precise)
        t['new_conv'].append(cbuf)
        t['new_ssm'].append(hs)
        t['xf'] = _proj(y.reshape(n, D_INNER), w['out_proj'], w_lo=lo('out_proj'), layer=i, residual=xf,
                        tm=tm, tn=1024)
    else:
        j = i - N_A_LAYERS
        q = _proj(xf, w['w_q'], layer=j, gain=gain, rope=t['rope'], out_heads=(not t['sample'],), tm=tm, tn=1024)
        if t['sample']:
            att = _attn_sample(q, t['cache_k'], t['cache_v'], t['k32'], t['v32'], b, l)
        else:
            att = _attn_prompt(q, t['kh'], t['vh'], b, l)
        t['xf'] = _proj(att, w['w_o'], layer=j, residual=xf, tm=tm, tn=1024)


def _shared_kv(t, prm, w):
    tm, xf = t['tm'], t['xf']
    kv_out = dict(out_dtypes=(F32,), out_heads=(False,)) if t['sample'] else \
        dict(out_dtypes=(F32, F32), out_heads=(False, True))
    k_outs = _proj(xf, w['w_kv'], gain=prm['norm_kv'], rope=t['rope'], nout=ATTN_DIM, tm=tm, tn=1024, **kv_out)
    v_outs = _proj(xf, w['w_kv'], col0=ATTN_DIM, gain=prm['norm_kv'], nout=ATTN_DIM, tm=tm, tn=1024, **kv_out)
    (t['k32'], t['kh']), (t['v32'], t['vh']) = ((k_outs, None), (v_outs, None)) if t['sample'] else (k_outs, v_outs)


def _moe_ple(trunks, i, prm, w):
    routed = [_router(t['xf'], prm['norm_ffn'][i], w['router'], w['router_b'], i, t['tm']) for t in trunks]
    dests, tile_e, n_used, slots, pad_lo, pad_hi = _moe_plan([r[1] for r in routed], [r[2] for r in routed],
                                                             MOE_TILE, [t['tm'] for t in trunks])
    xs = None
    for t, (xn, _, _), dest in zip(trunks, routed, dests):
        xs = _dispatch(xn, dest, slots, t['tm'], xs)
    xs = _pad_zero(xs, pad_lo, pad_hi)
    ys = _moe_experts(xs.reshape(slots * ROW_TILES, LANES), tile_e, n_used, prm['moe_w_gate'], prm['moe_w_up'],
                      prm['moe_w_down'], i, MOE_TILE)
    for t, (_, slab, _), dest in zip(trunks, routed, dests):
        t['xf'], t['y_final'] = _ple(t['xf'], ys, dest, slab.T, t['p'], prm['ple_norm'][i], w['ple_gate'],
                                     w['ple_proj'], i, prm['norm_final'] if i == DEPTH - 1 else None, t['tm'])


def _run(trunks, prm, w):
    for i in range(DEPTH):
        for t in trunks:
            _mixer(t, i, prm, w)
        _moe_ple(trunks, i, prm, w)
        if i == N_A_LAYERS - 1:
            for t in trunks:
                _shared_kv(t, prm, w)
    return [(t['y_final'].reshape(t['b'], t['l'], D_MODEL), jnp.stack(t['new_conv']), jnp.stack(t['new_ssm']),
             t['k32'].reshape(t['b'], t['l'], ATTN_DIM), t['v32'].reshape(t['b'], t['l'], ATTN_DIM))
            for t in trunks]


def kernel(x_prompt, x_sample, state_ssm, state_conv, cache_k, cache_v, p_prompt, p_sample, norm_mix, ssm_in_proj, ssm_conv_w, ssm_conv_b, ssm_dt_bias, ssm_a_log, ssm_d, ssm_norm_y, ssm_out_proj, norm_kv, w_kv, attn_w_q, attn_w_o, norm_ffn, moe_w_rg, moe_b_rg, moe_w_re, moe_b_re, moe_w_gate, moe_w_up, moe_w_down, ple_w_proj, ple_w_gate, ple_norm, norm_final):
    prm = dict(norm_mix=norm_mix, ssm_in_proj=ssm_in_proj, ssm_conv_w=ssm_conv_w, ssm_conv_b=ssm_conv_b,
               ssm_dt_bias=ssm_dt_bias, ssm_a_log=ssm_a_log, ssm_d=ssm_d, ssm_norm_y=ssm_norm_y,
               ssm_out_proj=ssm_out_proj, norm_kv=norm_kv, w_kv=w_kv, attn_w_q=attn_w_q, attn_w_o=attn_w_o,
               norm_ffn=norm_ffn, moe_w_rg=moe_w_rg, moe_b_rg=moe_b_rg, moe_w_re=moe_w_re, moe_b_re=moe_b_re,
               moe_w_gate=moe_w_gate, moe_w_up=moe_w_up, moe_w_down=moe_w_down, ple_w_proj=ple_w_proj,
               ple_w_gate=ple_w_gate, ple_norm=ple_norm, norm_final=norm_final)
    w = _prep_weights(prm)
    b, s, _ = x_prompt.shape
    db, dl, _ = x_sample.shape
    past_len = 8192
    conv0 = jnp.zeros((N_A_LAYERS, b, 3, CONV_DIM), x_prompt.dtype)
    ssm0 = jnp.zeros((N_A_LAYERS, b, SSM_HEADS, SSM_HEADDIM, D_STATE), state_ssm.dtype)
    trunks = [_trunk_state(x_prompt, p_prompt, 0, conv0, ssm0, None, None),
              _trunk_state(x_sample, p_sample, past_len, state_conv, state_ssm, cache_k, cache_v)]
    (y_p, conv_p, ssm_p, k_p, v_p), (y_s, conv_s, ssm_s, k_s, v_s) = _run(trunks, prm, w)
    keep = min(2048, s)
    heads = lambda t: t.reshape(t.shape[0], t.shape[1], N_KV_HEADS, HEAD_DIM)
    k_p = heads(k_p[:, s - keep:])
    v_p = heads(v_p[:, s - keep:])
    return (y_p, y_s, ssm_p, conv_p, k_p, v_p, ssm_s, conv_s, heads(k_s), heads(v_s))
```

```python
import functools

import jax
import jax.numpy as jnp
from jax import lax
from jax.experimental import pallas as pl
from jax.experimental.pallas import tpu as pltpu

F32 = jnp.float32
BF16 = jnp.bfloat16
I32 = jnp.int32

EPS = 1e-6
D_MODEL = 1024
D_INNER = 2048
SSM_HEADS = 32
SSM_HEADDIM = 64
SSM_GROUPS = 4
D_STATE = 128
CONV_DIM = 3072
ZXBC_DIM = D_INNER + CONV_DIM
CHUNK = 128
HEAD_DIM = 128
N_KV_HEADS = 8
ATTN_DIM = 1024
PATTERNS = ((128, 1), (512, 4), (2048, 16))
ATTN_SPAN = 2048
ROPE_THETA = 10000.0
N_EXPERTS = 32
N_EXPERT_GROUPS = 4
EXPERTS_PER_GROUP = 8
D_EXPERT = 512
N_A_LAYERS = 2
DEPTH = 4
PRECISE_MIXERS = 1
MOE_TILE = 256
PROJ_ROWS = 1024
ROUTER_ROWS = 8 + N_EXPERTS
LANES = 128
SUBLANES = 8
ROW_TILES = D_MODEL // LANES
VMEM_LIMIT = 56 * 1024 * 1024


def _params(sem):
    return pltpu.CompilerParams(dimension_semantics=sem, vmem_limit_bytes=VMEM_LIMIT)


def _iota(shape, dim):
    return lax.broadcasted_iota(I32, shape, dim)


def _rms(x, gain):
    ms = jnp.mean(x * x, axis=-1, keepdims=True)
    return x * lax.rsqrt(ms + EPS) * gain


def _sigmoid(x):
    return 0.5 * jnp.tanh(0.5 * x) + 0.5


def _silu(x):
    return x * _sigmoid(x)


def _dot(a, b):
    return jnp.dot(a, b, preferred_element_type=F32)


def _dot_nt(a, b):
    return lax.dot_general(a, b, (((1,), (1,)), ((), ())), preferred_element_type=F32)


def _split3(x):
    hi = x.astype(BF16)
    r = x - hi.astype(F32)
    mid = r.astype(BF16)
    lo = (r - mid.astype(F32)).astype(BF16)
    return hi, mid, lo


def _split2(x):
    hi = x.astype(BF16)
    return hi, (x - hi.astype(F32)).astype(BF16)


def _mm(a, b, dot):
    y = dot(a[0], b[0])
    if len(a) > 1:
        y = y + dot(a[1], b[0]) + dot(a[0], b[1])
    return y


def _rows_to_tiles(ref, x):
    m = x.shape[0]
    for s in range(ROW_TILES):
        ref[pl.ds(s, m, stride=ROW_TILES), :] = x[:, s * LANES:(s + 1) * LANES]


def _tiles_to_rows(ref, m):
    return jnp.concatenate([ref[pl.ds(s, m, stride=ROW_TILES), :] for s in range(ROW_TILES)], axis=1)


def _proj_body(*refs, has_norm, has_rope, has_res, x_heads, out_heads, precise):
    it = iter(refs)
    x_ref = next(it)
    w_ref = next(it)
    wlo_ref = next(it) if precise else None
    g_ref = next(it) if has_norm else None
    cos_ref = next(it) if has_rope else None
    sin_ref = next(it) if has_rope else None
    r_ref = next(it) if has_res else None
    o_refs = [next(it) for _ in out_heads]
    xb_ref = next(it)
    xlo_ref = next(it) if precise else None

    @pl.when(pl.program_id(1) == 0)
    def _():
        if x_heads:
            x = jnp.concatenate([x_ref[h] for h in range(x_ref.shape[0])], axis=1).astype(F32)
        else:
            x = x_ref[...].astype(F32)
        if has_norm:
            x = _rms(x, g_ref[...])
        if precise:
            xb_ref[...], xlo_ref[...] = _split2(x)
        else:
            xb_ref[...] = x.astype(BF16)

    if precise:
        y = _mm((xb_ref[...], xlo_ref[...]), (w_ref[...], wlo_ref[...]), _dot)
    else:
        y = _dot(xb_ref[...], w_ref[...])
    if has_rope:
        cos = cos_ref[...]
        sin = sin_ref[...]
        parts = []
        for h in range(y.shape[1] // HEAD_DIM):
            t = y[:, h * HEAD_DIM:(h + 1) * HEAD_DIM]
            parts.append(t * cos + pltpu.roll(t, HEAD_DIM // 2, 1) * sin)
        y = parts[0] if len(parts) == 1 else jnp.concatenate(parts, axis=1)
    if has_res:
        y = y + r_ref[...]
    for o, heads in zip(o_refs, out_heads):
        if heads:
            for h in range(o.shape[0]):
                o[h] = y[:, h * HEAD_DIM:(h + 1) * HEAD_DIM].astype(o.dtype)
        else:
            o[...] = y.astype(o.dtype)


def _proj(x, w, *, w_lo=None, layer=None, col0=0, gain=None, rope=None, residual=None, out_dtypes=(F32,),
          out_heads=None, nout=None, tm, tn):
    x_heads = x.ndim == 3
    n = x.shape[-2]
    k = x.shape[0] * x.shape[2] if x_heads else x.shape[1]
    nout = w.shape[-1] if nout is None else nout
    tn = min(tn, nout)
    out_heads = (False,) * len(out_dtypes) if out_heads is None else out_heads
    grid = (n // tm, nout // tn)
    jb = col0 // tn
    precise = w_lo is not None
    wspec = (pl.BlockSpec((k, tn), lambda i, j: (0, j + jb)) if layer is None else
             pl.BlockSpec((None, k, tn), lambda i, j: (layer, 0, j + jb)))
    in_specs = [pl.BlockSpec((k // HEAD_DIM, tm, HEAD_DIM), lambda i, j: (0, i, 0)) if x_heads else
                pl.BlockSpec((tm, k), lambda i, j: (i, 0)), wspec]
    args = [x, w]
    if precise:
        in_specs.append(wspec)
        args.append(w_lo)
    if gain is not None:
        in_specs.append(pl.BlockSpec((1, k), lambda i, j: (0, 0)))
        args.append(gain.reshape(1, k))
    if rope is not None:
        cos, sin = rope
        nblk = cos.shape[0] // tm
        for t in (cos, sin):
            in_specs.append(pl.BlockSpec((tm, HEAD_DIM), lambda i, j: (i % nblk, 0)))
            args.append(t)
    if residual is not None:
        in_specs.append(pl.BlockSpec((tm, tn), lambda i, j: (i, j)))
        args.append(residual)
    out_shape = [jax.ShapeDtypeStruct((nout // HEAD_DIM, n, HEAD_DIM) if hm else (n, nout), dt)
                 for dt, hm in zip(out_dtypes, out_heads)]
    out_specs = [pl.BlockSpec((tn // HEAD_DIM, tm, HEAD_DIM), lambda i, j: (j, i, 0)) if hm else
                 pl.BlockSpec((tm, tn), lambda i, j: (i, j)) for hm in out_heads]
    body = functools.partial(_proj_body, has_norm=gain is not None, has_rope=rope is not None,
                             has_res=residual is not None, x_heads=x_heads, out_heads=tuple(out_heads),
                             precise=precise)
    outs = pl.pallas_call(
        body, grid=grid, in_specs=in_specs, out_specs=out_specs, out_shape=out_shape,
        scratch_shapes=[pltpu.VMEM((tm, k), BF16)] * (2 if precise else 1),
        compiler_params=_params(("parallel", "arbitrary")), name="proj")(*args)
    return outs[0] if len(outs) == 1 else outs


def _ssd_body(zx_ref, dt_ref, cw_ref, cb_ref, dtb_ref, alog_ref, d_ref, ny_ref, conv0_ref, ssm0_ref,
              y_ref, convo_ref, ssmo_ref, st_ref, tail_ref, xbc_ref, yacc_ref, *pad_refs, valid, nchunks, precise):
    L = CHUNK
    c = pl.program_id(1)
    npair = SSM_HEADS // 2
    parts = _split2 if precise else (lambda x: (x.astype(BF16),))

    @pl.when(c == 0)
    def _():
        for j in range(npair):
            st_ref[:, j * LANES:(j + 1) * LANES] = ssm0_ref[0, j * LANES:(j + 1) * LANES, :].T
        tail_ref[0:8, :] = conv0_ref[0]

    if valid < L:
        pzx_ref, pdt_ref = pad_refs
        pzx_ref[...] = jnp.zeros_like(pzx_ref)
        pdt_ref[...] = jnp.zeros_like(pdt_ref)
        pzx_ref[0:valid, :] = zx_ref[0]
        pdt_ref[0:valid, :] = dt_ref[0]
        z = pzx_ref[:, :D_INNER]
        u = pzx_ref[:, D_INNER:]
        dtr = pdt_ref[...]
    else:
        z = zx_ref[0, :, :D_INNER]
        u = zx_ref[0, :, D_INNER:]
        dtr = dt_ref[0]

    tail_ref[8:8 + L, :] = u
    acc = cb_ref[...] + cw_ref[3:4, :] * u
    for k in range(1, 4):
        acc = acc + cw_ref[3 - k:4 - k, :] * tail_ref[8 - k:8 - k + L, :]
    tail_ref[0:8, :] = tail_ref[valid:valid + 8, :]
    xbc_ref[...] = _silu(acc)

    lane = _iota((1, LANES), 1)
    rowl = _iota((L, 1), 0)
    xdt = dtr + dtb_ref[...]
    dt = jnp.maximum(xdt, 0.0) + jnp.log(1.0 + jnp.exp(-jnp.abs(xdt)))
    dt = jnp.where((lane < SSM_HEADS) & (rowl < valid), dt, 0.0)
    a_neg = -jnp.exp(alog_ref[...])
    da = dt * a_neg
    rr = _iota((L, L), 0)
    cc = _iota((L, L), 1)
    causal = rr >= cc
    tri = jnp.where(causal, 1.0, 0.0).astype(BF16)
    hi, mid, lo = _split3(da)
    a_cs = _dot(tri, hi) + _dot(tri, mid) + _dot(tri, lo)
    a_cs_t = a_cs.T
    dt_t = dt.T
    tdt_t = dt_t * jnp.exp(a_cs_t[:, L - 1:L] - a_cs_t)
    ea = jnp.exp(a_cs)
    half = _iota((1, LANES), 1) < SSM_HEADDIM

    for g in range(SSM_GROUPS):
        b_g = xbc_ref[:, D_INNER + g * D_STATE:D_INNER + (g + 1) * D_STATE]
        c_g = xbc_ref[:, D_INNER + SSM_GROUPS * D_STATE + g * D_STATE:
                      D_INNER + SSM_GROUPS * D_STATE + (g + 1) * D_STATE]
        cb = _mm(parts(c_g), parts(b_g), _dot_nt)
        b_t = b_g.T
        for jj in range(npair // SSM_GROUPS):
            j = g * (npair // SSM_GROUPS) + jj
            cols = slice(j * LANES, (j + 1) * LANES)
            xs_b = parts(xbc_ref[:, cols])
            st_p = st_ref[:, cols]
            rhs = tuple(jnp.concatenate([a, b], axis=0) for a, b in zip(xs_b, parts(st_p)))
            ys, ss, cds = [], [], []
            for h in (2 * j, 2 * j + 1):
                colb = jnp.broadcast_to(a_cs[:, h:h + 1], (L, LANES))
                seg = colb - a_cs_t[h:h + 1, :]
                dec = jnp.exp(jnp.where(causal, seg, -jnp.inf))
                m = cb * dec * dt_t[h:h + 1, :]
                ecol = jnp.broadcast_to(ea[:, h:h + 1], (L, LANES))
                ce = c_g * ecol
                lhs = tuple(jnp.concatenate([a, b], axis=1) for a, b in zip(parts(m), parts(ce)))
                ys.append(_mm(lhs, rhs, _dot))
                ss.append(_mm(parts(b_t * tdt_t[h:h + 1, :]), xs_b, _dot))
                cds.append(ecol[L - 1:L, :])
            yacc_ref[:, cols] = jnp.where(half, ys[0], ys[1])
            st_ref[:, cols] = jnp.where(half, st_p * cds[0] + ss[0], st_p * cds[1] + ss[1])

    y = yacc_ref[...] + d_ref[...] * xbc_ref[:, :D_INNER]
    yn = _rms(y * _silu(z), ny_ref[...])
    if valid < L:
        y_ref[0] = yn[0:valid].astype(y_ref.dtype)
    else:
        y_ref[0] = yn.astype(y_ref.dtype)

    @pl.when(c == nchunks - 1)
    def _():
        convo_ref[0] = tail_ref[0:8, :]
        for j in range(npair):
            ssmo_ref[0, j * LANES:(j + 1) * LANES, :] = st_ref[:, j * LANES:(j + 1) * LANES].T


def _ssd(zx, dtr, conv_w, conv_b, dt_bias, a_log, d_skip, norm_y, conv0, ssm0, layer, precise):
    b, l, _ = zx.shape
    if l >= CHUNK:
        lb, valid, nchunks = CHUNK, CHUNK, l // CHUNK
    else:
        lb, valid, nchunks = l, l, 1
    pad_lanes = LANES - SSM_HEADS
    dtb = jnp.pad(dt_bias.astype(F32), (0, pad_lanes)).reshape(1, LANES)
    alog = jnp.pad(a_log.astype(F32), (0, pad_lanes)).reshape(1, LANES)
    d_exp = jnp.repeat(d_skip.astype(F32), SSM_HEADDIM).reshape(1, D_INNER)
    conv0p = jnp.pad(conv0, ((0, 0), (5, 0), (0, 0)))
    ssm0f = ssm0.reshape(ssm0.shape[0], b, SSM_HEADS * SSM_HEADDIM, D_STATE)
    const = lambda bi, c: (0, 0)
    in_specs = [
        pl.BlockSpec((1, lb, ZXBC_DIM), lambda bi, c: (bi, c, 0)),
        pl.BlockSpec((1, lb, LANES), lambda bi, c: (bi, c, 0)),
        pl.BlockSpec((4, CONV_DIM), const),
        pl.BlockSpec((1, CONV_DIM), const),
        pl.BlockSpec((1, LANES), const),
        pl.BlockSpec((1, LANES), const),
        pl.BlockSpec((1, D_INNER), const),
        pl.BlockSpec((1, D_INNER), const),
        pl.BlockSpec((1, 8, CONV_DIM), lambda bi, c: (bi, 0, 0)),
        pl.BlockSpec((None, 1, D_INNER, D_STATE), lambda bi, c: (layer, bi, 0, 0)),
    ]
    out_shape = [jax.ShapeDtypeStruct((b, l, D_INNER), F32 if precise else BF16),
                 jax.ShapeDtypeStruct((b, 8, CONV_DIM), F32),
                 jax.ShapeDtypeStruct((b, D_INNER, D_STATE), F32)]
    out_specs = [pl.BlockSpec((1, lb, D_INNER), lambda bi, c: (bi, c, 0)),
                 pl.BlockSpec((1, 8, CONV_DIM), lambda bi, c: (bi, 0, 0)),
                 pl.BlockSpec((1, D_INNER, D_STATE), lambda bi, c: (bi, 0, 0))]
    scratch = [pltpu.VMEM((D_STATE, D_INNER), F32), pltpu.VMEM((8 + CHUNK, CONV_DIM), F32),
               pltpu.VMEM((CHUNK, CONV_DIM), F32), pltpu.VMEM((CHUNK, D_INNER), F32)]
    if valid < CHUNK:
        scratch += [pltpu.VMEM((CHUNK, ZXBC_DIM), F32), pltpu.VMEM((CHUNK, LANES), F32)]
    y, convo, ssmo = pl.pallas_call(
        functools.partial(_ssd_body, valid=valid, nchunks=nchunks, precise=precise),
        grid=(b, nchunks), in_specs=in_specs, out_specs=out_specs, out_shape=out_shape,
        scratch_shapes=scratch, compiler_params=_params(("parallel", "arbitrary")), name="ssd")(
            zx, dtr, conv_w, conv_b.reshape(1, CONV_DIM), dtb, alog, d_exp, norm_y.reshape(1, D_INNER),
            conv0p, ssm0f)
    return y, convo[:, 5:], ssmo.reshape(b, SSM_HEADS, SSM_HEADDIM, D_STATE)


def _router_body(x_ref, g_ref, w_ref, b_ref, xn_ref, slab_ref, cnt_ref, base_ref, su_ref):
    tm = x_ref.shape[0]

    @pl.when(pl.program_id(0) == 0)
    def _():
        base_ref[...] = jnp.zeros_like(base_ref)
        su_ref[...] = jnp.where(_iota((tm, tm), 0) < _iota((tm, tm), 1), 1.0, 0.0).astype(BF16)

    xn = _rms(x_ref[...], g_ref[...])
    _rows_to_tiles(xn_ref, xn)
    xh, xm, xl = _split3(xn)
    wh = w_ref[0]
    wm = w_ref[1]
    wl = w_ref[2]
    logits = (_dot_nt(wh, xh) + _dot_nt(wh, xm) + _dot_nt(wm, xh)
              + _dot_nt(wh, xl) + _dot_nt(wm, xm) + _dot_nt(wl, xh)) + b_ref[...]
    gl = [logits[i:i + 1, :] for i in range(N_EXPERT_GROUPS)]
    best = gl[0]
    sel = jnp.zeros((1, tm), I32)
    for i in range(1, N_EXPERT_GROUPS):
        upd = gl[i] > best
        sel = jnp.where(upd, i, sel)
        best = jnp.where(upd, gl[i], best)
    den = jnp.exp(gl[0] - best)
    for i in range(1, N_EXPERT_GROUPS):
        den = den + jnp.exp(gl[i] - best)
    g_w = 1.0 / den
    chosen = logits[8:8 + EXPERTS_PER_GROUP, :]
    for i in range(1, N_EXPERT_GROUPS):
        chosen = jnp.where(sel == i, logits[8 + i * EXPERTS_PER_GROUP:8 + (i + 1) * EXPERTS_PER_GROUP, :], chosen)
    idx = _iota((EXPERTS_PER_GROUP, tm), 0)
    m1 = jnp.max(chosen, axis=0, keepdims=True)
    i1 = jnp.min(jnp.where(chosen == m1, idx, EXPERTS_PER_GROUP), axis=0, keepdims=True)
    rest = jnp.where(idx == i1, -jnp.inf, chosen)
    m2 = jnp.max(rest, axis=0, keepdims=True)
    i2 = jnp.min(jnp.where(rest == m2, idx, EXPERTS_PER_GROUP), axis=0, keepdims=True)
    t = jnp.exp(m2 - m1)
    inv = 1.0 / (1.0 + t)
    w0 = inv * g_w
    w1 = (t * inv) * g_w
    e0 = sel * EXPERTS_PER_GROUP + i1
    e1 = sel * EXPERTS_PER_GROUP + i2

    eidx = _iota((N_EXPERTS, tm), 0)
    oh0 = eidx == e0
    oh1 = eidx == e1
    oh = jnp.where(oh0, 1.0, jnp.where(oh1, 1.0, 0.0))
    base = base_ref[...]
    before = _dot(oh.astype(BF16), su_ref[...]) + jnp.concatenate([base] * (tm // LANES), axis=1)
    rank0 = jnp.sum(jnp.where(oh0, before, 0.0), axis=0, keepdims=True)
    rank1 = jnp.sum(jnp.where(oh1, before, 0.0), axis=0, keepdims=True)
    new_base = base + jnp.sum(oh, axis=1, keepdims=True)
    base_ref[...] = new_base
    cnt_ref[...] = new_base

    r = _iota((8, tm), 0)
    rows = (e0.astype(F32), e1.astype(F32), w0, w1, rank0, rank1)
    slab = jnp.zeros((8, tm), F32)
    for i, v in enumerate(rows):
        slab = jnp.where(r == i, v, slab)
    slab_ref[...] = slab


def _router(x, gain, w_split, bias, layer, tm):
    n = x.shape[0]
    xn, slab, cnt = pl.pallas_call(
        _router_body, grid=(n // tm,),
        in_specs=[pl.BlockSpec((tm, D_MODEL), lambda i: (i, 0)),
                  pl.BlockSpec((1, D_MODEL), lambda i: (0, 0)),
                  pl.BlockSpec((None, 3, ROUTER_ROWS, D_MODEL), lambda i: (layer, 0, 0, 0)),
                  pl.BlockSpec((None, ROUTER_ROWS, 1), lambda i: (layer, 0, 0))],
        out_specs=[pl.BlockSpec((tm * ROW_TILES, LANES), lambda i: (i, 0)),
                   pl.BlockSpec((8, tm), lambda i: (0, i)),
                   pl.BlockSpec((N_EXPERTS, LANES), lambda i: (0, 0))],
        out_shape=[jax.ShapeDtypeStruct((n * ROW_TILES, LANES), F32), jax.ShapeDtypeStruct((8, n), F32),
                   jax.ShapeDtypeStruct((N_EXPERTS, LANES), F32)],
        scratch_shapes=[pltpu.VMEM((N_EXPERTS, LANES), F32), pltpu.VMEM((tm, tm), BF16)],
        compiler_params=_params(("arbitrary",)), name="router")(x, gain.reshape(1, D_MODEL), w_split, bias)
    return xn, slab, cnt


def _moe_plan(slabs, cnts, tm, blks):
    counts = [cnt[:, 0].astype(I32) for cnt in cnts]
    total = sum(counts)
    padded = ((total + tm - 1) // tm) * tm
    pend = jnp.cumsum(padded)
    pstart = pend - padded
    dests = []
    base = pstart
    for slab, count, blk in zip(slabs, counts, blks):
        n = slab.shape[1]
        e_ids = slab[0:2].astype(I32)
        ranks = slab[4:6].astype(I32)
        onehot = e_ids[..., None] == jnp.arange(N_EXPERTS, dtype=I32)
        dest = jnp.sum(jnp.where(onehot, base, 0), axis=-1) + ranks
        dests.append(dest.T.reshape(n // blk, 1, 2 * blk))
        base = base + count
    slots = 2 * sum(slab.shape[1] for slab in slabs) + N_EXPERTS * tm
    n_tiles = slots // tm
    n_used = (pend[-1] // tm).astype(I32)
    tile_start = jnp.arange(n_tiles, dtype=I32) * tm
    tile_e = jnp.sum((pend[None, :] <= tile_start[:, None]).astype(I32), axis=1)
    tile_e = jnp.minimum(tile_e, N_EXPERTS - 1)
    last_e = jnp.sum(jnp.where(jnp.arange(n_tiles) == n_used - 1, tile_e, 0))
    tile_e = jnp.where(jnp.arange(n_tiles) < n_used, tile_e, last_e).astype(I32)
    return dests, tile_e, n_used.reshape(1), slots, (pstart + total).astype(I32), pend.astype(I32)


def _dispatch_body(dest_ref, xn_ref, *rest, blk):
    xs_hbm, sem = rest[-2:]

    def issue(i, carry):
        src = xn_ref.at[pl.ds(pl.multiple_of(i * ROW_TILES, ROW_TILES), ROW_TILES)]
        pltpu.make_async_copy(src, xs_hbm.at[dest_ref[0, 0, 2 * i]], sem).start()
        pltpu.make_async_copy(src, xs_hbm.at[dest_ref[0, 0, 2 * i + 1]], sem).start(priority=1)
        return carry

    lax.fori_loop(0, blk, issue, 0, unroll=8)

    def drain(i, carry):
        pltpu.make_async_copy(xn_ref.at[pl.ds(0, ROW_TILES)], xs_hbm.at[0], sem).wait()
        return carry

    lax.fori_loop(0, 2 * blk, drain, 0, unroll=8)


def _dispatch(xn, dest, slots, blk, xs=None):
    n = xn.shape[0] // ROW_TILES
    in_specs = [pl.BlockSpec((1, 1, 2 * blk), lambda i: (i, 0, 0), memory_space=pltpu.SMEM),
                pl.BlockSpec((blk * ROW_TILES, LANES), lambda i: (i, 0))]
    args = [dest, xn]
    if xs is not None:
        in_specs.append(pl.BlockSpec(memory_space=pl.ANY))
        args.append(xs)
    return pl.pallas_call(
        functools.partial(_dispatch_body, blk=blk), grid=(n // blk,), in_specs=in_specs,
        out_specs=pl.BlockSpec(memory_space=pl.ANY),
        out_shape=jax.ShapeDtypeStruct((slots, ROW_TILES, LANES), F32),
        scratch_shapes=[pltpu.SemaphoreType.DMA(())],
        input_output_aliases={} if xs is None else {2: 0},
        compiler_params=_params(("arbitrary",)), name="dispatch")(*args)


def _pad_zero_body(lo_ref, hi_ref, xs_in_hbm, xs_hbm, z_ref, sem):
    del xs_in_hbm
    z_ref[...] = jnp.zeros_like(z_ref)
    sizes = [1 << s for s in range(MOE_TILE.bit_length() - 2, -1, -1)]

    def copies(e, act):
        pos = lo_ref[e]
        n = hi_ref[e] - pos
        for size in sizes:
            take = (n & size) != 0

            @pl.when(take)
            def _(pos=pos, size=size):
                act(pltpu.make_async_copy(z_ref.at[pl.ds(0, size)], xs_hbm.at[pl.ds(pos, size)], sem))

            pos = pos + jnp.where(take, size, 0)

    def issue(e, carry):
        copies(e, lambda cp: cp.start())
        return carry

    def drain(e, carry):
        copies(e, lambda cp: cp.wait())
        return carry

    lax.fori_loop(0, N_EXPERTS, issue, 0)
    lax.fori_loop(0, N_EXPERTS, drain, 0)


def _pad_zero(xs, lo, hi):
    grid_spec = pltpu.PrefetchScalarGridSpec(
        num_scalar_prefetch=2, grid=(1,),
        in_specs=[pl.BlockSpec(memory_space=pl.ANY)], out_specs=pl.BlockSpec(memory_space=pl.ANY),
        scratch_shapes=[pltpu.VMEM((MOE_TILE // 2, ROW_TILES, LANES), F32), pltpu.SemaphoreType.DMA(())])
    return pl.pallas_call(
        _pad_zero_body, grid_spec=grid_spec, out_shape=jax.ShapeDtypeStruct(xs.shape, xs.dtype),
        input_output_aliases={2: 0},
        compiler_params=_params(("arbitrary",)), name="pad_zero")(lo, hi, xs)


def _moe_body(te_ref, nu_ref, xs_ref, wg_ref, wu_ref, wd_ref, o_ref, wgb_ref, wub_ref, wdb_ref, *, tm):
    t = pl.program_id(0)
    e = te_ref[t]
    prev = te_ref[jnp.maximum(t - 1, 0)]

    @pl.when((t == 0) | (e != prev))
    def _():
        wgb_ref[...] = wg_ref[0].astype(BF16)
        wub_ref[...] = wu_ref[0].astype(BF16)
        wdb_ref[...] = wd_ref[0].astype(BF16)

    @pl.when(t < nu_ref[0])
    def _():
        x = _tiles_to_rows(xs_ref, tm).astype(BF16)
        g = _dot(x, wgb_ref[...])
        u = _dot(x, wub_ref[...])
        hid = _silu(g) * u
        _rows_to_tiles(o_ref, _dot(hid.astype(BF16), wdb_ref[...]))


def _moe_experts(xs, tile_e, n_used, w_gate, w_up, w_down, layer, tm):
    slots = xs.shape[0] // ROW_TILES
    n_tiles = slots // tm
    row = lambda t, te, nu: (jnp.minimum(t, nu[0] - 1), 0)
    wsel = lambda t, te, nu: (layer, te[t], 0, 0)
    grid_spec = pltpu.PrefetchScalarGridSpec(
        num_scalar_prefetch=2, grid=(n_tiles,),
        in_specs=[pl.BlockSpec((tm * ROW_TILES, LANES), row),
                  pl.BlockSpec((None, 1, D_MODEL, D_EXPERT), wsel),
                  pl.BlockSpec((None, 1, D_MODEL, D_EXPERT), wsel),
                  pl.BlockSpec((None, 1, D_EXPERT, D_MODEL), wsel)],
        out_specs=pl.BlockSpec((tm * ROW_TILES, LANES), row),
        scratch_shapes=[pltpu.VMEM((D_MODEL, D_EXPERT), BF16), pltpu.VMEM((D_MODEL, D_EXPERT), BF16),
                        pltpu.VMEM((D_EXPERT, D_MODEL), BF16)])
    return pl.pallas_call(
        functools.partial(_moe_body, tm=tm), grid_spec=grid_spec,
        out_shape=jax.ShapeDtypeStruct((slots * ROW_TILES, LANES), F32),
        compiler_params=_params(("arbitrary",)), name="moe_experts")(tile_e, n_used, xs, w_gate, w_up, w_down)


def _ple_body(*refs, final, tm):
    dest_ref, x_ref, gw_ref, p_ref, g_ref, wg_ref, wp_ref = refs[:7]
    if final:
        fg_ref, ys_hbm, o_ref, f_ref, y0_ref, y1_ref, sem = refs[7:]
    else:
        ys_hbm, o_ref, y0_ref, y1_ref, sem = refs[7:]

    def issue(i, carry):
        row = pl.multiple_of(i * ROW_TILES, ROW_TILES)
        pltpu.make_async_copy(ys_hbm.at[dest_ref[0, 0, 2 * i]], y0_ref.at[pl.ds(row, ROW_TILES)], sem).start()
        pltpu.make_async_copy(ys_hbm.at[dest_ref[0, 0, 2 * i + 1]], y1_ref.at[pl.ds(row, ROW_TILES)], sem).start(priority=1)
        return carry

    lax.fori_loop(0, tm, issue, 0, unroll=8)
    pe = _dot(p_ref[...].astype(BF16), wp_ref[...])

    def drain(i, carry):
        pltpu.make_async_copy(ys_hbm.at[0], y0_ref.at[pl.ds(0, ROW_TILES)], sem).wait()
        return carry

    lax.fori_loop(0, 2 * tm, drain, 0, unroll=8)
    gw = gw_ref[...]
    x2 = x_ref[...] + gw[:, 2:3] * _tiles_to_rows(y0_ref, tm) + gw[:, 3:4] * _tiles_to_rows(y1_ref, tm)
    xn = _rms(x2, g_ref[...])
    gate = _sigmoid(_dot(xn.astype(BF16), wg_ref[...]))
    out = x2 + pe * gate
    o_ref[...] = out
    if final:
        f_ref[...] = _rms(out, fg_ref[...])


def _ple(x, ys, dest, gw, p, gain, w_gate, w_proj, layer, final_gain, tm):
    n = x.shape[0]
    dp = p.shape[-1]
    final = final_gain is not None
    rowspec = pl.BlockSpec((tm, D_MODEL), lambda i: (i, 0))
    vec = pl.BlockSpec((1, D_MODEL), lambda i: (0, 0))
    in_specs = [pl.BlockSpec((1, 1, 2 * tm), lambda i: (i, 0, 0), memory_space=pltpu.SMEM),
                rowspec, pl.BlockSpec((tm, 8), lambda i: (i, 0)),
                pl.BlockSpec((None, tm, dp), lambda i: (layer, i, 0)), vec,
                pl.BlockSpec((None, D_MODEL, D_MODEL), lambda i: (layer, 0, 0)),
                pl.BlockSpec((None, dp, D_MODEL), lambda i: (layer, 0, 0))]
    args = [dest, x, gw, p, gain.reshape(1, D_MODEL), w_gate, w_proj]
    out_shape = [jax.ShapeDtypeStruct((n, D_MODEL), F32)]
    out_specs = [rowspec]
    if final:
        in_specs.append(vec)
        args.append(final_gain.reshape(1, D_MODEL))
        out_shape.append(jax.ShapeDtypeStruct((n, D_MODEL), F32))
        out_specs.append(rowspec)
    in_specs.append(pl.BlockSpec(memory_space=pl.ANY))
    args.append(ys.reshape(-1, ROW_TILES, LANES))
    outs = pl.pallas_call(
        functools.partial(_ple_body, final=final, tm=tm), grid=(n // tm,), in_specs=in_specs,
        out_specs=out_specs, out_shape=out_shape,
        scratch_shapes=[pltpu.VMEM((tm * ROW_TILES, LANES), F32), pltpu.VMEM((tm * ROW_TILES, LANES), F32),
                        pltpu.SemaphoreType.DMA(())],
        compiler_params=_params(("arbitrary",)), name="ple")(*args)
    return (outs[0], outs[1]) if final else (outs[0], None)


ATTN_GROUP = 4
LOG2E = 1.4426950408889634


def _attn_group(qs, kcats, vcats, biases):
    R = CHUNK
    s = jnp.concatenate([_dot_nt(q, kcat) * (HEAD_DIM ** -0.5 * LOG2E) + bias
                         for q, kcat, bias in zip(qs, kcats, biases)], axis=0)
    m = jnp.max(s, axis=-1, keepdims=True)
    p = jnp.exp2(s - m)
    l = jnp.sum(p, axis=-1, keepdims=True)
    pb = p.astype(BF16)
    inv = 1.0 / l
    lse = m * (1.0 / LOG2E) + jnp.log(l)
    outs, lses = [], []
    for i, vcat in enumerate(vcats):
        sl = slice(i * R, (i + 1) * R)
        outs.append(_dot(pb[sl], vcat) * inv[sl])
        lses.append(jnp.broadcast_to(lse[sl], (R, LANES)))
    return outs, lses


def _attn_body(q0_ref, q1_ref, q2_ref, k_ref, v_ref, o_ref, kp_ref, vp_ref, og_ref, lg_ref, bias_ref):
    R = CHUNK
    row = _iota((R, 2 * R), 0)
    col = _iota((R, 2 * R), 1)
    first_key = jnp.where(pl.program_id(2) > 0, 0, R)
    bias_ref[0] = jnp.where((col >= row) & (col <= row + R), 0.0, -jnp.inf)
    bias_ref[1] = jnp.where((col >= jnp.maximum(row, first_key)) & (col <= row + R), 0.0, -jnp.inf)

    @pl.when(pl.program_id(2) == 0)
    def _():
        kp_ref[...] = jnp.zeros_like(kp_ref)
        vp_ref[...] = jnp.zeros_like(vp_ref)

    def rows(ref, start, dil):
        idx = pl.ds(start, R) if dil == 1 else pl.ds(start, R, stride=dil)
        return ref[idx, :]

    def put(ref, start, dil, val):
        idx = pl.ds(start, R) if dil == 1 else pl.ds(start, R, stride=dil)
        ref[idx, :] = val

    def group(g, q_ref, dil, blocks):
        nblk = ATTN_SPAN // (R * dil)
        qs, kcats, vcats, fks, starts = [], [], [], [], []
        for r, cb in blocks:
            start = r + dil * R * cb
            if cb == 0:
                prev_k, prev_v, prev_start, fk = kp_ref, vp_ref, r + dil * R * (nblk - 1), bias_ref[1]
            else:
                prev_k, prev_v, prev_start, fk = k_ref, v_ref, start - dil * R, bias_ref[0]
            kcats.append(jnp.concatenate([rows(prev_k, prev_start, dil).astype(BF16),
                                          rows(k_ref, start, dil).astype(BF16)], axis=0))
            vcats.append(jnp.concatenate([rows(prev_v, prev_start, dil).astype(BF16),
                                          rows(v_ref, start, dil).astype(BF16)], axis=0))
            qs.append(rows(q_ref, start, dil).astype(BF16))
            fks.append(fk)
            starts.append(start)
        outs, lses = _attn_group(qs, kcats, vcats, fks)
        for start, o, lse in zip(starts, outs, lses):
            put(og_ref.at[g], start, dil, o)
            put(lg_ref.at[g], start, dil, lse)

    for g, (q_ref, (_, dil)) in enumerate(zip((q0_ref, q1_ref, q2_ref), PATTERNS)):
        nblk = ATTN_SPAN // (R * dil)
        if nblk >= ATTN_GROUP:
            def lane(r, carry, q_ref=q_ref, dil=dil, nblk=nblk, g=g):
                for c0 in range(0, nblk, ATTN_GROUP):
                    group(g, q_ref, dil, [(r, cb) for cb in range(c0, c0 + ATTN_GROUP)])
                return carry
            n_iter = dil
        else:
            def lane(i, carry, q_ref=q_ref, dil=dil, nblk=nblk, g=g):
                for cb in range(nblk):
                    group(g, q_ref, dil, [(i * ATTN_GROUP + j, cb) for j in range(ATTN_GROUP)])
                return carry
            n_iter = dil // ATTN_GROUP
        if n_iter == 1:
            lane(0, 0)
        else:
            lax.fori_loop(0, n_iter, lane, 0)

    step = 256
    for i in range(ATTN_SPAN // step):
        sl = pl.ds(i * step, step)
        l0 = lg_ref[0, sl, :]
        l1 = lg_ref[1, sl, :]
        l2 = lg_ref[2, sl, :]
        m = jnp.maximum(jnp.maximum(l0, l1), l2)
        e0 = jnp.exp(l0 - m)
        e1 = jnp.exp(l1 - m)
        e2 = jnp.exp(l2 - m)
        num = e0 * og_ref[0, sl, :] + e1 * og_ref[1, sl, :] + e2 * og_ref[2, sl, :]
        o_ref[sl, :] = num * (1.0 / (e0 + e1 + e2))
    kp_ref[...] = k_ref[...]
    vp_ref[...] = v_ref[...]


def _attn_prompt(q, k, v, b, s):
    nspan = s // ATTN_SPAN
    blk = (ATTN_SPAN, HEAD_DIM)
    hblk = (None,) + blk
    qspec = lambda g: pl.BlockSpec(hblk, lambda bi, h, c: (g * N_KV_HEADS + h, bi * nspan + c, 0))
    kvspec = pl.BlockSpec(hblk, lambda bi, h, c: (h, bi * nspan + c, 0))
    return pl.pallas_call(
        _attn_body, grid=(b, N_KV_HEADS, nspan),
        in_specs=[qspec(0), qspec(1), qspec(2), kvspec, kvspec],
        out_specs=kvspec, out_shape=jax.ShapeDtypeStruct((N_KV_HEADS, b * s, HEAD_DIM), F32),
        scratch_shapes=[pltpu.VMEM(blk, F32), pltpu.VMEM(blk, F32),
                        pltpu.VMEM((3,) + blk, F32), pltpu.VMEM((3,) + blk, F32),
                        pltpu.VMEM((2, CHUNK, 2 * CHUNK), F32)],
        compiler_params=_params(("parallel", "parallel", "arbitrary")), name="attn_prompt")(q, q, q, k, v)


SAMPLE_ROWS = 16
DIL_MAX = PATTERNS[-1][1]
DENSE_TAIL = PATTERNS[1][0]


def _attn_sample_body(q_ref, kf_ref, vf_ref, kt_ref, vt_ref, kn_ref, vn_ref, bias_ref, o_ref, *, t_new):
    nq = N_KV_HEADS * SAMPLE_ROWS
    n_far = kf_ref.shape[1] * kf_ref.shape[2]
    n_tail = kt_ref.shape[1]
    q = q_ref[0].astype(BF16)
    kf = kf_ref[0].reshape(n_far, HEAD_DIM).astype(BF16)
    vf = vf_ref[0].reshape(n_far, HEAD_DIM).astype(BF16)
    s = jnp.concatenate([_dot_nt(q, kf), _dot_nt(q, kt_ref[0].astype(BF16)), _dot_nt(q, kn_ref[0].astype(BF16))],
                        axis=1) * (HEAD_DIM ** -0.5) + bias_ref[...]
    m = jnp.max(s, axis=-1, keepdims=True)
    p = jnp.exp(s - m)
    l = jnp.sum(p, axis=-1, keepdims=True)
    pb = p.astype(BF16)
    o = (_dot(pb[:, :n_far], vf) + _dot(pb[:, n_far:n_far + n_tail], vt_ref[0].astype(BF16))
         + _dot(pb[:, n_far + n_tail:], vn_ref[0].astype(BF16))) * (1.0 / l)
    lse = jnp.broadcast_to(m + jnp.log(l), (nq, LANES))
    lse1 = pltpu.roll(lse, nq - t_new, 0)
    lse2 = pltpu.roll(lse, nq - 2 * t_new, 0)
    o1 = pltpu.roll(o, nq - t_new, 0)
    o2 = pltpu.roll(o, nq - 2 * t_new, 0)
    mm = jnp.maximum(jnp.maximum(lse, lse1), lse2)
    e0 = jnp.exp(lse - mm)
    e1 = jnp.exp(lse1 - mm)
    e2 = jnp.exp(lse2 - mm)
    comb = (e0 * o + e1 * o1 + e2 * o2) * (1.0 / (e0 + e1 + e2))
    for h in range(N_KV_HEADS):
        o_ref[0, :, h * HEAD_DIM:(h + 1) * HEAD_DIM] = comb[h * SAMPLE_ROWS:h * SAMPLE_ROWS + t_new]


def _sample_bias(wb, t_new, far):
    rho = jnp.arange(N_KV_HEADS * SAMPLE_ROWS, dtype=I32)
    h = rho // SAMPLE_ROWS
    gt = rho % SAMPLE_ROWS
    g = jnp.where(gt < len(PATTERNS) * t_new, gt // t_new, 0)
    t = gt % t_new
    win = jnp.asarray([w for w, _ in PATTERNS], I32)[g]
    dil = jnp.asarray([d for _, d in PATTERNS], I32)[g]
    far_rows = t_new * N_KV_HEADS
    c = jnp.arange(far // DIL_MAX * far_rows, dtype=I32)
    pos_far = DIL_MAX * (c // far_rows) + (c % far_rows) // N_KV_HEADS
    c = jnp.arange((wb - far) * N_KV_HEADS, dtype=I32)
    pos_tail = far + c // N_KV_HEADS
    c = jnp.arange(LANES, dtype=I32)
    pos_new = jnp.where(c < far_rows, wb + c // N_KV_HEADS, -1)
    pos = jnp.concatenate([pos_far, pos_tail, pos_new])
    head = jnp.concatenate([jnp.arange(n, dtype=I32) % N_KV_HEADS for n in (pos_far.shape[0], pos_tail.shape[0], LANES)])
    dist = wb + t[:, None] - pos[None, :]
    valid = ((head[None, :] == h[:, None]) & (pos[None, :] >= 0) & (dist >= 0) & (dist <= win[:, None])
             & (dist % dil[:, None] == 0))
    return jnp.where(valid, 0.0, -jnp.inf).astype(F32)


def _attn_sample(q, cache_k, cache_v, k_new, v_new, b, t_new):
    wb = cache_k.shape[1]
    tail = min(DENSE_TAIL, wb)
    far = wb - tail
    far_rows = t_new * N_KV_HEADS
    nq = N_KV_HEADS * SAMPLE_ROWS
    qh = q.reshape(b, t_new, 3, N_KV_HEADS, HEAD_DIM).transpose(0, 3, 2, 1, 4).reshape(b, N_KV_HEADS, 3 * t_new, HEAD_DIM)
    qh = jnp.pad(qh, ((0, 0), (0, 0), (0, SAMPLE_ROWS - 3 * t_new), (0, 0))).reshape(b, nq, HEAD_DIM)
    new = lambda x: jnp.pad(x.reshape(b, far_rows, HEAD_DIM), ((0, 0), (0, LANES - far_rows), (0, 0)))
    grouped = lambda x: x.reshape(b, wb // DIL_MAX, DIL_MAX * N_KV_HEADS, HEAD_DIM)
    flat = lambda x: x.reshape(b, wb * N_KV_HEADS, HEAD_DIM)
    bias = _sample_bias(wb, t_new, far)
    far_spec = pl.BlockSpec((1, far // DIL_MAX, far_rows, HEAD_DIM), lambda bi: (bi, 0, 0, 0))
    tail_spec = pl.BlockSpec((1, tail * N_KV_HEADS, HEAD_DIM), lambda bi: (bi, wb // tail - 1, 0))
    sq_spec = pl.BlockSpec((1, LANES, HEAD_DIM), lambda bi: (bi, 0, 0))
    o = pl.pallas_call(
        functools.partial(_attn_sample_body, t_new=t_new), grid=(b,),
        in_specs=[sq_spec, far_spec, far_spec, tail_spec, tail_spec, sq_spec, sq_spec,
                  pl.BlockSpec(bias.shape, lambda bi: (0, 0))],
        out_specs=pl.BlockSpec((1, t_new, ATTN_DIM), lambda bi: (bi, 0, 0)),
        out_shape=jax.ShapeDtypeStruct((b, t_new, ATTN_DIM), F32),
        compiler_params=_params(("parallel",)), name="attn_sample")(
            qh, grouped(cache_k), grouped(cache_v), flat(cache_k), flat(cache_v), new(k_new), new(v_new), bias)
    return o.reshape(b * t_new, ATTN_DIM)


def _rope_tables(pos0, l, reps):
    inv_freq = ROPE_THETA ** (-jnp.arange(0, HEAD_DIM, 2, dtype=F32) / HEAD_DIM)
    pos = pos0 + jnp.arange(l, dtype=jnp.int32)
    ang = pos.astype(F32)[:, None] * inv_freq[None, :]
    cos = jnp.cos(ang)
    sin = jnp.sin(ang)
    cosf = jnp.concatenate([cos, cos], axis=-1)
    sinf = jnp.concatenate([-sin, sin], axis=-1)
    if reps > 1:
        cosf = jnp.tile(cosf, (reps, 1))
        sinf = jnp.tile(sinf, (reps, 1))
    return cosf, sinf


def _prep_weights(prm):
    w = {}
    w['in_proj'], w['in_proj_lo'] = _split2(prm['ssm_in_proj'])
    w['in_dt'], w['in_dt_lo'] = _split2(
        jnp.pad(prm['ssm_in_proj'][:, :, ZXBC_DIM:], ((0, 0), (0, 0), (0, LANES - SSM_HEADS))))
    w['out_proj'], w['out_proj_lo'] = _split2(prm['ssm_out_proj'])
    w['w_kv'] = prm['w_kv'].astype(BF16)
    w['w_q'] = prm['attn_w_q'].astype(BF16)
    w['w_o'] = prm['attn_w_o'].astype(BF16)
    w['ple_gate'] = prm['ple_w_gate'].astype(BF16)
    w['ple_proj'] = prm['ple_w_proj'].astype(BF16)
    wr = jnp.concatenate([jnp.swapaxes(prm['moe_w_rg'], 1, 2),
                          jnp.zeros((DEPTH, 8 - N_EXPERT_GROUPS, D_MODEL), F32),
                          jnp.swapaxes(prm['moe_w_re'], 1, 2)], axis=1)
    w['router'] = jnp.stack(_split3(wr), axis=1)
    w['router_b'] = jnp.concatenate([prm['moe_b_rg'], jnp.zeros((DEPTH, 8 - N_EXPERT_GROUPS), F32),
                                     prm['moe_b_re']], axis=1).reshape(DEPTH, ROUTER_ROWS, 1)
    return w


def _trunk_state(x, p, pos0, conv_state, ssm_state, cache_k, cache_v):
    b, l, _ = x.shape
    n = b * l
    tm = min(512, n)
    return dict(b=b, l=l, n=n, tm=tm, tp=min(PROJ_ROWS, n), sample=cache_k is not None, xf=x.reshape(n, D_MODEL),
                p=p.reshape(DEPTH, n, -1), rope=_rope_tables(pos0, l, 1 if l >= tm else tm // l),
                conv_state=conv_state, ssm_state=ssm_state, cache_k=cache_k, cache_v=cache_v,
                new_conv=[], new_ssm=[], k32=None, v32=None, kh=None, vh=None, y_final=None)


def _mixer(t, i, prm, w):
    b, l, n, tm, tp, xf = t['b'], t['l'], t['n'], t['tm'], t['tp'], t['xf']
    gain = prm['norm_mix'][i]
    if i < N_A_LAYERS:
        precise = i < PRECISE_MIXERS
        lo = (lambda name: w[name + '_lo']) if precise else (lambda name: None)
        zx = _proj(xf, w['in_proj'], w_lo=lo('in_proj'), layer=i, gain=gain, nout=ZXBC_DIM, tm=tp, tn=1024)
        dtr = _proj(xf, w['in_dt'], w_lo=lo('in_dt'), layer=i, gain=gain, tm=tp, tn=LANES)
        y, cbuf, hs = _ssd(zx.reshape(b, l, ZXBC_DIM), dtr.reshape(b, l, LANES), prm['ssm_conv_w'][i],
                           prm['ssm_conv_b'][i], prm['ssm_dt_bias'][i], prm['ssm_a_log'][i], prm['ssm_d'][i],
                           prm['ssm_norm_y'][i], t['conv_state'][i], t['ssm_state'], i, precise)
        t['new_conv'].append(cbuf)
        t['new_ssm'].append(hs)
        t['xf'] = _proj(y.reshape(n, D_INNER), w['out_proj'], w_lo=lo('out_proj'), layer=i, residual=xf,
                        tm=tm, tn=1024)
    else:
        j = i - N_A_LAYERS
        q = _proj(xf, w['w_q'], layer=j, gain=gain, rope=t['rope'], out_heads=(not t['sample'],), tm=tp, tn=1024)
        if t['sample']:
            att = _attn_sample(q, t['cache_k'], t['cache_v'], t['k32'], t['v32'], b, l)
        else:
            att = _attn_prompt(q, t['kh'], t['vh'], b, l)
        t['xf'] = _proj(att, w['w_o'], layer=j, residual=xf, tm=tp, tn=1024)


def _shared_kv(t, prm, w):
    tm, xf = t['tp'], t['xf']
    kv_out = dict(out_dtypes=(F32,), out_heads=(False,)) if t['sample'] else \
        dict(out_dtypes=(F32, F32), out_heads=(False, True))
    k_outs = _proj(xf, w['w_kv'], gain=prm['norm_kv'], rope=t['rope'], nout=ATTN_DIM, tm=tm, tn=1024, **kv_out)
    v_outs = _proj(xf, w['w_kv'], col0=ATTN_DIM, gain=prm['norm_kv'], nout=ATTN_DIM, tm=tm, tn=1024, **kv_out)
    (t['k32'], t['kh']), (t['v32'], t['vh']) = ((k_outs, None), (v_outs, None)) if t['sample'] else (k_outs, v_outs)


def _moe_ple(trunks, i, prm, w):
    routed = [_router(t['xf'], prm['norm_ffn'][i], w['router'], w['router_b'], i, t['tm']) for t in trunks]
    dests, tile_e, n_used, slots, pad_lo, pad_hi = _moe_plan([r[1] for r in routed], [r[2] for r in routed],
                                                             MOE_TILE, [t['tm'] for t in trunks])
    xs = None
    for t, (xn, _, _), dest in zip(trunks, routed, dests):
        xs = _dispatch(xn, dest, slots, t['tm'], xs)
    xs = _pad_zero(xs, pad_lo, pad_hi)
    ys = _moe_experts(xs.reshape(slots * ROW_TILES, LANES), tile_e, n_used, prm['moe_w_gate'], prm['moe_w_up'],
                      prm['moe_w_down'], i, MOE_TILE)
    for t, (_, slab, _), dest in zip(trunks, routed, dests):
        t['xf'], t['y_final'] = _ple(t['xf'], ys, dest, slab.T, t['p'], prm['ple_norm'][i], w['ple_gate'],
                                     w['ple_proj'], i, prm['norm_final'] if i == DEPTH - 1 else None, t['tm'])


def _run(trunks, prm, w):
    for i in range(DEPTH):
        for t in trunks:
            _mixer(t, i, prm, w)
        _moe_ple(trunks, i, prm, w)
        if i == N_A_LAYERS - 1:
            for t in trunks:
                _shared_kv(t, prm, w)
    return [(t['y_final'].reshape(t['b'], t['l'], D_MODEL), jnp.stack(t['new_conv']), jnp.stack(t['new_ssm']),
             t['k32'].reshape(t['b'], t['l'], ATTN_DIM), t['v32'].reshape(t['b'], t['l'], ATTN_DIM))
            for t in trunks]


def kernel(x_prompt, x_sample, state_ssm, state_conv, cache_k, cache_v, p_prompt, p_sample, norm_mix, ssm_in_proj, ssm_conv_w, ssm_conv_b, ssm_dt_bias, ssm_a_log, ssm_d, ssm_norm_y, ssm_out_proj, norm_kv, w_kv, attn_w_q, attn_w_o, norm_ffn, moe_w_rg, moe_b_rg, moe_w_re, moe_b_re, moe_w_gate, moe_w_up, moe_w_down, ple_w_proj, ple_w_gate, ple_norm, norm_final):
    prm = dict(norm_mix=norm_mix, ssm_in_proj=ssm_in_proj, ssm_conv_w=ssm_conv_w, ssm_conv_b=ssm_conv_b,
               ssm_dt_bias=ssm_dt_bias, ssm_a_log=ssm_a_log, ssm_d=ssm_d, ssm_norm_y=ssm_norm_y,
               ssm_out_proj=ssm_out_proj, norm_kv=norm_kv, w_kv=w_kv, attn_w_q=attn_w_q, attn_w_o=attn_w_o,
               norm_ffn=norm_ffn, moe_w_rg=moe_w_rg, moe_b_rg=moe_b_rg, moe_w_re=moe_w_re, moe_b_re=moe_b_re,
               moe_w_gate=moe_w_gate, moe_w_up=moe_w_up, moe_w_down=moe_w_down, ple_w_proj=ple_w_proj,
               ple_w_gate=ple_w_gate, ple_norm=ple_norm, norm_final=norm_final)
    w = _prep_weights(prm)
    b, s, _ = x_prompt.shape
    db, dl, _ = x_sample.shape
    past_len = 8192
    conv0 = jnp.zeros((N_A_LAYERS, b, 3, CONV_DIM), x_prompt.dtype)
    ssm0 = jnp.zeros((N_A_LAYERS, b, SSM_HEADS, SSM_HEADDIM, D_STATE), state_ssm.dtype)
    trunks = [_trunk_state(x_prompt, p_prompt, 0, conv0, ssm0, None, None),
              _trunk_state(x_sample, p_sample, past_len, state_conv, state_ssm, cache_k, cache_v)]
    (y_p, conv_p, ssm_p, k_p, v_p), (y_s, conv_s, ssm_s, k_s, v_s) = _run(trunks, prm, w)
    keep = min(2048, s)
    heads = lambda t: t.reshape(t.shape[0], t.shape[1], N_KV_HEADS, HEAD_DIM)
    k_p = heads(k_p[:, s - keep:])
    v_p = heads(v_p[:, s - keep:])
    return (y_p, y_s, ssm_p, conv_p, k_p, v_p, ssm_s, conv_s, heads(k_s), heads(v_s))
```

```python
import functools

import jax
import jax.numpy as jnp
from jax import lax
from jax.experimental import pallas as pl
from jax.experimental.pallas import tpu as pltpu

F32 = jnp.float32
BF16 = jnp.bfloat16
I32 = jnp.int32

EPS = 1e-6
D_MODEL = 1024
D_INNER = 2048
SSM_HEADS = 32
SSM_HEADDIM = 64
SSM_GROUPS = 4
D_STATE = 128
CONV_DIM = 3072
ZXBC_DIM = D_INNER + CONV_DIM
CHUNK = 128
HEAD_DIM = 128
N_KV_HEADS = 8
ATTN_DIM = 1024
PATTERNS = ((128, 1), (512, 4), (2048, 16))
ATTN_SPAN = 2048
ROPE_THETA = 10000.0
N_EXPERTS = 32
N_EXPERT_GROUPS = 4
EXPERTS_PER_GROUP = 8
D_EXPERT = 512
N_A_LAYERS = 2
DEPTH = 4
PRECISE_MIXERS = 1
MOE_TILE = 256
PROJ_ROWS = 1024
ROUTER_ROWS = 8 + N_EXPERTS
ROUTER_W_ROWS = 128
LANES = 128
SUBLANES = 8
ROW_TILES = D_MODEL // LANES
VMEM_LIMIT = 56 * 1024 * 1024


def _params(sem):
    return pltpu.CompilerParams(dimension_semantics=sem, vmem_limit_bytes=VMEM_LIMIT)


def _iota(shape, dim):
    return lax.broadcasted_iota(I32, shape, dim)


def _rms(x, gain):
    ms = jnp.mean(x * x, axis=-1, keepdims=True)
    return x * lax.rsqrt(ms + EPS) * gain


def _sigmoid(x):
    return 0.5 * jnp.tanh(0.5 * x) + 0.5


def _silu(x):
    return x * _sigmoid(x)


def _dot(a, b):
    return jnp.dot(a, b, preferred_element_type=F32)


def _dot_nt(a, b):
    return lax.dot_general(a, b, (((1,), (1,)), ((), ())), preferred_element_type=F32)


def _split3(x):
    hi = x.astype(BF16)
    r = x - hi.astype(F32)
    mid = r.astype(BF16)
    lo = (r - mid.astype(F32)).astype(BF16)
    return hi, mid, lo


def _split2(x):
    hi = x.astype(BF16)
    return hi, (x - hi.astype(F32)).astype(BF16)


def _mm(a, b, dot):
    y = dot(a[0], b[0])
    if len(a) > 1:
        y = y + dot(a[1], b[0]) + dot(a[0], b[1])
    return y


def _rows_to_tiles(ref, x):
    m = x.shape[0]
    for s in range(ROW_TILES):
        ref[pl.ds(s, m, stride=ROW_TILES), :] = x[:, s * LANES:(s + 1) * LANES]


def _tiles_to_rows(ref, m):
    return jnp.concatenate([ref[pl.ds(s, m, stride=ROW_TILES), :] for s in range(ROW_TILES)], axis=1)


def _proj_body(*refs, has_norm, has_rope, has_res, x_heads, out_heads, precise):
    it = iter(refs)
    x_ref = next(it)
    w_ref = next(it)
    wlo_ref = next(it) if precise else None
    g_ref = next(it) if has_norm else None
    cos_ref = next(it) if has_rope else None
    sin_ref = next(it) if has_rope else None
    r_ref = next(it) if has_res else None
    o_refs = [next(it) for _ in out_heads]
    xb_ref = next(it)
    xlo_ref = next(it) if precise else None

    @pl.when(pl.program_id(1) == 0)
    def _():
        if x_heads:
            x = jnp.concatenate([x_ref[h] for h in range(x_ref.shape[0])], axis=1).astype(F32)
        else:
            x = x_ref[...].astype(F32)
        if has_norm:
            x = _rms(x, g_ref[...])
        if precise:
            xb_ref[...], xlo_ref[...] = _split2(x)
        else:
            xb_ref[...] = x.astype(BF16)

    if precise:
        y = _mm((xb_ref[...], xlo_ref[...]), (w_ref[...], wlo_ref[...]), _dot)
    else:
        y = _dot(xb_ref[...], w_ref[...])
    if has_rope:
        cos = cos_ref[...]
        sin = sin_ref[...]
        parts = []
        for h in range(y.shape[1] // HEAD_DIM):
            t = y[:, h * HEAD_DIM:(h + 1) * HEAD_DIM]
            parts.append(t * cos + pltpu.roll(t, HEAD_DIM // 2, 1) * sin)
        y = parts[0] if len(parts) == 1 else jnp.concatenate(parts, axis=1)
    if has_res:
        y = y + r_ref[...]
    for o, heads in zip(o_refs, out_heads):
        if heads:
            for h in range(o.shape[0]):
                o[h] = y[:, h * HEAD_DIM:(h + 1) * HEAD_DIM].astype(o.dtype)
        else:
            o[...] = y.astype(o.dtype)


def _proj(x, w, *, w_lo=None, layer=None, col0=0, gain=None, rope=None, residual=None, out_dtypes=(F32,),
          out_heads=None, nout=None, tm, tn):
    x_heads = x.ndim == 3
    n = x.shape[-2]
    k = x.shape[0] * x.shape[2] if x_heads else x.shape[1]
    nout = w.shape[-1] if nout is None else nout
    tn = min(tn, nout)
    out_heads = (False,) * len(out_dtypes) if out_heads is None else out_heads
    grid = (n // tm, nout // tn)
    jb = col0 // tn
    precise = w_lo is not None
    wspec = (pl.BlockSpec((k, tn), lambda i, j: (0, j + jb)) if layer is None else
             pl.BlockSpec((None, k, tn), lambda i, j: (layer, 0, j + jb)))
    in_specs = [pl.BlockSpec((k // HEAD_DIM, tm, HEAD_DIM), lambda i, j: (0, i, 0)) if x_heads else
                pl.BlockSpec((tm, k), lambda i, j: (i, 0)), wspec]
    args = [x, w]
    if precise:
        in_specs.append(wspec)
        args.append(w_lo)
    if gain is not None:
        in_specs.append(pl.BlockSpec((1, k), lambda i, j: (0, 0)))
        args.append(gain.reshape(1, k))
    if rope is not None:
        cos, sin = rope
        nblk = cos.shape[0] // tm
        for t in (cos, sin):
            in_specs.append(pl.BlockSpec((tm, HEAD_DIM), lambda i, j: (i % nblk, 0)))
            args.append(t)
    if residual is not None:
        in_specs.append(pl.BlockSpec((tm, tn), lambda i, j: (i, j)))
        args.append(residual)
    out_shape = [jax.ShapeDtypeStruct((nout // HEAD_DIM, n, HEAD_DIM) if hm else (n, nout), dt)
                 for dt, hm in zip(out_dtypes, out_heads)]
    out_specs = [pl.BlockSpec((tn // HEAD_DIM, tm, HEAD_DIM), lambda i, j: (j, i, 0)) if hm else
                 pl.BlockSpec((tm, tn), lambda i, j: (i, j)) for hm in out_heads]
    body = functools.partial(_proj_body, has_norm=gain is not None, has_rope=rope is not None,
                             has_res=residual is not None, x_heads=x_heads, out_heads=tuple(out_heads),
                             precise=precise)
    outs = pl.pallas_call(
        body, grid=grid, in_specs=in_specs, out_specs=out_specs, out_shape=out_shape,
        scratch_shapes=[pltpu.VMEM((tm, k), BF16)] * (2 if precise else 1),
        compiler_params=_params(("parallel", "arbitrary")), name="proj")(*args)
    return outs[0] if len(outs) == 1 else outs


def _ssd_body(zx_ref, dt_ref, cw_ref, cb_ref, dtb_ref, alog_ref, d_ref, ny_ref, conv0_ref, ssm0_ref,
              y_ref, convo_ref, ssmo_ref, st_ref, tail_ref, xbc_ref, yacc_ref, *pad_refs, valid, nchunks, precise):
    L = CHUNK
    c = pl.program_id(1)
    npair = SSM_HEADS // 2
    parts = _split2 if precise else (lambda x: (x.astype(BF16),))

    @pl.when(c == 0)
    def _():
        for j in range(npair):
            st_ref[:, j * LANES:(j + 1) * LANES] = ssm0_ref[0, j * LANES:(j + 1) * LANES, :].T
        tail_ref[0:8, :] = conv0_ref[0]

    if valid < L:
        pzx_ref, pdt_ref = pad_refs
        pzx_ref[...] = jnp.zeros_like(pzx_ref)
        pdt_ref[...] = jnp.zeros_like(pdt_ref)
        pzx_ref[0:valid, :] = zx_ref[0]
        pdt_ref[0:valid, :] = dt_ref[0]
        z = pzx_ref[:, :D_INNER]
        u = pzx_ref[:, D_INNER:]
        dtr = pdt_ref[...]
    else:
        z = zx_ref[0, :, :D_INNER]
        u = zx_ref[0, :, D_INNER:]
        dtr = dt_ref[0]

    tail_ref[8:8 + L, :] = u
    acc = cb_ref[...] + cw_ref[3:4, :] * u
    for k in range(1, 4):
        acc = acc + cw_ref[3 - k:4 - k, :] * tail_ref[8 - k:8 - k + L, :]
    tail_ref[0:8, :] = tail_ref[valid:valid + 8, :]
    xbc_ref[...] = _silu(acc)

    lane = _iota((1, LANES), 1)
    rowl = _iota((L, 1), 0)
    xdt = dtr + dtb_ref[...]
    dt = jnp.maximum(xdt, 0.0) + jnp.log(1.0 + jnp.exp(-jnp.abs(xdt)))
    dt = jnp.where((lane < SSM_HEADS) & (rowl < valid), dt, 0.0)
    a_neg = -jnp.exp(alog_ref[...])
    da = dt * a_neg
    rr = _iota((L, L), 0)
    cc = _iota((L, L), 1)
    causal = rr >= cc
    tri = jnp.where(causal, 1.0, 0.0).astype(BF16)
    hi, mid, lo = _split3(da)
    a_cs = _dot(tri, hi) + _dot(tri, mid) + _dot(tri, lo)
    a_cs_t = a_cs.T
    dt_t = dt.T
    tdt_t = dt_t * jnp.exp(a_cs_t[:, L - 1:L] - a_cs_t)
    ea = jnp.exp(a_cs)
    half = _iota((1, LANES), 1) < SSM_HEADDIM

    for g in range(SSM_GROUPS):
        b_g = xbc_ref[:, D_INNER + g * D_STATE:D_INNER + (g + 1) * D_STATE]
        c_g = xbc_ref[:, D_INNER + SSM_GROUPS * D_STATE + g * D_STATE:
                      D_INNER + SSM_GROUPS * D_STATE + (g + 1) * D_STATE]
        cb = _mm(parts(c_g), parts(b_g), _dot_nt)
        b_t = b_g.T
        for jj in range(npair // SSM_GROUPS):
            j = g * (npair // SSM_GROUPS) + jj
            cols = slice(j * LANES, (j + 1) * LANES)
            xs_b = parts(xbc_ref[:, cols])
            st_p = st_ref[:, cols]
            rhs = tuple(jnp.concatenate([a, b], axis=0) for a, b in zip(xs_b, parts(st_p)))
            ys, ss, cds = [], [], []
            for h in (2 * j, 2 * j + 1):
                colb = jnp.broadcast_to(a_cs[:, h:h + 1], (L, LANES))
                seg = colb - a_cs_t[h:h + 1, :]
                dec = jnp.exp(jnp.where(causal, seg, -jnp.inf))
                m = cb * dec * dt_t[h:h + 1, :]
                ecol = jnp.broadcast_to(ea[:, h:h + 1], (L, LANES))
                ce = c_g * ecol
                lhs = tuple(jnp.concatenate([a, b], axis=1) for a, b in zip(parts(m), parts(ce)))
                ys.append(_mm(lhs, rhs, _dot))
                ss.append(_mm(parts(b_t * tdt_t[h:h + 1, :]), xs_b, _dot))
                cds.append(ecol[L - 1:L, :])
            yacc_ref[:, cols] = jnp.where(half, ys[0], ys[1])
            st_ref[:, cols] = jnp.where(half, st_p * cds[0] + ss[0], st_p * cds[1] + ss[1])

    y = yacc_ref[...] + d_ref[...] * xbc_ref[:, :D_INNER]
    yn = _rms(y * _silu(z), ny_ref[...])
    if valid < L:
        y_ref[0] = yn[0:valid].astype(y_ref.dtype)
    else:
        y_ref[0] = yn.astype(y_ref.dtype)

    @pl.when(c == nchunks - 1)
    def _():
        convo_ref[0] = tail_ref[0:8, :]
        for j in range(npair):
            ssmo_ref[0, j * LANES:(j + 1) * LANES, :] = st_ref[:, j * LANES:(j + 1) * LANES].T


def _ssd(zx, dtr, conv_w, conv_b, dt_bias, a_log, d_skip, norm_y, conv0, ssm0, layer, precise):
    b, l, _ = zx.shape
    if l >= CHUNK:
        lb, valid, nchunks = CHUNK, CHUNK, l // CHUNK
    else:
        lb, valid, nchunks = l, l, 1
    pad_lanes = LANES - SSM_HEADS
    dtb = jnp.pad(dt_bias.astype(F32), (0, pad_lanes)).reshape(1, LANES)
    alog = jnp.pad(a_log.astype(F32), (0, pad_lanes)).reshape(1, LANES)
    d_exp = jnp.repeat(d_skip.astype(F32), SSM_HEADDIM).reshape(1, D_INNER)
    conv0p = jnp.pad(conv0, ((0, 0), (5, 0), (0, 0)))
    ssm0f = ssm0.reshape(ssm0.shape[0], b, SSM_HEADS * SSM_HEADDIM, D_STATE)
    const = lambda bi, c: (0, 0)
    in_specs = [
        pl.BlockSpec((1, lb, ZXBC_DIM), lambda bi, c: (bi, c, 0)),
        pl.BlockSpec((1, lb, LANES), lambda bi, c: (bi, c, 0)),
        pl.BlockSpec((4, CONV_DIM), const),
        pl.BlockSpec((1, CONV_DIM), const),
        pl.BlockSpec((1, LANES), const),
        pl.BlockSpec((1, LANES), const),
        pl.BlockSpec((1, D_INNER), const),
        pl.BlockSpec((1, D_INNER), const),
        pl.BlockSpec((1, 8, CONV_DIM), lambda bi, c: (bi, 0, 0)),
        pl.BlockSpec((None, 1, D_INNER, D_STATE), lambda bi, c: (layer, bi, 0, 0)),
    ]
    out_shape = [jax.ShapeDtypeStruct((b, l, D_INNER), F32 if precise else BF16),
                 jax.ShapeDtypeStruct((b, 8, CONV_DIM), F32),
                 jax.ShapeDtypeStruct((b, D_INNER, D_STATE), F32)]
    out_specs = [pl.BlockSpec((1, lb, D_INNER), lambda bi, c: (bi, c, 0)),
                 pl.BlockSpec((1, 8, CONV_DIM), lambda bi, c: (bi, 0, 0)),
                 pl.BlockSpec((1, D_INNER, D_STATE), lambda bi, c: (bi, 0, 0))]
    scratch = [pltpu.VMEM((D_STATE, D_INNER), F32), pltpu.VMEM((8 + CHUNK, CONV_DIM), F32),
               pltpu.VMEM((CHUNK, CONV_DIM), F32), pltpu.VMEM((CHUNK, D_INNER), F32)]
    if valid < CHUNK:
        scratch += [pltpu.VMEM((CHUNK, ZXBC_DIM), F32), pltpu.VMEM((CHUNK, LANES), F32)]
    y, convo, ssmo = pl.pallas_call(
        functools.partial(_ssd_body, valid=valid, nchunks=nchunks, precise=precise),
        grid=(b, nchunks), in_specs=in_specs, out_specs=out_specs, out_shape=out_shape,
        scratch_shapes=scratch, compiler_params=_params(("parallel", "arbitrary")), name="ssd")(
            zx, dtr, conv_w, conv_b.reshape(1, CONV_DIM), dtb, alog, d_exp, norm_y.reshape(1, D_INNER),
            conv0p, ssm0f)
    return y, convo[:, 5:], ssmo.reshape(b, SSM_HEADS, SSM_HEADDIM, D_STATE)


def _router_body(x_ref, g_ref, w_ref, b_ref, xn_ref, slab_ref, cnt_ref, base_ref, su_ref):
    tm = x_ref.shape[0]

    @pl.when(pl.program_id(0) == 0)
    def _():
        base_ref[...] = jnp.zeros_like(base_ref)
        su_ref[...] = jnp.where(_iota((tm, tm), 0) < _iota((tm, tm), 1), 1.0, 0.0).astype(BF16)

    xn = _rms(x_ref[...], g_ref[...])
    _rows_to_tiles(xn_ref, xn)
    xh, xm, xl = _split3(xn)
    w3 = w_ref[...]
    rh = _dot_nt(w3, xh)
    rm = _dot_nt(w3, xm)
    rl = _dot_nt(w3, xl)
    r1, r2, r3 = ROUTER_ROWS, 2 * ROUTER_ROWS, 3 * ROUTER_ROWS
    logits = (rh[0:r1] + rm[0:r1] + rh[r1:r2] + rl[0:r1] + rm[r1:r2] + rh[r2:r3]) + b_ref[...]
    gl = [logits[i:i + 1, :] for i in range(N_EXPERT_GROUPS)]
    best = gl[0]
    sel = jnp.zeros((1, tm), I32)
    for i in range(1, N_EXPERT_GROUPS):
        upd = gl[i] > best
        sel = jnp.where(upd, i, sel)
        best = jnp.where(upd, gl[i], best)
    den = jnp.exp(gl[0] - best)
    for i in range(1, N_EXPERT_GROUPS):
        den = den + jnp.exp(gl[i] - best)
    g_w = 1.0 / den
    chosen = logits[8:8 + EXPERTS_PER_GROUP, :]
    for i in range(1, N_EXPERT_GROUPS):
        chosen = jnp.where(sel == i, logits[8 + i * EXPERTS_PER_GROUP:8 + (i + 1) * EXPERTS_PER_GROUP, :], chosen)
    idx = _iota((EXPERTS_PER_GROUP, tm), 0)
    m1 = jnp.max(chosen, axis=0, keepdims=True)
    i1 = jnp.min(jnp.where(chosen == m1, idx, EXPERTS_PER_GROUP), axis=0, keepdims=True)
    rest = jnp.where(idx == i1, -jnp.inf, chosen)
    m2 = jnp.max(rest, axis=0, keepdims=True)
    i2 = jnp.min(jnp.where(rest == m2, idx, EXPERTS_PER_GROUP), axis=0, keepdims=True)
    t = jnp.exp(m2 - m1)
    inv = 1.0 / (1.0 + t)
    w0 = inv * g_w
    w1 = (t * inv) * g_w
    e0 = sel * EXPERTS_PER_GROUP + i1
    e1 = sel * EXPERTS_PER_GROUP + i2

    eidx = _iota((N_EXPERTS, tm), 0)
    oh0 = eidx == e0
    oh1 = eidx == e1
    oh = jnp.where(oh0, 1.0, jnp.where(oh1, 1.0, 0.0))
    base = base_ref[...]
    before = _dot(oh.astype(BF16), su_ref[...]) + jnp.concatenate([base] * (tm // LANES), axis=1)
    rank0 = jnp.sum(jnp.where(oh0, before, 0.0), axis=0, keepdims=True)
    rank1 = jnp.sum(jnp.where(oh1, before, 0.0), axis=0, keepdims=True)
    new_base = base + jnp.sum(oh, axis=1, keepdims=True)
    base_ref[...] = new_base
    cnt_ref[...] = new_base

    r = _iota((8, tm), 0)
    rows = (e0.astype(F32), e1.astype(F32), w0, w1, rank0, rank1)
    slab = jnp.zeros((8, tm), F32)
    for i, v in enumerate(rows):
        slab = jnp.where(r == i, v, slab)
    slab_ref[...] = slab


def _router(x, gain, w_split, bias, layer, tm):
    n = x.shape[0]
    xn, slab, cnt = pl.pallas_call(
        _router_body, grid=(n // tm,),
        in_specs=[pl.BlockSpec((tm, D_MODEL), lambda i: (i, 0)),
                  pl.BlockSpec((1, D_MODEL), lambda i: (0, 0)),
                  pl.BlockSpec((None, ROUTER_W_ROWS, D_MODEL), lambda i: (layer, 0, 0)),
                  pl.BlockSpec((None, ROUTER_ROWS, 1), lambda i: (layer, 0, 0))],
        out_specs=[pl.BlockSpec((tm * ROW_TILES, LANES), lambda i: (i, 0)),
                   pl.BlockSpec((8, tm), lambda i: (0, i)),
                   pl.BlockSpec((N_EXPERTS, LANES), lambda i: (0, 0))],
        out_shape=[jax.ShapeDtypeStruct((n * ROW_TILES, LANES), F32), jax.ShapeDtypeStruct((8, n), F32),
                   jax.ShapeDtypeStruct((N_EXPERTS, LANES), F32)],
        scratch_shapes=[pltpu.VMEM((N_EXPERTS, LANES), F32), pltpu.VMEM((tm, tm), BF16)],
        compiler_params=_params(("arbitrary",)), name="router")(x, gain.reshape(1, D_MODEL), w_split, bias)
    return xn, slab, cnt


def _moe_plan(slabs, cnts, tm, blks):
    counts = [cnt[:, 0].astype(I32) for cnt in cnts]
    total = sum(counts)
    padded = ((total + tm - 1) // tm) * tm
    pend = jnp.cumsum(padded)
    pstart = pend - padded
    dests = []
    base = pstart
    for slab, count, blk in zip(slabs, counts, blks):
        n = slab.shape[1]
        e_ids = slab[0:2].astype(I32)
        ranks = slab[4:6].astype(I32)
        onehot = e_ids[..., None] == jnp.arange(N_EXPERTS, dtype=I32)
        dest = jnp.sum(jnp.where(onehot, base, 0), axis=-1) + ranks
        dests.append(dest.T.reshape(n // blk, 1, 2 * blk))
        base = base + count
    slots = 2 * sum(slab.shape[1] for slab in slabs) + N_EXPERTS * tm
    n_tiles = slots // tm
    n_used = (pend[-1] // tm).astype(I32)
    tile_start = jnp.arange(n_tiles, dtype=I32) * tm
    tile_e = jnp.sum((pend[None, :] <= tile_start[:, None]).astype(I32), axis=1)
    tile_e = jnp.minimum(tile_e, N_EXPERTS - 1)
    last_e = jnp.sum(jnp.where(jnp.arange(n_tiles) == n_used - 1, tile_e, 0))
    tile_e = jnp.where(jnp.arange(n_tiles) < n_used, tile_e, last_e).astype(I32)
    return dests, tile_e, n_used.reshape(1), slots, (pstart + total).astype(I32), pend.astype(I32)


def _dispatch_body(dest_ref, xn_ref, *rest, blk):
    xs_hbm, sem = rest[-2:]

    def issue(i, carry):
        src = xn_ref.at[pl.ds(pl.multiple_of(i * ROW_TILES, ROW_TILES), ROW_TILES)]
        pltpu.make_async_copy(src, xs_hbm.at[dest_ref[0, 0, 2 * i]], sem).start()
        pltpu.make_async_copy(src, xs_hbm.at[dest_ref[0, 0, 2 * i + 1]], sem).start(priority=1)
        return carry

    lax.fori_loop(0, blk, issue, 0, unroll=8)

    def drain(i, carry):
        pltpu.make_async_copy(xn_ref.at[pl.ds(0, ROW_TILES)], xs_hbm.at[0], sem).wait()
        return carry

    lax.fori_loop(0, 2 * blk, drain, 0, unroll=8)


def _dispatch(xn, dest, slots, blk, xs=None):
    n = xn.shape[0] // ROW_TILES
    in_specs = [pl.BlockSpec((1, 1, 2 * blk), lambda i: (i, 0, 0), memory_space=pltpu.SMEM),
                pl.BlockSpec((blk * ROW_TILES, LANES), lambda i: (i, 0))]
    args = [dest, xn]
    if xs is not None:
        in_specs.append(pl.BlockSpec(memory_space=pl.ANY))
        args.append(xs)
    return pl.pallas_call(
        functools.partial(_dispatch_body, blk=blk), grid=(n // blk,), in_specs=in_specs,
        out_specs=pl.BlockSpec(memory_space=pl.ANY),
        out_shape=jax.ShapeDtypeStruct((slots, ROW_TILES, LANES), F32),
        scratch_shapes=[pltpu.SemaphoreType.DMA(())],
        input_output_aliases={} if xs is None else {2: 0},
        compiler_params=_params(("arbitrary",)), name="dispatch")(*args)


def _pad_zero_body(lo_ref, hi_ref, xs_in_hbm, xs_hbm, z_ref, sem):
    del xs_in_hbm
    z_ref[...] = jnp.zeros_like(z_ref)
    sizes = [1 << s for s in range(MOE_TILE.bit_length() - 2, -1, -1)]

    def copies(e, act):
        pos = lo_ref[e]
        n = hi_ref[e] - pos
        for size in sizes:
            take = (n & size) != 0

            @pl.when(take)
            def _(pos=pos, size=size):
                act(pltpu.make_async_copy(z_ref.at[pl.ds(0, size)], xs_hbm.at[pl.ds(pos, size)], sem))

            pos = pos + jnp.where(take, size, 0)

    def issue(e, carry):
        copies(e, lambda cp: cp.start())
        return carry

    def drain(e, carry):
        copies(e, lambda cp: cp.wait())
        return carry

    lax.fori_loop(0, N_EXPERTS, issue, 0)
    lax.fori_loop(0, N_EXPERTS, drain, 0)


def _pad_zero(xs, lo, hi):
    grid_spec = pltpu.PrefetchScalarGridSpec(
        num_scalar_prefetch=2, grid=(1,),
        in_specs=[pl.BlockSpec(memory_space=pl.ANY)], out_specs=pl.BlockSpec(memory_space=pl.ANY),
        scratch_shapes=[pltpu.VMEM((MOE_TILE // 2, ROW_TILES, LANES), F32), pltpu.SemaphoreType.DMA(())])
    return pl.pallas_call(
        _pad_zero_body, grid_spec=grid_spec, out_shape=jax.ShapeDtypeStruct(xs.shape, xs.dtype),
        input_output_aliases={2: 0},
        compiler_params=_params(("arbitrary",)), name="pad_zero")(lo, hi, xs)


def _moe_body(te_ref, nu_ref, xs_ref, wg_ref, wu_ref, wd_ref, o_ref, wgb_ref, wub_ref, wdb_ref, *, tm):
    t = pl.program_id(0)
    e = te_ref[t]
    prev = te_ref[jnp.maximum(t - 1, 0)]

    @pl.when((t == 0) | (e != prev))
    def _():
        wgb_ref[...] = wg_ref[0].astype(BF16)
        wub_ref[...] = wu_ref[0].astype(BF16)
        wdb_ref[...] = wd_ref[0].astype(BF16)

    @pl.when(t < nu_ref[0])
    def _():
        x = _tiles_to_rows(xs_ref, tm).astype(BF16)
        g = _dot(x, wgb_ref[...])
        u = _dot(x, wub_ref[...])
        hid = _silu(g) * u
        _rows_to_tiles(o_ref, _dot(hid.astype(BF16), wdb_ref[...]))


def _moe_experts(xs, tile_e, n_used, w_gate, w_up, w_down, layer, tm):
    slots = xs.shape[0] // ROW_TILES
    n_tiles = slots // tm
    row = lambda t, te, nu: (jnp.minimum(t, nu[0] - 1), 0)
    wsel = lambda t, te, nu: (layer, te[t], 0, 0)
    grid_spec = pltpu.PrefetchScalarGridSpec(
        num_scalar_prefetch=2, grid=(n_tiles,),
        in_specs=[pl.BlockSpec((tm * ROW_TILES, LANES), row),
                  pl.BlockSpec((None, 1, D_MODEL, D_EXPERT), wsel),
                  pl.BlockSpec((None, 1, D_MODEL, D_EXPERT), wsel),
                  pl.BlockSpec((None, 1, D_EXPERT, D_MODEL), wsel)],
        out_specs=pl.BlockSpec((tm * ROW_TILES, LANES), row),
        scratch_shapes=[pltpu.VMEM((D_MODEL, D_EXPERT), BF16), pltpu.VMEM((D_MODEL, D_EXPERT), BF16),
                        pltpu.VMEM((D_EXPERT, D_MODEL), BF16)])
    return pl.pallas_call(
        functools.partial(_moe_body, tm=tm), grid_spec=grid_spec,
        out_shape=jax.ShapeDtypeStruct((slots * ROW_TILES, LANES), F32),
        compiler_params=_params(("arbitrary",)), name="moe_experts")(tile_e, n_used, xs, w_gate, w_up, w_down)


def _ple_body(*refs, final, tm):
    dest_ref, x_ref, gw_ref, p_ref, g_ref, wg_ref, wp_ref = refs[:7]
    if final:
        fg_ref, ys_hbm, o_ref, f_ref, y0_ref, y1_ref, sem = refs[7:]
    else:
        ys_hbm, o_ref, y0_ref, y1_ref, sem = refs[7:]

    def issue(i, carry):
        row = pl.multiple_of(i * ROW_TILES, ROW_TILES)
        pltpu.make_async_copy(ys_hbm.at[dest_ref[0, 0, 2 * i]], y0_ref.at[pl.ds(row, ROW_TILES)], sem).start()
        pltpu.make_async_copy(ys_hbm.at[dest_ref[0, 0, 2 * i + 1]], y1_ref.at[pl.ds(row, ROW_TILES)], sem).start(priority=1)
        return carry

    lax.fori_loop(0, tm, issue, 0, unroll=8)
    pe = _dot(p_ref[...].astype(BF16), wp_ref[...])

    def drain(i, carry):
        pltpu.make_async_copy(ys_hbm.at[0], y0_ref.at[pl.ds(0, ROW_TILES)], sem).wait()
        return carry

    lax.fori_loop(0, 2 * tm, drain, 0, unroll=8)
    gw = gw_ref[...]
    x2 = x_ref[...] + gw[:, 2:3] * _tiles_to_rows(y0_ref, tm) + gw[:, 3:4] * _tiles_to_rows(y1_ref, tm)
    xn = _rms(x2, g_ref[...])
    gate = _sigmoid(_dot(xn.astype(BF16), wg_ref[...]))
    out = x2 + pe * gate
    o_ref[...] = out
    if final:
        f_ref[...] = _rms(out, fg_ref[...])


def _ple(x, ys, dest, gw, p, gain, w_gate, w_proj, layer, final_gain, tm):
    n = x.shape[0]
    dp = p.shape[-1]
    final = final_gain is not None
    rowspec = pl.BlockSpec((tm, D_MODEL), lambda i: (i, 0))
    vec = pl.BlockSpec((1, D_MODEL), lambda i: (0, 0))
    in_specs = [pl.BlockSpec((1, 1, 2 * tm), lambda i: (i, 0, 0), memory_space=pltpu.SMEM),
                rowspec, pl.BlockSpec((tm, 8), lambda i: (i, 0)),
                pl.BlockSpec((None, tm, dp), lambda i: (layer, i, 0)), vec,
                pl.BlockSpec((None, D_MODEL, D_MODEL), lambda i: (layer, 0, 0)),
                pl.BlockSpec((None, dp, D_MODEL), lambda i: (layer, 0, 0))]
    args = [dest, x, gw, p, gain.reshape(1, D_MODEL), w_gate, w_proj]
    out_shape = [jax.ShapeDtypeStruct((n, D_MODEL), F32)]
    out_specs = [rowspec]
    if final:
        in_specs.append(vec)
        args.append(final_gain.reshape(1, D_MODEL))
        out_shape.append(jax.ShapeDtypeStruct((n, D_MODEL), F32))
        out_specs.append(rowspec)
    in_specs.append(pl.BlockSpec(memory_space=pl.ANY))
    args.append(ys.reshape(-1, ROW_TILES, LANES))
    outs = pl.pallas_call(
        functools.partial(_ple_body, final=final, tm=tm), grid=(n // tm,), in_specs=in_specs,
        out_specs=out_specs, out_shape=out_shape,
        scratch_shapes=[pltpu.VMEM((tm * ROW_TILES, LANES), F32), pltpu.VMEM((tm * ROW_TILES, LANES), F32),
                        pltpu.SemaphoreType.DMA(())],
        compiler_params=_params(("arbitrary",)), name="ple")(*args)
    return (outs[0], outs[1]) if final else (outs[0], None)


ATTN_GROUP = 4
LOG2E = 1.4426950408889634


def _attn_group(qs, kcats, vcats, biases):
    R = CHUNK
    s = jnp.concatenate([_dot_nt(q, kcat) * (HEAD_DIM ** -0.5 * LOG2E) + bias
                         for q, kcat, bias in zip(qs, kcats, biases)], axis=0)
    m = jnp.max(s, axis=-1, keepdims=True)
    p = jnp.exp2(s - m)
    l = jnp.sum(p, axis=-1, keepdims=True)
    pb = p.astype(BF16)
    inv = 1.0 / l
    lse = m * (1.0 / LOG2E) + jnp.log(l)
    outs, lses = [], []
    for i, vcat in enumerate(vcats):
        sl = slice(i * R, (i + 1) * R)
        outs.append(_dot(pb[sl], vcat) * inv[sl])
        lses.append(jnp.broadcast_to(lse[sl], (R, LANES)))
    return outs, lses


def _attn_body(q0_ref, q1_ref, q2_ref, k_ref, v_ref, o_ref, kp_ref, vp_ref, og_ref, lg_ref, bias_ref):
    R = CHUNK
    row = _iota((R, 2 * R), 0)
    col = _iota((R, 2 * R), 1)
    first_key = jnp.where(pl.program_id(2) > 0, 0, R)
    bias_ref[0] = jnp.where((col >= row) & (col <= row + R), 0.0, -jnp.inf)
    bias_ref[1] = jnp.where((col >= jnp.maximum(row, first_key)) & (col <= row + R), 0.0, -jnp.inf)

    @pl.when(pl.program_id(2) == 0)
    def _():
        kp_ref[...] = jnp.zeros_like(kp_ref)
        vp_ref[...] = jnp.zeros_like(vp_ref)

    def rows(ref, start, dil):
        idx = pl.ds(start, R) if dil == 1 else pl.ds(start, R, stride=dil)
        return ref[idx, :]

    def put(ref, start, dil, val):
        idx = pl.ds(start, R) if dil == 1 else pl.ds(start, R, stride=dil)
        ref[idx, :] = val

    def group(g, q_ref, dil, blocks):
        nblk = ATTN_SPAN // (R * dil)
        qs, kcats, vcats, fks, starts = [], [], [], [], []
        for r, cb in blocks:
            start = r + dil * R * cb
            if cb == 0:
                prev_k, prev_v, prev_start, fk = kp_ref, vp_ref, r + dil * R * (nblk - 1), bias_ref[1]
            else:
                prev_k, prev_v, prev_start, fk = k_ref, v_ref, start - dil * R, bias_ref[0]
            kcats.append(jnp.concatenate([rows(prev_k, prev_start, dil).astype(BF16),
                                          rows(k_ref, start, dil).astype(BF16)], axis=0))
            vcats.append(jnp.concatenate([rows(prev_v, prev_start, dil).astype(BF16),
                                          rows(v_ref, start, dil).astype(BF16)], axis=0))
            qs.append(rows(q_ref, start, dil).astype(BF16))
            fks.append(fk)
            starts.append(start)
        outs, lses = _attn_group(qs, kcats, vcats, fks)
        for start, o, lse in zip(starts, outs, lses):
            put(og_ref.at[g], start, dil, o)
            put(lg_ref.at[g], start, dil, lse)

    for g, (q_ref, (_, dil)) in enumerate(zip((q0_ref, q1_ref, q2_ref), PATTERNS)):
        nblk = ATTN_SPAN // (R * dil)
        if nblk >= ATTN_GROUP:
            def lane(r, carry, q_ref=q_ref, dil=dil, nblk=nblk, g=g):
                for c0 in range(0, nblk, ATTN_GROUP):
                    group(g, q_ref, dil, [(r, cb) for cb in range(c0, c0 + ATTN_GROUP)])
                return carry
            n_iter = dil
        else:
            def lane(i, carry, q_ref=q_ref, dil=dil, nblk=nblk, g=g):
                for cb in range(nblk):
                    group(g, q_ref, dil, [(i * ATTN_GROUP + j, cb) for j in range(ATTN_GROUP)])
                return carry
            n_iter = dil // ATTN_GROUP
        if n_iter == 1:
            lane(0, 0)
        else:
            lax.fori_loop(0, n_iter, lane, 0)

    step = 256
    for i in range(ATTN_SPAN // step):
        sl = pl.ds(i * step, step)
        l0 = lg_ref[0, sl, :]
        l1 = lg_ref[1, sl, :]
        l2 = lg_ref[2, sl, :]
        m = jnp.maximum(jnp.maximum(l0, l1), l2)
        e0 = jnp.exp(l0 - m)
        e1 = jnp.exp(l1 - m)
        e2 = jnp.exp(l2 - m)
        num = e0 * og_ref[0, sl, :] + e1 * og_ref[1, sl, :] + e2 * og_ref[2, sl, :]
        o_ref[sl, :] = num * (1.0 / (e0 + e1 + e2))
    kp_ref[...] = k_ref[...]
    vp_ref[...] = v_ref[...]


def _attn_prompt(q, k, v, b, s):
    nspan = s // ATTN_SPAN
    blk = (ATTN_SPAN, HEAD_DIM)
    hblk = (None,) + blk
    qspec = lambda g: pl.BlockSpec(hblk, lambda bi, h, c: (g * N_KV_HEADS + h, bi * nspan + c, 0))
    kvspec = pl.BlockSpec(hblk, lambda bi, h, c: (h, bi * nspan + c, 0))
    return pl.pallas_call(
        _attn_body, grid=(b, N_KV_HEADS, nspan),
        in_specs=[qspec(0), qspec(1), qspec(2), kvspec, kvspec],
        out_specs=kvspec, out_shape=jax.ShapeDtypeStruct((N_KV_HEADS, b * s, HEAD_DIM), F32),
        scratch_shapes=[pltpu.VMEM(blk, F32), pltpu.VMEM(blk, F32),
                        pltpu.VMEM((3,) + blk, F32), pltpu.VMEM((3,) + blk, F32),
                        pltpu.VMEM((2, CHUNK, 2 * CHUNK), F32)],
        compiler_params=_params(("parallel", "parallel", "arbitrary")), name="attn_prompt")(q, q, q, k, v)


SAMPLE_ROWS = 16
DIL_MAX = PATTERNS[-1][1]
DENSE_TAIL = PATTERNS[1][0]


def _attn_sample_body(q_ref, kf_ref, vf_ref, kt_ref, vt_ref, kn_ref, vn_ref, bias_ref, o_ref, *, t_new):
    nq = N_KV_HEADS * SAMPLE_ROWS
    n_far = kf_ref.shape[1] * kf_ref.shape[2]
    n_tail = kt_ref.shape[1]
    q = q_ref[0].astype(BF16)
    kf = kf_ref[0].reshape(n_far, HEAD_DIM).astype(BF16)
    vf = vf_ref[0].reshape(n_far, HEAD_DIM).astype(BF16)
    s = jnp.concatenate([_dot_nt(q, kf), _dot_nt(q, kt_ref[0].astype(BF16)), _dot_nt(q, kn_ref[0].astype(BF16))],
                        axis=1) * (HEAD_DIM ** -0.5) + bias_ref[...]
    m = jnp.max(s, axis=-1, keepdims=True)
    p = jnp.exp(s - m)
    l = jnp.sum(p, axis=-1, keepdims=True)
    pb = p.astype(BF16)
    o = (_dot(pb[:, :n_far], vf) + _dot(pb[:, n_far:n_far + n_tail], vt_ref[0].astype(BF16))
         + _dot(pb[:, n_far + n_tail:], vn_ref[0].astype(BF16))) * (1.0 / l)
    lse = jnp.broadcast_to(m + jnp.log(l), (nq, LANES))
    lse1 = pltpu.roll(lse, nq - t_new, 0)
    lse2 = pltpu.roll(lse, nq - 2 * t_new, 0)
    o1 = pltpu.roll(o, nq - t_new, 0)
    o2 = pltpu.roll(o, nq - 2 * t_new, 0)
    mm = jnp.maximum(jnp.maximum(lse, lse1), lse2)
    e0 = jnp.exp(lse - mm)
    e1 = jnp.exp(lse1 - mm)
    e2 = jnp.exp(lse2 - mm)
    comb = (e0 * o + e1 * o1 + e2 * o2) * (1.0 / (e0 + e1 + e2))
    for h in range(N_KV_HEADS):
        o_ref[0, :, h * HEAD_DIM:(h + 1) * HEAD_DIM] = comb[h * SAMPLE_ROWS:h * SAMPLE_ROWS + t_new]


def _sample_bias(wb, t_new, far):
    rho = jnp.arange(N_KV_HEADS * SAMPLE_ROWS, dtype=I32)
    h = rho // SAMPLE_ROWS
    gt = rho % SAMPLE_ROWS
    g = jnp.where(gt < len(PATTERNS) * t_new, gt // t_new, 0)
    t = gt % t_new
    win = jnp.asarray([w for w, _ in PATTERNS], I32)[g]
    dil = jnp.asarray([d for _, d in PATTERNS], I32)[g]
    far_rows = t_new * N_KV_HEADS
    c = jnp.arange(far // DIL_MAX * far_rows, dtype=I32)
    pos_far = DIL_MAX * (c // far_rows) + (c % far_rows) // N_KV_HEADS
    c = jnp.arange((wb - far) * N_KV_HEADS, dtype=I32)
    pos_tail = far + c // N_KV_HEADS
    c = jnp.arange(LANES, dtype=I32)
    pos_new = jnp.where(c < far_rows, wb + c // N_KV_HEADS, -1)
    pos = jnp.concatenate([pos_far, pos_tail, pos_new])
    head = jnp.concatenate([jnp.arange(n, dtype=I32) % N_KV_HEADS for n in (pos_far.shape[0], pos_tail.shape[0], LANES)])
    dist = wb + t[:, None] - pos[None, :]
    valid = ((head[None, :] == h[:, None]) & (pos[None, :] >= 0) & (dist >= 0) & (dist <= win[:, None])
             & (dist % dil[:, None] == 0))
    return jnp.where(valid, 0.0, -jnp.inf).astype(F32)


def _attn_sample(q, cache_k, cache_v, k_new, v_new, b, t_new):
    wb = cache_k.shape[1]
    tail = min(DENSE_TAIL, wb)
    far = wb - tail
    far_rows = t_new * N_KV_HEADS
    nq = N_KV_HEADS * SAMPLE_ROWS
    qh = q.reshape(b, t_new, 3, N_KV_HEADS, HEAD_DIM).transpose(0, 3, 2, 1, 4).reshape(b, N_KV_HEADS, 3 * t_new, HEAD_DIM)
    qh = jnp.pad(qh, ((0, 0), (0, 0), (0, SAMPLE_ROWS - 3 * t_new), (0, 0))).reshape(b, nq, HEAD_DIM)
    new = lambda x: jnp.pad(x.reshape(b, far_rows, HEAD_DIM), ((0, 0), (0, LANES - far_rows), (0, 0)))
    grouped = lambda x: x.reshape(b, wb // DIL_MAX, DIL_MAX * N_KV_HEADS, HEAD_DIM)
    flat = lambda x: x.reshape(b, wb * N_KV_HEADS, HEAD_DIM)
    bias = _sample_bias(wb, t_new, far)
    far_spec = pl.BlockSpec((1, far // DIL_MAX, far_rows, HEAD_DIM), lambda bi: (bi, 0, 0, 0))
    tail_spec = pl.BlockSpec((1, tail * N_KV_HEADS, HEAD_DIM), lambda bi: (bi, wb // tail - 1, 0))
    sq_spec = pl.BlockSpec((1, LANES, HEAD_DIM), lambda bi: (bi, 0, 0))
    o = pl.pallas_call(
        functools.partial(_attn_sample_body, t_new=t_new), grid=(b,),
        in_specs=[sq_spec, far_spec, far_spec, tail_spec, tail_spec, sq_spec, sq_spec,
                  pl.BlockSpec(bias.shape, lambda bi: (0, 0))],
        out_specs=pl.BlockSpec((1, t_new, ATTN_DIM), lambda bi: (bi, 0, 0)),
        out_shape=jax.ShapeDtypeStruct((b, t_new, ATTN_DIM), F32),
        compiler_params=_params(("parallel",)), name="attn_sample")(
            qh, grouped(cache_k), grouped(cache_v), flat(cache_k), flat(cache_v), new(k_new), new(v_new), bias)
    return o.reshape(b * t_new, ATTN_DIM)


def _rope_tables(pos0, l, reps):
    inv_freq = ROPE_THETA ** (-jnp.arange(0, HEAD_DIM, 2, dtype=F32) / HEAD_DIM)
    pos = pos0 + jnp.arange(l, dtype=jnp.int32)
    ang = pos.astype(F32)[:, None] * inv_freq[None, :]
    cos = jnp.cos(ang)
    sin = jnp.sin(ang)
    cosf = jnp.concatenate([cos, cos], axis=-1)
    sinf = jnp.concatenate([-sin, sin], axis=-1)
    if reps > 1:
        cosf = jnp.tile(cosf, (reps, 1))
        sinf = jnp.tile(sinf, (reps, 1))
    return cosf, sinf


def _prep_weights(prm):
    w = {}
    w['in_proj'], w['in_proj_lo'] = _split2(prm['ssm_in_proj'])
    w['in_dt'], w['in_dt_lo'] = _split2(
        jnp.pad(prm['ssm_in_proj'][:, :, ZXBC_DIM:], ((0, 0), (0, 0), (0, LANES - SSM_HEADS))))
    w['out_proj'], w['out_proj_lo'] = _split2(prm['ssm_out_proj'])
    w['w_kv'] = prm['w_kv'].astype(BF16)
    w['w_q'] = prm['attn_w_q'].astype(BF16)
    w['w_o'] = prm['attn_w_o'].astype(BF16)
    w['ple_gate'] = prm['ple_w_gate'].astype(BF16)
    w['ple_proj'] = prm['ple_w_proj'].astype(BF16)
    wr = jnp.concatenate([jnp.swapaxes(prm['moe_w_rg'], 1, 2),
                          jnp.zeros((DEPTH, 8 - N_EXPERT_GROUPS, D_MODEL), F32),
                          jnp.swapaxes(prm['moe_w_re'], 1, 2)], axis=1)
    w['router'] = jnp.concatenate(_split3(wr) + (jnp.zeros((DEPTH, ROUTER_W_ROWS - 3 * ROUTER_ROWS, D_MODEL), BF16),),
                                  axis=1)
    w['router_b'] = jnp.concatenate([prm['moe_b_rg'], jnp.zeros((DEPTH, 8 - N_EXPERT_GROUPS), F32),
                                     prm['moe_b_re']], axis=1).reshape(DEPTH, ROUTER_ROWS, 1)
    return w


def _trunk_state(x, p, pos0, conv_state, ssm_state, cache_k, cache_v):
    b, l, _ = x.shape
    n = b * l
    tm = min(512, n)
    return dict(b=b, l=l, n=n, tm=tm, tp=min(PROJ_ROWS, n), sample=cache_k is not None, xf=x.reshape(n, D_MODEL),
                p=p.reshape(DEPTH, n, -1), rope=_rope_tables(pos0, l, 1 if l >= tm else tm // l),
                conv_state=conv_state, ssm_state=ssm_state, cache_k=cache_k, cache_v=cache_v,
                new_conv=[], new_ssm=[], k32=None, v32=None, kh=None, vh=None, y_final=None)


def _mixer(t, i, prm, w):
    b, l, n, tm, tp, xf = t['b'], t['l'], t['n'], t['tm'], t['tp'], t['xf']
    gain = prm['norm_mix'][i]
    if i < N_A_LAYERS:
        precise = i < PRECISE_MIXERS
        lo = (lambda name: w[name + '_lo']) if precise else (lambda name: None)
        zx = _proj(xf, w['in_proj'], w_lo=lo('in_proj'), layer=i, gain=gain, nout=ZXBC_DIM, tm=tp, tn=1024)
        dtr = _proj(xf, w['in_dt'], w_lo=lo('in_dt'), layer=i, gain=gain, tm=tp, tn=LANES)
        y, cbuf, hs = _ssd(zx.reshape(b, l, ZXBC_DIM), dtr.reshape(b, l, LANES), prm['ssm_conv_w'][i],
                           prm['ssm_conv_b'][i], prm['ssm_dt_bias'][i], prm['ssm_a_log'][i], prm['ssm_d'][i],
                           prm['ssm_norm_y'][i], t['conv_state'][i], t['ssm_state'], i, precise)
        t['new_conv'].append(cbuf)
        t['new_ssm'].append(hs)
        t['xf'] = _proj(y.reshape(n, D_INNER), w['out_proj'], w_lo=lo('out_proj'), layer=i, residual=xf,
                        tm=tm, tn=1024)
    else:
        j = i - N_A_LAYERS
        q = _proj(xf, w['w_q'], layer=j, gain=gain, rope=t['rope'], out_heads=(not t['sample'],), tm=tp, tn=1024)
        if t['sample']:
            att = _attn_sample(q, t['cache_k'], t['cache_v'], t['k32'], t['v32'], b, l)
        else:
            att = _attn_prompt(q, t['kh'], t['vh'], b, l)
        t['xf'] = _proj(att, w['w_o'], layer=j, residual=xf, tm=tp, tn=1024)


def _shared_kv(t, prm, w):
    tm, xf = t['tp'], t['xf']
    kv_out = dict(out_dtypes=(F32,), out_heads=(False,)) if t['sample'] else \
        dict(out_dtypes=(F32, F32), out_heads=(False, True))
    k_outs = _proj(xf, w['w_kv'], gain=prm['norm_kv'], rope=t['rope'], nout=ATTN_DIM, tm=tm, tn=1024, **kv_out)
    v_outs = _proj(xf, w['w_kv'], col0=ATTN_DIM, gain=prm['norm_kv'], nout=ATTN_DIM, tm=tm, tn=1024, **kv_out)
    (t['k32'], t['kh']), (t['v32'], t['vh']) = ((k_outs, None), (v_outs, None)) if t['sample'] else (k_outs, v_outs)


def _moe_ple(trunks, i, prm, w):
    routed = [_router(t['xf'], prm['norm_ffn'][i], w['router'], w['router_b'], i, t['tp']) for t in trunks]
    dests, tile_e, n_used, slots, pad_lo, pad_hi = _moe_plan([r[1] for r in routed], [r[2] for r in routed],
                                                             MOE_TILE, [t['tp'] for t in trunks])
    xs = None
    for t, (xn, _, _), dest in zip(trunks, routed, dests):
        xs = _dispatch(xn, dest, slots, t['tp'], xs)
    xs = _pad_zero(xs, pad_lo, pad_hi)
    ys = _moe_experts(xs.reshape(slots * ROW_TILES, LANES), tile_e, n_used, prm['moe_w_gate'], prm['moe_w_up'],
                      prm['moe_w_down'], i, MOE_TILE)
    for t, (_, slab, _), dest in zip(trunks, routed, dests):
        t['xf'], t['y_final'] = _ple(t['xf'], ys, dest, slab.T, t['p'], prm['ple_norm'][i], w['ple_gate'],
                                     w['ple_proj'], i, prm['norm_final'] if i == DEPTH - 1 else None, t['tp'])


def _run(trunks, prm, w):
    for i in range(DEPTH):
        for t in trunks:
            _mixer(t, i, prm, w)
        _moe_ple(trunks, i, prm, w)
        if i == N_A_LAYERS - 1:
            for t in trunks:
                _shared_kv(t, prm, w)
    return [(t['y_final'].reshape(t['b'], t['l'], D_MODEL), jnp.stack(t['new_conv']), jnp.stack(t['new_ssm']),
             t['k32'].reshape(t['b'], t['l'], ATTN_DIM), t['v32'].reshape(t['b'], t['l'], ATTN_DIM))
            for t in trunks]


def kernel(x_prompt, x_sample, state_ssm, state_conv, cache_k, cache_v, p_prompt, p_sample, norm_mix, ssm_in_proj, ssm_conv_w, ssm_conv_b, ssm_dt_bias, ssm_a_log, ssm_d, ssm_norm_y, ssm_out_proj, norm_kv, w_kv, attn_w_q, attn_w_o, norm_ffn, moe_w_rg, moe_b_rg, moe_w_re, moe_b_re, moe_w_gate, moe_w_up, moe_w_down, ple_w_proj, ple_w_gate, ple_norm, norm_final):
    prm = dict(norm_mix=norm_mix, ssm_in_proj=ssm_in_proj, ssm_conv_w=ssm_conv_w, ssm_conv_b=ssm_conv_b,
               ssm_dt_bias=ssm_dt_bias, ssm_a_log=ssm_a_log, ssm_d=ssm_d, ssm_norm_y=ssm_norm_y,
               ssm_out_proj=ssm_out_proj, norm_kv=norm_kv, w_kv=w_kv, attn_w_q=attn_w_q, attn_w_o=attn_w_o,
               norm_ffn=norm_ffn, moe_w_rg=moe_w_rg, moe_b_rg=moe_b_rg, moe_w_re=moe_w_re, moe_b_re=moe_b_re,
               moe_w_gate=moe_w_gate, moe_w_up=moe_w_up, moe_w_down=moe_w_down, ple_w_proj=ple_w_proj,
               ple_w_gate=ple_w_gate, ple_norm=ple_norm, norm_final=norm_final)
    w = _prep_weights(prm)
    b, s, _ = x_prompt.shape
    db, dl, _ = x_sample.shape
    past_len = 8192
    conv0 = jnp.zeros((N_A_LAYERS, b, 3, CONV_DIM), x_prompt.dtype)
    ssm0 = jnp.zeros((N_A_LAYERS, b, SSM_HEADS, SSM_HEADDIM, D_STATE), state_ssm.dtype)
    trunks = [_trunk_state(x_prompt, p_prompt, 0, conv0, ssm0, None, None),
              _trunk_state(x_sample, p_sample, past_len, state_conv, state_ssm, cache_k, cache_v)]
    (y_p, conv_p, ssm_p, k_p, v_p), (y_s, conv_s, ssm_s, k_s, v_s) = _run(trunks, prm, w)
    keep = min(2048, s)
    heads = lambda t: t.reshape(t.shape[0], t.shape[1], N_KV_HEADS, HEAD_DIM)
    k_p = heads(k_p[:, s - keep:])
    v_p = heads(v_p[:, s - keep:])
    return (y_p, y_s, ssm_p, conv_p, k_p, v_p, ssm_s, conv_s, heads(k_s), heads(v_s))
```

```python
import functools

import jax
import jax.numpy as jnp
from jax import lax
from jax.experimental import pallas as pl
from jax.experimental.pallas import tpu as pltpu

F32 = jnp.float32
BF16 = jnp.bfloat16
I32 = jnp.int32

EPS = 1e-6
D_MODEL = 1024
D_INNER = 2048
SSM_HEADS = 32
SSM_HEADDIM = 64
SSM_GROUPS = 4
D_STATE = 128
CONV_DIM = 3072
ZXBC_DIM = D_INNER + CONV_DIM
CHUNK = 128
HEAD_DIM = 128
N_KV_HEADS = 8
ATTN_DIM = 1024
PATTERNS = ((128, 1), (512, 4), (2048, 16))
ATTN_SPAN = 2048
ROPE_THETA = 10000.0
N_EXPERTS = 32
N_EXPERT_GROUPS = 4
EXPERTS_PER_GROUP = 8
D_EXPERT = 512
N_A_LAYERS = 2
DEPTH = 4
PRECISE_MIXERS = 1
MOE_TILE = 256
PROJ_ROWS = 1024
ROUTER_ROWS = 8 + N_EXPERTS
ROUTER_W_ROWS = 128
LANES = 128
SUBLANES = 8
ROW_TILES = D_MODEL // LANES
VMEM_LIMIT = 56 * 1024 * 1024


def _params(sem):
    return pltpu.CompilerParams(dimension_semantics=sem, vmem_limit_bytes=VMEM_LIMIT)


def _iota(shape, dim):
    return lax.broadcasted_iota(I32, shape, dim)


def _rms(x, gain):
    ms = jnp.mean(x * x, axis=-1, keepdims=True)
    return x * lax.rsqrt(ms + EPS) * gain


def _sigmoid(x):
    return 0.5 * jnp.tanh(0.5 * x) + 0.5


def _silu(x):
    return x * _sigmoid(x)


def _dot(a, b):
    return jnp.dot(a, b, preferred_element_type=F32)


def _dot_nt(a, b):
    return lax.dot_general(a, b, (((1,), (1,)), ((), ())), preferred_element_type=F32)


def _split3(x):
    hi = x.astype(BF16)
    r = x - hi.astype(F32)
    mid = r.astype(BF16)
    lo = (r - mid.astype(F32)).astype(BF16)
    return hi, mid, lo


def _split2(x):
    hi = x.astype(BF16)
    return hi, (x - hi.astype(F32)).astype(BF16)


def _mm(a, b, dot):
    y = dot(a[0], b[0])
    if len(a) > 1:
        y = y + dot(a[1], b[0]) + dot(a[0], b[1])
    return y


def _rows_to_tiles(ref, x):
    m = x.shape[0]
    for s in range(ROW_TILES):
        ref[pl.ds(s, m, stride=ROW_TILES), :] = x[:, s * LANES:(s + 1) * LANES]


def _tiles_to_rows(ref, m):
    return jnp.concatenate([ref[pl.ds(s, m, stride=ROW_TILES), :] for s in range(ROW_TILES)], axis=1)


def _proj_body(*refs, has_norm, has_rope, has_res, x_heads, out_heads, precise):
    it = iter(refs)
    x_ref = next(it)
    w_ref = next(it)
    wlo_ref = next(it) if precise else None
    g_ref = next(it) if has_norm else None
    cos_ref = next(it) if has_rope else None
    sin_ref = next(it) if has_rope else None
    r_ref = next(it) if has_res else None
    o_refs = [next(it) for _ in out_heads]
    xb_ref = next(it)
    xlo_ref = next(it) if precise else None

    @pl.when(pl.program_id(1) == 0)
    def _():
        if x_heads:
            x = jnp.concatenate([x_ref[h] for h in range(x_ref.shape[0])], axis=1).astype(F32)
        else:
            x = x_ref[...].astype(F32)
        if has_norm:
            x = _rms(x, g_ref[...])
        if precise:
            xb_ref[...], xlo_ref[...] = _split2(x)
        else:
            xb_ref[...] = x.astype(BF16)

    if precise:
        y = _mm((xb_ref[...], xlo_ref[...]), (w_ref[...], wlo_ref[...]), _dot)
    else:
        y = _dot(xb_ref[...], w_ref[...])
    if has_rope:
        cos = cos_ref[...]
        sin = sin_ref[...]
        parts = []
        for h in range(y.shape[1] // HEAD_DIM):
            t = y[:, h * HEAD_DIM:(h + 1) * HEAD_DIM]
            parts.append(t * cos + pltpu.roll(t, HEAD_DIM // 2, 1) * sin)
        y = parts[0] if len(parts) == 1 else jnp.concatenate(parts, axis=1)
    if has_res:
        y = y + r_ref[...]
    for o, heads in zip(o_refs, out_heads):
        if heads:
            for h in range(o.shape[0]):
                o[h] = y[:, h * HEAD_DIM:(h + 1) * HEAD_DIM].astype(o.dtype)
        else:
            o[...] = y.astype(o.dtype)


def _proj(x, w, *, w_lo=None, layer=None, col0=0, gain=None, rope=None, residual=None, out_dtypes=(F32,),
          out_heads=None, nout=None, tm, tn):
    x_heads = x.ndim == 3
    n = x.shape[-2]
    k = x.shape[0] * x.shape[2] if x_heads else x.shape[1]
    nout = w.shape[-1] if nout is None else nout
    tn = min(tn, nout)
    out_heads = (False,) * len(out_dtypes) if out_heads is None else out_heads
    grid = (n // tm, nout // tn)
    jb = col0 // tn
    precise = w_lo is not None
    wspec = (pl.BlockSpec((k, tn), lambda i, j: (0, j + jb)) if layer is None else
             pl.BlockSpec((None, k, tn), lambda i, j: (layer, 0, j + jb)))
    in_specs = [pl.BlockSpec((k // HEAD_DIM, tm, HEAD_DIM), lambda i, j: (0, i, 0)) if x_heads else
                pl.BlockSpec((tm, k), lambda i, j: (i, 0)), wspec]
    args = [x, w]
    if precise:
        in_specs.append(wspec)
        args.append(w_lo)
    if gain is not None:
        in_specs.append(pl.BlockSpec((1, k), lambda i, j: (0, 0)))
        args.append(gain.reshape(1, k))
    if rope is not None:
        cos, sin = rope
        nblk = cos.shape[0] // tm
        for t in (cos, sin):
            in_specs.append(pl.BlockSpec((tm, HEAD_DIM), lambda i, j: (i % nblk, 0)))
            args.append(t)
    if residual is not None:
        in_specs.append(pl.BlockSpec((tm, tn), lambda i, j: (i, j)))
        args.append(residual)
    out_shape = [jax.ShapeDtypeStruct((nout // HEAD_DIM, n, HEAD_DIM) if hm else (n, nout), dt)
                 for dt, hm in zip(out_dtypes, out_heads)]
    out_specs = [pl.BlockSpec((tn // HEAD_DIM, tm, HEAD_DIM), lambda i, j: (j, i, 0)) if hm else
                 pl.BlockSpec((tm, tn), lambda i, j: (i, j)) for hm in out_heads]
    body = functools.partial(_proj_body, has_norm=gain is not None, has_rope=rope is not None,
                             has_res=residual is not None, x_heads=x_heads, out_heads=tuple(out_heads),
                             precise=precise)
    outs = pl.pallas_call(
        body, grid=grid, in_specs=in_specs, out_specs=out_specs, out_shape=out_shape,
        scratch_shapes=[pltpu.VMEM((tm, k), BF16)] * (2 if precise else 1),
        compiler_params=_params(("parallel", "arbitrary")), name="proj")(*args)
    return outs[0] if len(outs) == 1 else outs


def _ssd_body(zx_ref, dt_ref, cw_ref, cb_ref, dtb_ref, alog_ref, d_ref, ny_ref, conv0_ref, ssm0_ref,
              y_ref, convo_ref, ssmo_ref, st_ref, tail_ref, xbc_ref, yacc_ref, *pad_refs, valid, nchunks, precise):
    L = CHUNK
    c = pl.program_id(1)
    npair = SSM_HEADS // 2
    parts = _split2 if precise else (lambda x: (x.astype(BF16),))

    @pl.when(c == 0)
    def _():
        for j in range(npair):
            st_ref[:, j * LANES:(j + 1) * LANES] = ssm0_ref[0, j * LANES:(j + 1) * LANES, :].T
        tail_ref[0:8, :] = conv0_ref[0]

    if valid < L:
        pzx_ref, pdt_ref = pad_refs
        pzx_ref[...] = jnp.zeros_like(pzx_ref)
        pdt_ref[...] = jnp.zeros_like(pdt_ref)
        pzx_ref[0:valid, :] = zx_ref[0]
        pdt_ref[0:valid, :] = dt_ref[0]
        z = pzx_ref[:, :D_INNER]
        u = pzx_ref[:, D_INNER:]
        dtr = pdt_ref[...]
    else:
        z = zx_ref[0, :, :D_INNER]
        u = zx_ref[0, :, D_INNER:]
        dtr = dt_ref[0]

    tail_ref[8:8 + L, :] = u
    acc = cb_ref[...] + cw_ref[3:4, :] * u
    for k in range(1, 4):
        acc = acc + cw_ref[3 - k:4 - k, :] * tail_ref[8 - k:8 - k + L, :]
    tail_ref[0:8, :] = tail_ref[valid:valid + 8, :]
    xbc_ref[...] = _silu(acc)

    lane = _iota((1, LANES), 1)
    rowl = _iota((L, 1), 0)
    xdt = dtr + dtb_ref[...]
    dt = jnp.maximum(xdt, 0.0) + jnp.log(1.0 + jnp.exp(-jnp.abs(xdt)))
    dt = jnp.where((lane < SSM_HEADS) & (rowl < valid), dt, 0.0)
    a_neg = -jnp.exp(alog_ref[...])
    da = dt * a_neg
    rr = _iota((L, L), 0)
    cc = _iota((L, L), 1)
    causal = rr >= cc
    tri = jnp.where(causal, 1.0, 0.0).astype(BF16)
    hi, mid, lo = _split3(da)
    a_cs = _dot(tri, hi) + _dot(tri, mid) + _dot(tri, lo)
    a_cs_t = a_cs.T
    dt_t = dt.T
    tdt_t = dt_t * jnp.exp(a_cs_t[:, L - 1:L] - a_cs_t)
    ea = jnp.exp(a_cs)
    half = _iota((1, LANES), 1) < SSM_HEADDIM

    for g in range(SSM_GROUPS):
        b_g = xbc_ref[:, D_INNER + g * D_STATE:D_INNER + (g + 1) * D_STATE]
        c_g = xbc_ref[:, D_INNER + SSM_GROUPS * D_STATE + g * D_STATE:
                      D_INNER + SSM_GROUPS * D_STATE + (g + 1) * D_STATE]
        cb = _mm(parts(c_g), parts(b_g), _dot_nt)
        b_t = b_g.T
        for jj in range(npair // SSM_GROUPS):
            j = g * (npair // SSM_GROUPS) + jj
            cols = slice(j * LANES, (j + 1) * LANES)
            xs_b = parts(xbc_ref[:, cols])
            st_p = st_ref[:, cols]
            rhs = tuple(jnp.concatenate([a, b], axis=0) for a, b in zip(xs_b, parts(st_p)))
            ys, ss, cds = [], [], []
            for h in (2 * j, 2 * j + 1):
                colb = jnp.broadcast_to(a_cs[:, h:h + 1], (L, LANES))
                seg = colb - a_cs_t[h:h + 1, :]
                dec = jnp.exp(jnp.where(causal, seg, -jnp.inf))
                m = cb * dec * dt_t[h:h + 1, :]
                ecol = jnp.broadcast_to(ea[:, h:h + 1], (L, LANES))
                ce = c_g * ecol
                lhs = tuple(jnp.concatenate([a, b], axis=1) for a, b in zip(parts(m), parts(ce)))
                ys.append(_mm(lhs, rhs, _dot))
                ss.append(_mm(parts(b_t * tdt_t[h:h + 1, :]), xs_b, _dot))
                cds.append(ecol[L - 1:L, :])
            yacc_ref[:, cols] = jnp.where(half, ys[0], ys[1])
            st_ref[:, cols] = jnp.where(half, st_p * cds[0] + ss[0], st_p * cds[1] + ss[1])

    y = yacc_ref[...] + d_ref[...] * xbc_ref[:, :D_INNER]
    yn = _rms(y * _silu(z), ny_ref[...])
    if valid < L:
        y_ref[0] = yn[0:valid].astype(y_ref.dtype)
    else:
        y_ref[0] = yn.astype(y_ref.dtype)

    @pl.when(c == nchunks - 1)
    def _():
        convo_ref[0] = tail_ref[0:8, :]
        for j in range(npair):
            ssmo_ref[0, j * LANES:(j + 1) * LANES, :] = st_ref[:, j * LANES:(j + 1) * LANES].T


def _ssd(zx, dtr, conv_w, conv_b, dt_bias, a_log, d_skip, norm_y, conv0, ssm0, layer, precise):
    b, l, _ = zx.shape
    if l >= CHUNK:
        lb, valid, nchunks = CHUNK, CHUNK, l // CHUNK
    else:
        lb, valid, nchunks = l, l, 1
    pad_lanes = LANES - SSM_HEADS
    dtb = jnp.pad(dt_bias.astype(F32), (0, pad_lanes)).reshape(1, LANES)
    alog = jnp.pad(a_log.astype(F32), (0, pad_lanes)).reshape(1, LANES)
    d_exp = jnp.repeat(d_skip.astype(F32), SSM_HEADDIM).reshape(1, D_INNER)
    conv0p = jnp.pad(conv0, ((0, 0), (5, 0), (0, 0)))
    ssm0f = ssm0.reshape(ssm0.shape[0], b, SSM_HEADS * SSM_HEADDIM, D_STATE)
    const = lambda bi, c: (0, 0)
    in_specs = [
        pl.BlockSpec((1, lb, ZXBC_DIM), lambda bi, c: (bi, c, 0)),
        pl.BlockSpec((1, lb, LANES), lambda bi, c: (bi, c, 0)),
        pl.BlockSpec((4, CONV_DIM), const),
        pl.BlockSpec((1, CONV_DIM), const),
        pl.BlockSpec((1, LANES), const),
        pl.BlockSpec((1, LANES), const),
        pl.BlockSpec((1, D_INNER), const),
        pl.BlockSpec((1, D_INNER), const),
        pl.BlockSpec((1, 8, CONV_DIM), lambda bi, c: (bi, 0, 0)),
        pl.BlockSpec((None, 1, D_INNER, D_STATE), lambda bi, c: (layer, bi, 0, 0)),
    ]
    out_shape = [jax.ShapeDtypeStruct((b, l, D_INNER), F32 if precise else BF16),
                 jax.ShapeDtypeStruct((b, 8, CONV_DIM), F32),
                 jax.ShapeDtypeStruct((b, D_INNER, D_STATE), F32)]
    out_specs = [pl.BlockSpec((1, lb, D_INNER), lambda bi, c: (bi, c, 0)),
                 pl.BlockSpec((1, 8, CONV_DIM), lambda bi, c: (bi, 0, 0)),
                 pl.BlockSpec((1, D_INNER, D_STATE), lambda bi, c: (bi, 0, 0))]
    scratch = [pltpu.VMEM((D_STATE, D_INNER), F32), pltpu.VMEM((8 + CHUNK, CONV_DIM), F32),
               pltpu.VMEM((CHUNK, CONV_DIM), F32), pltpu.VMEM((CHUNK, D_INNER), F32)]
    if valid < CHUNK:
        scratch += [pltpu.VMEM((CHUNK, ZXBC_DIM), F32), pltpu.VMEM((CHUNK, LANES), F32)]
    y, convo, ssmo = pl.pallas_call(
        functools.partial(_ssd_body, valid=valid, nchunks=nchunks, precise=precise),
        grid=(b, nchunks), in_specs=in_specs, out_specs=out_specs, out_shape=out_shape,
        scratch_shapes=scratch, compiler_params=_params(("parallel", "arbitrary")), name="ssd")(
            zx, dtr, conv_w, conv_b.reshape(1, CONV_DIM), dtb, alog, d_exp, norm_y.reshape(1, D_INNER),
            conv0p, ssm0f)
    return y, convo[:, 5:], ssmo.reshape(b, SSM_HEADS, SSM_HEADDIM, D_STATE)


def _router_body(x_ref, g_ref, w_ref, b_ref, xn_ref, slab_ref, cnt_ref, base_ref, su_ref):
    tm = x_ref.shape[0]

    @pl.when(pl.program_id(0) == 0)
    def _():
        base_ref[...] = jnp.zeros_like(base_ref)
        su_ref[...] = jnp.where(_iota((tm, tm), 0) < _iota((tm, tm), 1), 1.0, 0.0).astype(BF16)

    xn = _rms(x_ref[...], g_ref[...])
    _rows_to_tiles(xn_ref, xn)
    xh, xm, xl = _split3(xn)
    w3 = w_ref[...]
    rh = _dot_nt(w3, xh)
    rm = _dot_nt(w3, xm)
    rl = _dot_nt(w3, xl)
    r1, r2, r3 = ROUTER_ROWS, 2 * ROUTER_ROWS, 3 * ROUTER_ROWS
    logits = (rh[0:r1] + rm[0:r1] + rh[r1:r2] + rl[0:r1] + rm[r1:r2] + rh[r2:r3]) + b_ref[...]
    gl = [logits[i:i + 1, :] for i in range(N_EXPERT_GROUPS)]
    best = gl[0]
    sel = jnp.zeros((1, tm), I32)
    for i in range(1, N_EXPERT_GROUPS):
        upd = gl[i] > best
        sel = jnp.where(upd, i, sel)
        best = jnp.where(upd, gl[i], best)
    den = jnp.exp(gl[0] - best)
    for i in range(1, N_EXPERT_GROUPS):
        den = den + jnp.exp(gl[i] - best)
    g_w = 1.0 / den
    chosen = logits[8:8 + EXPERTS_PER_GROUP, :]
    for i in range(1, N_EXPERT_GROUPS):
        chosen = jnp.where(sel == i, logits[8 + i * EXPERTS_PER_GROUP:8 + (i + 1) * EXPERTS_PER_GROUP, :], chosen)
    idx = _iota((EXPERTS_PER_GROUP, tm), 0)
    m1 = jnp.max(chosen, axis=0, keepdims=True)
    i1 = jnp.min(jnp.where(chosen == m1, idx, EXPERTS_PER_GROUP), axis=0, keepdims=True)
    rest = jnp.where(idx == i1, -jnp.inf, chosen)
    m2 = jnp.max(rest, axis=0, keepdims=True)
    i2 = jnp.min(jnp.where(rest == m2, idx, EXPERTS_PER_GROUP), axis=0, keepdims=True)
    t = jnp.exp(m2 - m1)
    inv = 1.0 / (1.0 + t)
    w0 = inv * g_w
    w1 = (t * inv) * g_w
    e0 = sel * EXPERTS_PER_GROUP + i1
    e1 = sel * EXPERTS_PER_GROUP + i2

    eidx = _iota((N_EXPERTS, tm), 0)
    oh0 = eidx == e0
    oh1 = eidx == e1
    oh = jnp.where(oh0, 1.0, jnp.where(oh1, 1.0, 0.0))
    base = base_ref[...]
    before = _dot(oh.astype(BF16), su_ref[...]) + jnp.concatenate([base] * (tm // LANES), axis=1)
    rank0 = jnp.sum(jnp.where(oh0, before, 0.0), axis=0, keepdims=True)
    rank1 = jnp.sum(jnp.where(oh1, before, 0.0), axis=0, keepdims=True)
    new_base = base + jnp.sum(oh, axis=1, keepdims=True)
    base_ref[...] = new_base
    cnt_ref[...] = new_base

    r = _iota((8, tm), 0)
    rows = (e0.astype(F32), e1.astype(F32), w0, w1, rank0, rank1)
    slab = jnp.zeros((8, tm), F32)
    for i, v in enumerate(rows):
        slab = jnp.where(r == i, v, slab)
    slab_ref[...] = slab


def _router(x, gain, w_split, bias, layer, tm):
    n = x.shape[0]
    xn, slab, cnt = pl.pallas_call(
        _router_body, grid=(n // tm,),
        in_specs=[pl.BlockSpec((tm, D_MODEL), lambda i: (i, 0)),
                  pl.BlockSpec((1, D_MODEL), lambda i: (0, 0)),
                  pl.BlockSpec((None, ROUTER_W_ROWS, D_MODEL), lambda i: (layer, 0, 0)),
                  pl.BlockSpec((None, ROUTER_ROWS, 1), lambda i: (layer, 0, 0))],
        out_specs=[pl.BlockSpec((tm * ROW_TILES, LANES), lambda i: (i, 0)),
                   pl.BlockSpec((8, tm), lambda i: (0, i)),
                   pl.BlockSpec((N_EXPERTS, LANES), lambda i: (0, 0))],
        out_shape=[jax.ShapeDtypeStruct((n * ROW_TILES, LANES), F32), jax.ShapeDtypeStruct((8, n), F32),
                   jax.ShapeDtypeStruct((N_EXPERTS, LANES), F32)],
        scratch_shapes=[pltpu.VMEM((N_EXPERTS, LANES), F32), pltpu.VMEM((tm, tm), BF16)],
        compiler_params=_params(("arbitrary",)), name="router")(x, gain.reshape(1, D_MODEL), w_split, bias)
    return xn, slab, cnt


def _moe_plan(slabs, cnts, tm, blks):
    counts = [cnt[:, 0].astype(I32) for cnt in cnts]
    total = sum(counts)
    padded = ((total + tm - 1) // tm) * tm
    pend = jnp.cumsum(padded)
    pstart = pend - padded
    dests = []
    base = pstart
    for slab, count, blk in zip(slabs, counts, blks):
        n = slab.shape[1]
        e_ids = slab[0:2].astype(I32)
        ranks = slab[4:6].astype(I32)
        onehot = e_ids[..., None] == jnp.arange(N_EXPERTS, dtype=I32)
        dest = jnp.sum(jnp.where(onehot, base, 0), axis=-1) + ranks
        dests.append(dest.T.reshape(n // blk, 1, 2 * blk))
        base = base + count
    slots = 2 * sum(slab.shape[1] for slab in slabs) + N_EXPERTS * tm
    n_tiles = slots // tm
    n_used = (pend[-1] // tm).astype(I32)
    tile_start = jnp.arange(n_tiles, dtype=I32) * tm
    tile_e = jnp.sum((pend[None, :] <= tile_start[:, None]).astype(I32), axis=1)
    tile_e = jnp.minimum(tile_e, N_EXPERTS - 1)
    last_e = jnp.sum(jnp.where(jnp.arange(n_tiles) == n_used - 1, tile_e, 0))
    tile_e = jnp.where(jnp.arange(n_tiles) < n_used, tile_e, last_e).astype(I32)
    return dests, tile_e, n_used.reshape(1), slots, (pstart + total).astype(I32), pend.astype(I32)


def _dispatch_body(dest_ref, xn_ref, *rest, blk):
    xs_hbm, sem = rest[-2:]

    def issue(i, carry):
        src = xn_ref.at[pl.ds(pl.multiple_of(i * ROW_TILES, ROW_TILES), ROW_TILES)]
        pltpu.make_async_copy(src, xs_hbm.at[dest_ref[0, 0, 2 * i]], sem).start()
        pltpu.make_async_copy(src, xs_hbm.at[dest_ref[0, 0, 2 * i + 1]], sem).start(priority=1)
        return carry

    lax.fori_loop(0, blk, issue, 0, unroll=8)

    def drain(i, carry):
        pltpu.make_async_copy(xn_ref.at[pl.ds(0, ROW_TILES)], xs_hbm.at[0], sem).wait()
        return carry

    lax.fori_loop(0, 2 * blk, drain, 0, unroll=8)


def _dispatch(xn, dest, slots, blk, xs=None):
    n = xn.shape[0] // ROW_TILES
    in_specs = [pl.BlockSpec((1, 1, 2 * blk), lambda i: (i, 0, 0), memory_space=pltpu.SMEM),
                pl.BlockSpec((blk * ROW_TILES, LANES), lambda i: (i, 0))]
    args = [dest, xn]
    if xs is not None:
        in_specs.append(pl.BlockSpec(memory_space=pl.ANY))
        args.append(xs)
    return pl.pallas_call(
        functools.partial(_dispatch_body, blk=blk), grid=(n // blk,), in_specs=in_specs,
        out_specs=pl.BlockSpec(memory_space=pl.ANY),
        out_shape=jax.ShapeDtypeStruct((slots, ROW_TILES, LANES), F32),
        scratch_shapes=[pltpu.SemaphoreType.DMA(())],
        input_output_aliases={} if xs is None else {2: 0},
        compiler_params=_params(("arbitrary",)), name="dispatch")(*args)


def _pad_zero_body(lo_ref, hi_ref, xs_in_hbm, xs_hbm, z_ref, sem):
    del xs_in_hbm
    z_ref[...] = jnp.zeros_like(z_ref)
    sizes = [1 << s for s in range(MOE_TILE.bit_length() - 2, -1, -1)]

    def copies(e, act):
        pos = lo_ref[e]
        n = hi_ref[e] - pos
        for size in sizes:
            take = (n & size) != 0

            @pl.when(take)
            def _(pos=pos, size=size):
                act(pltpu.make_async_copy(z_ref.at[pl.ds(0, size)], xs_hbm.at[pl.ds(pos, size)], sem))

            pos = pos + jnp.where(take, size, 0)

    def issue(e, carry):
        copies(e, lambda cp: cp.start())
        return carry

    def drain(e, carry):
        copies(e, lambda cp: cp.wait())
        return carry

    lax.fori_loop(0, N_EXPERTS, issue, 0)
    lax.fori_loop(0, N_EXPERTS, drain, 0)


def _pad_zero(xs, lo, hi):
    grid_spec = pltpu.PrefetchScalarGridSpec(
        num_scalar_prefetch=2, grid=(1,),
        in_specs=[pl.BlockSpec(memory_space=pl.ANY)], out_specs=pl.BlockSpec(memory_space=pl.ANY),
        scratch_shapes=[pltpu.VMEM((MOE_TILE // 2, ROW_TILES, LANES), F32), pltpu.SemaphoreType.DMA(())])
    return pl.pallas_call(
        _pad_zero_body, grid_spec=grid_spec, out_shape=jax.ShapeDtypeStruct(xs.shape, xs.dtype),
        input_output_aliases={2: 0},
        compiler_params=_params(("arbitrary",)), name="pad_zero")(lo, hi, xs)


def _moe_body(te_ref, nu_ref, xs_ref, wg_ref, wu_ref, wd_ref, o_ref, wgb_ref, wub_ref, wdb_ref, *, tm):
    t = pl.program_id(0)
    e = te_ref[t]
    prev = te_ref[jnp.maximum(t - 1, 0)]

    @pl.when((t == 0) | (e != prev))
    def _():
        wgb_ref[...] = wg_ref[0].astype(BF16)
        wub_ref[...] = wu_ref[0].astype(BF16)
        wdb_ref[...] = wd_ref[0].astype(BF16)

    @pl.when(t < nu_ref[0])
    def _():
        x = _tiles_to_rows(xs_ref, tm).astype(BF16)
        g = _dot(x, wgb_ref[...])
        u = _dot(x, wub_ref[...])
        hid = _silu(g) * u
        _rows_to_tiles(o_ref, _dot(hid.astype(BF16), wdb_ref[...]))


def _moe_experts(xs, tile_e, n_used, w_gate, w_up, w_down, layer, tm):
    slots = xs.shape[0] // ROW_TILES
    n_tiles = slots // tm
    row = lambda t, te, nu: (jnp.minimum(t, nu[0] - 1), 0)
    wsel = lambda t, te, nu: (layer, te[t], 0, 0)
    grid_spec = pltpu.PrefetchScalarGridSpec(
        num_scalar_prefetch=2, grid=(n_tiles,),
        in_specs=[pl.BlockSpec((tm * ROW_TILES, LANES), row),
                  pl.BlockSpec((None, 1, D_MODEL, D_EXPERT), wsel),
                  pl.BlockSpec((None, 1, D_MODEL, D_EXPERT), wsel),
                  pl.BlockSpec((None, 1, D_EXPERT, D_MODEL), wsel)],
        out_specs=pl.BlockSpec((tm * ROW_TILES, LANES), row),
        scratch_shapes=[pltpu.VMEM((D_MODEL, D_EXPERT), BF16), pltpu.VMEM((D_MODEL, D_EXPERT), BF16),
                        pltpu.VMEM((D_EXPERT, D_MODEL), BF16)])
    return pl.pallas_call(
        functools.partial(_moe_body, tm=tm), grid_spec=grid_spec,
        out_shape=jax.ShapeDtypeStruct((slots * ROW_TILES, LANES), F32),
        compiler_params=_params(("arbitrary",)), name="moe_experts")(tile_e, n_used, xs, w_gate, w_up, w_down)


def _ple_body(*refs, final, tm):
    dest_ref, x_ref, gw_ref, p_ref, g_ref, wg_ref, wp_ref = refs[:7]
    if final:
        fg_ref, ys_hbm, o_ref, f_ref, y0_ref, y1_ref, sem = refs[7:]
    else:
        ys_hbm, o_ref, y0_ref, y1_ref, sem = refs[7:]

    def issue(i, carry):
        row = pl.multiple_of(i * ROW_TILES, ROW_TILES)
        pltpu.make_async_copy(ys_hbm.at[dest_ref[0, 0, 2 * i]], y0_ref.at[pl.ds(row, ROW_TILES)], sem).start()
        pltpu.make_async_copy(ys_hbm.at[dest_ref[0, 0, 2 * i + 1]], y1_ref.at[pl.ds(row, ROW_TILES)], sem).start(priority=1)
        return carry

    lax.fori_loop(0, tm, issue, 0, unroll=8)
    pe = _dot(p_ref[...].astype(BF16), wp_ref[...])

    def drain(i, carry):
        pltpu.make_async_copy(ys_hbm.at[0], y0_ref.at[pl.ds(0, ROW_TILES)], sem).wait()
        return carry

    lax.fori_loop(0, 2 * tm, drain, 0, unroll=8)
    gw = gw_ref[...]
    x2 = x_ref[...] + gw[:, 2:3] * _tiles_to_rows(y0_ref, tm) + gw[:, 3:4] * _tiles_to_rows(y1_ref, tm)
    xn = _rms(x2, g_ref[...])
    gate = _sigmoid(_dot(xn.astype(BF16), wg_ref[...]))
    out = x2 + pe * gate
    o_ref[...] = out
    if final:
        f_ref[...] = _rms(out, fg_ref[...])


def _ple(x, ys, dest, gw, p, gain, w_gate, w_proj, layer, final_gain, tm):
    n = x.shape[0]
    dp = p.shape[-1]
    final = final_gain is not None
    rowspec = pl.BlockSpec((tm, D_MODEL), lambda i: (i, 0))
    vec = pl.BlockSpec((1, D_MODEL), lambda i: (0, 0))
    in_specs = [pl.BlockSpec((1, 1, 2 * tm), lambda i: (i, 0, 0), memory_space=pltpu.SMEM),
                rowspec, pl.BlockSpec((tm, 8), lambda i: (i, 0)),
                pl.BlockSpec((None, tm, dp), lambda i: (layer, i, 0)), vec,
                pl.BlockSpec((None, D_MODEL, D_MODEL), lambda i: (layer, 0, 0)),
                pl.BlockSpec((None, dp, D_MODEL), lambda i: (layer, 0, 0))]
    args = [dest, x, gw, p, gain.reshape(1, D_MODEL), w_gate, w_proj]
    out_shape = [jax.ShapeDtypeStruct((n, D_MODEL), F32)]
    out_specs = [rowspec]
    if final:
        in_specs.append(vec)
        args.append(final_gain.reshape(1, D_MODEL))
        out_shape.append(jax.ShapeDtypeStruct((n, D_MODEL), F32))
        out_specs.append(rowspec)
    in_specs.append(pl.BlockSpec(memory_space=pl.ANY))
    args.append(ys.reshape(-1, ROW_TILES, LANES))
    outs = pl.pallas_call(
        functools.partial(_ple_body, final=final, tm=tm), grid=(n // tm,), in_specs=in_specs,
        out_specs=out_specs, out_shape=out_shape,
        scratch_shapes=[pltpu.VMEM((tm * ROW_TILES, LANES), F32), pltpu.VMEM((tm * ROW_TILES, LANES), F32),
                        pltpu.SemaphoreType.DMA(())],
        compiler_params=_params(("arbitrary",)), name="ple")(*args)
    return (outs[0], outs[1]) if final else (outs[0], None)


ATTN_GROUP = 8
LOG2E = 1.4426950408889634


def _attn_group(qs, kcats, vcats, biases):
    R = CHUNK
    s = jnp.concatenate([_dot_nt(q, kcat) * (HEAD_DIM ** -0.5 * LOG2E) + bias
                         for q, kcat, bias in zip(qs, kcats, biases)], axis=0)
    m = jnp.max(s, axis=-1, keepdims=True)
    p = jnp.exp2(s - m)
    l = jnp.sum(p, axis=-1, keepdims=True)
    pb = p.astype(BF16)
    inv = 1.0 / l
    lse = m * (1.0 / LOG2E) + jnp.log(l)
    outs, lses = [], []
    for i, vcat in enumerate(vcats):
        sl = slice(i * R, (i + 1) * R)
        outs.append(_dot(pb[sl], vcat) * inv[sl])
        lses.append(jnp.broadcast_to(lse[sl], (R, LANES)))
    return outs, lses


def _attn_body(q0_ref, q1_ref, q2_ref, k_ref, v_ref, o_ref, kp_ref, vp_ref, og_ref, lg_ref, bias_ref):
    R = CHUNK
    row = _iota((R, 2 * R), 0)
    col = _iota((R, 2 * R), 1)
    first_key = jnp.where(pl.program_id(2) > 0, 0, R)
    bias_ref[0] = jnp.where((col >= row) & (col <= row + R), 0.0, -jnp.inf)
    bias_ref[1] = jnp.where((col >= jnp.maximum(row, first_key)) & (col <= row + R), 0.0, -jnp.inf)

    @pl.when(pl.program_id(2) == 0)
    def _():
        kp_ref[...] = jnp.zeros_like(kp_ref)
        vp_ref[...] = jnp.zeros_like(vp_ref)

    def rows(ref, start, dil):
        idx = pl.ds(start, R) if dil == 1 else pl.ds(start, R, stride=dil)
        return ref[idx, :]

    def put(ref, start, dil, val):
        idx = pl.ds(start, R) if dil == 1 else pl.ds(start, R, stride=dil)
        ref[idx, :] = val

    def group(g, q_ref, dil, blocks):
        nblk = ATTN_SPAN // (R * dil)
        qs, kcats, vcats, fks, starts = [], [], [], [], []
        for r, cb in blocks:
            start = r + dil * R * cb
            if cb == 0:
                prev_k, prev_v, prev_start, fk = kp_ref, vp_ref, r + dil * R * (nblk - 1), bias_ref[1]
            else:
                prev_k, prev_v, prev_start, fk = k_ref, v_ref, start - dil * R, bias_ref[0]
            kcats.append(jnp.concatenate([rows(prev_k, prev_start, dil).astype(BF16),
                                          rows(k_ref, start, dil).astype(BF16)], axis=0))
            vcats.append(jnp.concatenate([rows(prev_v, prev_start, dil).astype(BF16),
                                          rows(v_ref, start, dil).astype(BF16)], axis=0))
            qs.append(rows(q_ref, start, dil).astype(BF16))
            fks.append(fk)
            starts.append(start)
        outs, lses = _attn_group(qs, kcats, vcats, fks)
        for start, o, lse in zip(starts, outs, lses):
            put(og_ref.at[g], start, dil, o)
            put(lg_ref.at[g], start, dil, lse)

    for g, (q_ref, (_, dil)) in enumerate(zip((q0_ref, q1_ref, q2_ref), PATTERNS)):
        nblk = ATTN_SPAN // (R * dil)
        if nblk >= ATTN_GROUP:
            def lane(r, carry, q_ref=q_ref, dil=dil, nblk=nblk, g=g):
                for c0 in range(0, nblk, ATTN_GROUP):
                    group(g, q_ref, dil, [(r, cb) for cb in range(c0, c0 + ATTN_GROUP)])
                return carry
            n_iter = dil
        else:
            lpg = ATTN_GROUP // nblk

            def lane(i, carry, q_ref=q_ref, dil=dil, nblk=nblk, g=g, lpg=lpg):
                group(g, q_ref, dil, [(i * lpg + j, cb) for j in range(lpg) for cb in range(nblk)])
                return carry
            n_iter = dil // lpg
        if n_iter == 1:
            lane(0, 0)
        else:
            lax.fori_loop(0, n_iter, lane, 0)

    step = 256
    for i in range(ATTN_SPAN // step):
        sl = pl.ds(i * step, step)
        l0 = lg_ref[0, sl, :]
        l1 = lg_ref[1, sl, :]
        l2 = lg_ref[2, sl, :]
        m = jnp.maximum(jnp.maximum(l0, l1), l2)
        e0 = jnp.exp(l0 - m)
        e1 = jnp.exp(l1 - m)
        e2 = jnp.exp(l2 - m)
        num = e0 * og_ref[0, sl, :] + e1 * og_ref[1, sl, :] + e2 * og_ref[2, sl, :]
        o_ref[sl, :] = num * (1.0 / (e0 + e1 + e2))
    kp_ref[...] = k_ref[...]
    vp_ref[...] = v_ref[...]


def _attn_prompt(q, k, v, b, s):
    nspan = s // ATTN_SPAN
    blk = (ATTN_SPAN, HEAD_DIM)
    hblk = (None,) + blk
    qspec = lambda g: pl.BlockSpec(hblk, lambda bi, h, c: (g * N_KV_HEADS + h, bi * nspan + c, 0))
    kvspec = pl.BlockSpec(hblk, lambda bi, h, c: (h, bi * nspan + c, 0))
    return pl.pallas_call(
        _attn_body, grid=(b, N_KV_HEADS, nspan),
        in_specs=[qspec(0), qspec(1), qspec(2), kvspec, kvspec],
        out_specs=kvspec, out_shape=jax.ShapeDtypeStruct((N_KV_HEADS, b * s, HEAD_DIM), F32),
        scratch_shapes=[pltpu.VMEM(blk, F32), pltpu.VMEM(blk, F32),
                        pltpu.VMEM((3,) + blk, F32), pltpu.VMEM((3,) + blk, F32),
                        pltpu.VMEM((2, CHUNK, 2 * CHUNK), F32)],
        compiler_params=_params(("parallel", "parallel", "arbitrary")), name="attn_prompt")(q, q, q, k, v)


SAMPLE_ROWS = 16
DIL_MAX = PATTERNS[-1][1]
DENSE_TAIL = PATTERNS[1][0]


def _attn_sample_body(q_ref, kf_ref, vf_ref, kt_ref, vt_ref, kn_ref, vn_ref, bias_ref, o_ref, *, t_new):
    nq = N_KV_HEADS * SAMPLE_ROWS
    n_far = kf_ref.shape[1] * kf_ref.shape[2]
    n_tail = kt_ref.shape[1]
    q = q_ref[0].astype(BF16)
    kf = kf_ref[0].reshape(n_far, HEAD_DIM).astype(BF16)
    vf = vf_ref[0].reshape(n_far, HEAD_DIM).astype(BF16)
    s = jnp.concatenate([_dot_nt(q, kf), _dot_nt(q, kt_ref[0].astype(BF16)), _dot_nt(q, kn_ref[0].astype(BF16))],
                        axis=1) * (HEAD_DIM ** -0.5) + bias_ref[...]
    m = jnp.max(s, axis=-1, keepdims=True)
    p = jnp.exp(s - m)
    l = jnp.sum(p, axis=-1, keepdims=True)
    pb = p.astype(BF16)
    o = (_dot(pb[:, :n_far], vf) + _dot(pb[:, n_far:n_far + n_tail], vt_ref[0].astype(BF16))
         + _dot(pb[:, n_far + n_tail:], vn_ref[0].astype(BF16))) * (1.0 / l)
    lse = jnp.broadcast_to(m + jnp.log(l), (nq, LANES))
    lse1 = pltpu.roll(lse, nq - t_new, 0)
    lse2 = pltpu.roll(lse, nq - 2 * t_new, 0)
    o1 = pltpu.roll(o, nq - t_new, 0)
    o2 = pltpu.roll(o, nq - 2 * t_new, 0)
    mm = jnp.maximum(jnp.maximum(lse, lse1), lse2)
    e0 = jnp.exp(lse - mm)
    e1 = jnp.exp(lse1 - mm)
    e2 = jnp.exp(lse2 - mm)
    comb = (e0 * o + e1 * o1 + e2 * o2) * (1.0 / (e0 + e1 + e2))
    for h in range(N_KV_HEADS):
        o_ref[0, :, h * HEAD_DIM:(h + 1) * HEAD_DIM] = comb[h * SAMPLE_ROWS:h * SAMPLE_ROWS + t_new]


def _sample_bias(wb, t_new, far):
    rho = jnp.arange(N_KV_HEADS * SAMPLE_ROWS, dtype=I32)
    h = rho // SAMPLE_ROWS
    gt = rho % SAMPLE_ROWS
    g = jnp.where(gt < len(PATTERNS) * t_new, gt // t_new, 0)
    t = gt % t_new
    win = jnp.asarray([w for w, _ in PATTERNS], I32)[g]
    dil = jnp.asarray([d for _, d in PATTERNS], I32)[g]
    far_rows = t_new * N_KV_HEADS
    c = jnp.arange(far // DIL_MAX * far_rows, dtype=I32)
    pos_far = DIL_MAX * (c // far_rows) + (c % far_rows) // N_KV_HEADS
    c = jnp.arange((wb - far) * N_KV_HEADS, dtype=I32)
    pos_tail = far + c // N_KV_HEADS
    c = jnp.arange(LANES, dtype=I32)
    pos_new = jnp.where(c < far_rows, wb + c // N_KV_HEADS, -1)
    pos = jnp.concatenate([pos_far, pos_tail, pos_new])
    head = jnp.concatenate([jnp.arange(n, dtype=I32) % N_KV_HEADS for n in (pos_far.shape[0], pos_tail.shape[0], LANES)])
    dist = wb + t[:, None] - pos[None, :]
    valid = ((head[None, :] == h[:, None]) & (pos[None, :] >= 0) & (dist >= 0) & (dist <= win[:, None])
             & (dist % dil[:, None] == 0))
    return jnp.where(valid, 0.0, -jnp.inf).astype(F32)


def _attn_sample(q, cache_k, cache_v, k_new, v_new, b, t_new):
    wb = cache_k.shape[1]
    tail = min(DENSE_TAIL, wb)
    far = wb - tail
    far_rows = t_new * N_KV_HEADS
    nq = N_KV_HEADS * SAMPLE_ROWS
    qh = q.reshape(b, t_new, 3, N_KV_HEADS, HEAD_DIM).transpose(0, 3, 2, 1, 4).reshape(b, N_KV_HEADS, 3 * t_new, HEAD_DIM)
    qh = jnp.pad(qh, ((0, 0), (0, 0), (0, SAMPLE_ROWS - 3 * t_new), (0, 0))).reshape(b, nq, HEAD_DIM)
    new = lambda x: jnp.pad(x.reshape(b, far_rows, HEAD_DIM), ((0, 0), (0, LANES - far_rows), (0, 0)))
    grouped = lambda x: x.reshape(b, wb // DIL_MAX, DIL_MAX * N_KV_HEADS, HEAD_DIM)
    flat = lambda x: x.reshape(b, wb * N_KV_HEADS, HEAD_DIM)
    bias = _sample_bias(wb, t_new, far)
    far_spec = pl.BlockSpec((1, far // DIL_MAX, far_rows, HEAD_DIM), lambda bi: (bi, 0, 0, 0))
    tail_spec = pl.BlockSpec((1, tail * N_KV_HEADS, HEAD_DIM), lambda bi: (bi, wb // tail - 1, 0))
    sq_spec = pl.BlockSpec((1, LANES, HEAD_DIM), lambda bi: (bi, 0, 0))
    o = pl.pallas_call(
        functools.partial(_attn_sample_body, t_new=t_new), grid=(b,),
        in_specs=[sq_spec, far_spec, far_spec, tail_spec, tail_spec, sq_spec, sq_spec,
                  pl.BlockSpec(bias.shape, lambda bi: (0, 0))],
        out_specs=pl.BlockSpec((1, t_new, ATTN_DIM), lambda bi: (bi, 0, 0)),
        out_shape=jax.ShapeDtypeStruct((b, t_new, ATTN_DIM), F32),
        compiler_params=_params(("parallel",)), name="attn_sample")(
            qh, grouped(cache_k), grouped(cache_v), flat(cache_k), flat(cache_v), new(k_new), new(v_new), bias)
    return o.reshape(b * t_new, ATTN_DIM)


def _rope_tables(pos0, l, reps):
    inv_freq = ROPE_THETA ** (-jnp.arange(0, HEAD_DIM, 2, dtype=F32) / HEAD_DIM)
    pos = pos0 + jnp.arange(l, dtype=jnp.int32)
    ang = pos.astype(F32)[:, None] * inv_freq[None, :]
    cos = jnp.cos(ang)
    sin = jnp.sin(ang)
    cosf = jnp.concatenate([cos, cos], axis=-1)
    sinf = jnp.concatenate([-sin, sin], axis=-1)
    if reps > 1:
        cosf = jnp.tile(cosf, (reps, 1))
        sinf = jnp.tile(sinf, (reps, 1))
    return cosf, sinf


def _prep_weights(prm):
    w = {}
    w['in_proj'], w['in_proj_lo'] = _split2(prm['ssm_in_proj'])
    w['in_dt'], w['in_dt_lo'] = _split2(
        jnp.pad(prm['ssm_in_proj'][:, :, ZXBC_DIM:], ((0, 0), (0, 0), (0, LANES - SSM_HEADS))))
    w['out_proj'], w['out_proj_lo'] = _split2(prm['ssm_out_proj'])
    w['w_kv'] = prm['w_kv'].astype(BF16)
    w['w_q'] = prm['attn_w_q'].astype(BF16)
    w['w_o'] = prm['attn_w_o'].astype(BF16)
    w['ple_gate'] = prm['ple_w_gate'].astype(BF16)
    w['ple_proj'] = prm['ple_w_proj'].astype(BF16)
    wr = jnp.concatenate([jnp.swapaxes(prm['moe_w_rg'], 1, 2),
                          jnp.zeros((DEPTH, 8 - N_EXPERT_GROUPS, D_MODEL), F32),
                          jnp.swapaxes(prm['moe_w_re'], 1, 2)], axis=1)
    w['router'] = jnp.concatenate(_split3(wr) + (jnp.zeros((DEPTH, ROUTER_W_ROWS - 3 * ROUTER_ROWS, D_MODEL), BF16),),
                                  axis=1)
    w['router_b'] = jnp.concatenate([prm['moe_b_rg'], jnp.zeros((DEPTH, 8 - N_EXPERT_GROUPS), F32),
                                     prm['moe_b_re']], axis=1).reshape(DEPTH, ROUTER_ROWS, 1)
    return w


def _trunk_state(x, p, pos0, conv_state, ssm_state, cache_k, cache_v):
    b, l, _ = x.shape
    n = b * l
    tm = min(512, n)
    return dict(b=b, l=l, n=n, tm=tm, tp=min(PROJ_ROWS, n), sample=cache_k is not None, xf=x.reshape(n, D_MODEL),
                p=p.reshape(DEPTH, n, -1), rope=_rope_tables(pos0, l, 1 if l >= tm else tm // l),
                conv_state=conv_state, ssm_state=ssm_state, cache_k=cache_k, cache_v=cache_v,
                new_conv=[], new_ssm=[], k32=None, v32=None, kh=None, vh=None, y_final=None)


def _mixer(t, i, prm, w):
    b, l, n, tm, tp, xf = t['b'], t['l'], t['n'], t['tm'], t['tp'], t['xf']
    gain = prm['norm_mix'][i]
    if i < N_A_LAYERS:
        precise = i < PRECISE_MIXERS
        lo = (lambda name: w[name + '_lo']) if precise else (lambda name: None)
        zx = _proj(xf, w['in_proj'], w_lo=lo('in_proj'), layer=i, gain=gain, nout=ZXBC_DIM, tm=tp, tn=1024)
        dtr = _proj(xf, w['in_dt'], w_lo=lo('in_dt'), layer=i, gain=gain, tm=tp, tn=LANES)
        y, cbuf, hs = _ssd(zx.reshape(b, l, ZXBC_DIM), dtr.reshape(b, l, LANES), prm['ssm_conv_w'][i],
                           prm['ssm_conv_b'][i], prm['ssm_dt_bias'][i], prm['ssm_a_log'][i], prm['ssm_d'][i],
                           prm['ssm_norm_y'][i], t['conv_state'][i], t['ssm_state'], i, precise)
        t['new_conv'].append(cbuf)
        t['new_ssm'].append(hs)
        t['xf'] = _proj(y.reshape(n, D_INNER), w['out_proj'], w_lo=lo('out_proj'), layer=i, residual=xf,
                        tm=tm, tn=1024)
    else:
        j = i - N_A_LAYERS
        q = _proj(xf, w['w_q'], layer=j, gain=gain, rope=t['rope'], out_heads=(not t['sample'],), tm=tp, tn=1024)
        if t['sample']:
            att = _attn_sample(q, t['cache_k'], t['cache_v'], t['k32'], t['v32'], b, l)
        else:
            att = _attn_prompt(q, t['kh'], t['vh'], b, l)
        t['xf'] = _proj(att, w['w_o'], layer=j, residual=xf, tm=tp, tn=1024)


def _shared_kv(t, prm, w):
    tm, xf = t['tp'], t['xf']
    kv_out = dict(out_dtypes=(F32,), out_heads=(False,)) if t['sample'] else \
        dict(out_dtypes=(F32, F32), out_heads=(False, True))
    k_outs = _proj(xf, w['w_kv'], gain=prm['norm_kv'], rope=t['rope'], nout=ATTN_DIM, tm=tm, tn=1024, **kv_out)
    v_outs = _proj(xf, w['w_kv'], col0=ATTN_DIM, gain=prm['norm_kv'], nout=ATTN_DIM, tm=tm, tn=1024, **kv_out)
    (t['k32'], t['kh']), (t['v32'], t['vh']) = ((k_outs, None), (v_outs, None)) if t['sample'] else (k_outs, v_outs)


def _moe_ple(trunks, i, prm, w):
    routed = [_router(t['xf'], prm['norm_ffn'][i], w['router'], w['router_b'], i, t['tp']) for t in trunks]
    dests, tile_e, n_used, slots, pad_lo, pad_hi = _moe_plan([r[1] for r in routed], [r[2] for r in routed],
                                                             MOE_TILE, [t['tp'] for t in trunks])
    xs = None
    for t, (xn, _, _), dest in zip(trunks, routed, dests):
        xs = _dispatch(xn, dest, slots, t['tp'], xs)
    xs = _pad_zero(xs, pad_lo, pad_hi)
    ys = _moe_experts(xs.reshape(slots * ROW_TILES, LANES), tile_e, n_used, prm['moe_w_gate'], prm['moe_w_up'],
                      prm['moe_w_down'], i, MOE_TILE)
    for t, (_, slab, _), dest in zip(trunks, routed, dests):
        t['xf'], t['y_final'] = _ple(t['xf'], ys, dest, slab.T, t['p'], prm['ple_norm'][i], w['ple_gate'],
                                     w['ple_proj'], i, prm['norm_final'] if i == DEPTH - 1 else None, t['tp'])


def _run(trunks, prm, w):
    for i in range(DEPTH):
        for t in trunks:
            _mixer(t, i, prm, w)
        _moe_ple(trunks, i, prm, w)
        if i == N_A_LAYERS - 1:
            for t in trunks:
                _shared_kv(t, prm, w)
    return [(t['y_final'].reshape(t['b'], t['l'], D_MODEL), jnp.stack(t['new_conv']), jnp.stack(t['new_ssm']),
             t['k32'].reshape(t['b'], t['l'], ATTN_DIM), t['v32'].reshape(t['b'], t['l'], ATTN_DIM))
            for t in trunks]


def kernel(x_prompt, x_sample, state_ssm, state_conv, cache_k, cache_v, p_prompt, p_sample, norm_mix, ssm_in_proj, ssm_conv_w, ssm_conv_b, ssm_dt_bias, ssm_a_log, ssm_d, ssm_norm_y, ssm_out_proj, norm_kv, w_kv, attn_w_q, attn_w_o, norm_ffn, moe_w_rg, moe_b_rg, moe_w_re, moe_b_re, moe_w_gate, moe_w_up, moe_w_down, ple_w_proj, ple_w_gate, ple_norm, norm_final):
    prm = dict(norm_mix=norm_mix, ssm_in_proj=ssm_in_proj, ssm_conv_w=ssm_conv_w, ssm_conv_b=ssm_conv_b,
               ssm_dt_bias=ssm_dt_bias, ssm_a_log=ssm_a_log, ssm_d=ssm_d, ssm_norm_y=ssm_norm_y,
               ssm_out_proj=ssm_out_proj, norm_kv=norm_kv, w_kv=w_kv, attn_w_q=attn_w_q, attn_w_o=attn_w_o,
               norm_ffn=norm_ffn, moe_w_rg=moe_w_rg, moe_b_rg=moe_b_rg, moe_w_re=moe_w_re, moe_b_re=moe_b_re,
               moe_w_gate=moe_w_gate, moe_w_up=moe_w_up, moe_w_down=moe_w_down, ple_w_proj=ple_w_proj,
               ple_w_gate=ple_w_gate, ple_norm=ple_norm, norm_final=norm_final)
    w = _prep_weights(prm)
    b, s, _ = x_prompt.shape
    db, dl, _ = x_sample.shape
    past_len = 8192
    conv0 = jnp.zeros((N_A_LAYERS, b, 3, CONV_DIM), x_prompt.dtype)
    ssm0 = jnp.zeros((N_A_LAYERS, b, SSM_HEADS, SSM_HEADDIM, D_STATE), state_ssm.dtype)
    trunks = [_trunk_state(x_prompt, p_prompt, 0, conv0, ssm0, None, None),
              _trunk_state(x_sample, p_sample, past_len, state_conv, state_ssm, cache_k, cache_v)]
    (y_p, conv_p, ssm_p, k_p, v_p), (y_s, conv_s, ssm_s, k_s, v_s) = _run(trunks, prm, w)
    keep = min(2048, s)
    heads = lambda t: t.reshape(t.shape[0], t.shape[1], N_KV_HEADS, HEAD_DIM)
    k_p = heads(k_p[:, s - keep:])
    v_p = heads(v_p[:, s - keep:])
    return (y_p, y_s, ssm_p, conv_p, k_p, v_p, ssm_s, conv_s, heads(k_s), heads(v_s))
```
